```python
import jax, jax.numpy as jnp
from jax import lax
import numpy as np

D_MODEL = 2048
BATCH = 32
SEQ = 256
DEPTH = 2
DEC_BATCH = 8
DEC_SEQ = 1024
PAST_LEN = 512

GRID_W = 64
CONV_DIM = 1024
CONV_WIDTH = 31
GLA_HEADS = 4
GLA_DK = 128
GLA_DV = 256
GLA_QK = GLA_HEADS * GLA_DK
GLA_V = GLA_HEADS * GLA_DV
GLA_GATE_RANK = 16
GLA_TAU = 16.0
GLA_CHUNK = 64
MLA_HEADS = 8
Q_RANK = 512
KV_RANK = 512
QK_NOPE = 128
QK_ROPE = 64
V_HEAD = 128
ROPE_BASE = 10000.0
Q_BLOCK = 128
N_EXPERTS = 32
TOP_K = 4
D_FF = 2048
SWIGLU_LIMIT = 7.0
SWIGLU_ALPHA = 1.702
MOE_BLOCK = 128
LN_EPS = 1e-5
RMS_EPS = 1e-6
DEEPNORM_ALPHA = (2 * DEPTH) ** 0.25
DEEPNORM_BETA = (8 * DEPTH) ** -0.25
IN_SIZES = (CONV_DIM, CONV_DIM,
            GLA_QK, GLA_QK, GLA_V, GLA_V,
            GLA_GATE_RANK, GLA_GATE_RANK,
            Q_RANK, KV_RANK, QK_ROPE,
            D_MODEL, D_MODEL, D_MODEL)
IN_DIM = sum(IN_SIZES)

kernel_name = 'hybrid_diffusion_conv_gla_mla_moe_step'

F32 = jnp.float32


def _layernorm(x, g=None, b=None):
    xf = x.astype(F32)
    mu = jnp.mean(xf, axis=-1, keepdims=True)
    xc = xf - mu
    y = xc * lax.rsqrt(jnp.mean(xc * xc, axis=-1, keepdims=True) + LN_EPS)
    if g is not None:
        y = y * g.astype(F32) + b.astype(F32)
    return y.astype(x.dtype)


def _rmsnorm(x, g):
    xf = x.astype(F32)
    y = xf * lax.rsqrt(jnp.mean(xf * xf, axis=-1, keepdims=True) + RMS_EPS) * g.astype(F32)
    return y.astype(x.dtype)


def _axial_rope_cos_sin(n_tokens):
    rows = n_tokens // GRID_W
    row = jnp.repeat(jnp.arange(rows), GRID_W).astype(F32)
    col = jnp.tile(jnp.arange(GRID_W), rows).astype(F32)
    n_freq = QK_ROPE // 4
    inv = ROPE_BASE ** (-jnp.arange(n_freq, dtype=F32) / n_freq)
    ar = row[:, None] * inv
    ac = col[:, None] * inv
    ang = jnp.concatenate([ar, ar, ac, ac], axis=-1)
    return jnp.cos(ang), jnp.sin(ang)


def _rotate_half_axial(x):
    x1, x2, x3, x4 = jnp.split(x, 4, axis=-1)
    return jnp.concatenate([-x2, x1, -x4, x3], axis=-1)


def _apply_axial_rope(x, cos, sin):
    xf = x.astype(F32)
    return (xf * cos + _rotate_half_axial(xf) * sin).astype(x.dtype)


def _conv_branch(a, g, w_dw, b_dw, ln_g, ln_b, w_out):
    u = a * jax.nn.sigmoid(g)
    pad = CONV_WIDTH // 2
    y = lax.conv_general_dilated(u, w_dw[:, None, :].astype(u.dtype), window_strides=(1,),
                                 padding=((pad, pad),), dimension_numbers=('NWC', 'WIO', 'NWC'),
                                 feature_group_count=CONV_DIM) + b_dw
    y = jax.nn.silu(_layernorm(y, ln_g, ln_b))
    return y @ w_out


def _gla_chunk_scan(q, k, v, log_a, s0):
    B, H, T, DK = q.shape
    DV = v.shape[-1]
    n = T // GLA_CHUNK
    q, k, log_a = (t.reshape(B, H, n, GLA_CHUNK, DK) for t in (q, k, log_a))
    v = v.reshape(B, H, n, GLA_CHUNK, DV)
    b = jnp.cumsum(log_a, axis=3)
    b_last = b[:, :, :, -1:, :]
    q_d = q * jnp.exp(b)
    k_d = k * jnp.exp(-b)
    k_end = k * jnp.exp(b_last - b)
    mask = jnp.tril(jnp.ones((GLA_CHUNK, GLA_CHUNK), dtype=bool))
    att = jnp.where(mask, jnp.einsum('bhnid,bhnjd->bhnij', q_d, k_d), 0.0)
    o_intra = jnp.einsum('bhnij,bhnjv->bhniv', att, v)
    chunk_kv = jnp.einsum('bhnjd,bhnjv->bhndv', k_end, v)
    decay = jnp.exp(b_last[:, :, :, 0, :])

    def step(s, inp):
        dec, kv = inp
        return dec[..., None] * s + kv, s

    s_fin, s_prev = lax.scan(step, s0, (jnp.moveaxis(decay, 2, 0), jnp.moveaxis(chunk_kv, 2, 0)))
    s_prev = jnp.moveaxis(s_prev, 0, 2)
    o_inter = jnp.einsum('bhnid,bhndv->bhniv', q_d, s_prev)
    return (o_intra + o_inter).reshape(B, H, T, DV), s_fin


def _gla_branch(q, k, v, r, z_f, z_b, w_gf, b_gf, w_gb, b_gb, norm_g, w_out, s0_f, s0_b):
    B, T, _ = q.shape

    def heads(t, d):
        return t.reshape(B, T, GLA_HEADS, d).transpose(0, 2, 1, 3).astype(F32)

    qh = heads(q, GLA_DK) * (GLA_DK ** -0.5)
    kh = heads(k, GLA_DK)
    vh = heads(v, GLA_DV)
    la_f = heads(jax.nn.log_sigmoid((z_f @ w_gf + b_gf).astype(F32)) / GLA_TAU, GLA_DK)
    la_b = heads(jax.nn.log_sigmoid((z_b @ w_gb + b_gb).astype(F32)) / GLA_TAU, GLA_DK)
    o_f, s_f = _gla_chunk_scan(qh, kh, vh, la_f, s0_f.astype(F32))
    o_b, s_b = _gla_chunk_scan(jnp.flip(qh, 2), jnp.flip(kh, 2), jnp.flip(vh, 2), jnp.flip(la_b, 2),
                               s0_b.astype(F32))
    o = (o_f + jnp.flip(o_b, 2)).transpose(0, 2, 1, 3)
    o = o * lax.rsqrt(jnp.mean(o * o, axis=-1, keepdims=True) + RMS_EPS)
    o = o * norm_g.astype(F32).reshape(GLA_HEADS, GLA_DV)
    o = (o.reshape(B, T, GLA_V) * jax.nn.silu(r.astype(F32))).astype(q.dtype)
    return o @ w_out, s_f, s_b


def _mla_attend(q_nope, q_pe, k_nope, k_pe, v):
    B, T, H, _ = q_nope.shape
    nb = T // Q_BLOCK
    scale = (QK_NOPE + QK_ROPE) ** -0.5

    def to_blocks(t):
        return jnp.moveaxis(t.reshape((B, nb, Q_BLOCK) + t.shape[2:]), 1, 0)

    def block(args):
        qn, qp = args
        s = (jnp.einsum('bqhd,bkhd->bhqk', qn, k_nope, preferred_element_type=F32)
             + jnp.einsum('bqhr,bkr->bhqk', qp, k_pe, preferred_element_type=F32))
        p = jax.nn.softmax(s * scale, axis=-1).astype(v.dtype)
        return jnp.einsum('bhqk,bkhd->bqhd', p, v)

    o = lax.map(block, (to_blocks(q_nope), to_blocks(q_pe)))
    return jnp.moveaxis(o, 0, 1).reshape(B, T, H * V_HEAD)


def _moe(h, router_w, router_b, w_gate, b_gate, w_up, b_up, w_down, b_down):
    N, D = h.shape
    logits = (h @ router_w).astype(F32) + router_b.astype(F32)
    top_v, top_e = lax.top_k(logits, TOP_K)
    gates = jax.nn.softmax(top_v, axis=-1)
    NK = N * TOP_K
    flat_e = top_e.reshape(NK)
    flat_tok = jnp.repeat(jnp.arange(N, dtype=jnp.int32), TOP_K)
    flat_g = gates.reshape(NK)
    order = jnp.argsort(flat_e)
    se = flat_e[order]
    counts = jnp.bincount(flat_e, length=N_EXPERTS)
    padded = (counts + MOE_BLOCK - 1) // MOE_BLOCK * MOE_BLOCK
    start = jnp.cumsum(counts) - counts
    pend = jnp.cumsum(padded)
    pstart = pend - padded
    dest = pstart[se] + jnp.arange(NK) - start[se]
    n_blocks = (NK + N_EXPERTS * (MOE_BLOCK - 1) + MOE_BLOCK - 1) // MOE_BLOCK
    cap = n_blocks * MOE_BLOCK
    buf_tok = jnp.zeros((cap,), jnp.int32).at[dest].set(flat_tok[order])
    buf_g = jnp.zeros((cap,), F32).at[dest].set(flat_g[order])
    block_e = jnp.minimum(jnp.searchsorted(pend, jnp.arange(n_blocks) * MOE_BLOCK, side='right'),
                          N_EXPERTS - 1)
    xb = h[buf_tok].reshape(n_blocks, MOE_BLOCK, D)

    def expert_block(args):
        xblk, e = args
        gt = xblk @ w_gate[e] + b_gate[e]
        up = xblk @ w_up[e] + b_up[e]
        gt = jnp.minimum(gt, SWIGLU_LIMIT)
        up = jnp.clip(up, -SWIGLU_LIMIT, SWIGLU_LIMIT)
        act = gt * jax.nn.sigmoid(SWIGLU_ALPHA * gt) * (up + 1)
        return act @ w_down[e] + b_down[e]

    yb = lax.map(expert_block, (xb, block_e)).reshape(cap, D)
    y = jnp.zeros((N, D), F32).at[buf_tok].add(yb.astype(F32) * buf_g[:, None])
    return y.astype(h.dtype)


def _layer(x, mod, lw, ctx):
    B, T, D = x.shape
    sh_a, sc_a, g_a, sh_f, sc_f, g_f = jnp.split(mod, 6, axis=-1)
    h = _layernorm(x) * (1 + sc_a) + sh_a
    split_at = [int(s) for s in np.cumsum(IN_SIZES)[:-1]]
    (ca, cg, gq, gk, gv, gr, zf, zb, cq_raw, ckv_raw, kpe, m_a, m_b, m_c) = jnp.split(h @ lw['w_in'], split_at, axis=-1)

    y_a = _conv_branch(ca, cg, lw['conv_w_dw'], lw['conv_b_dw'], lw['conv_ln_g'], lw['conv_ln_b'], lw['conv_w_out'])

    if ctx is None:
        s0_f = jnp.zeros((B, GLA_HEADS, GLA_DK, GLA_DV), F32)
        s0_b = jnp.zeros((B, GLA_HEADS, GLA_DK, GLA_DV), F32)
    else:
        s0_f, s0_b = ctx[2], ctx[3]
    y_b, s_f, s_b = _gla_branch(gq, gk, gv, gr, zf, zb, lw['gla_w_gate_fwd'], lw['gla_b_gate_fwd'],
                                lw['gla_w_gate_bwd'], lw['gla_b_gate_bwd'], lw['gla_norm_g'], lw['gla_w_out'],
                                s0_f, s0_b)

    cq = _rmsnorm(cq_raw, lw['mla_q_norm_g'])
    q = (cq @ lw['mla_w_uq']).reshape(B, T, MLA_HEADS, QK_NOPE + QK_ROPE)
    q_nope, q_pe = q[..., :QK_NOPE], q[..., QK_NOPE:]
    ckv = _rmsnorm(ckv_raw, lw['mla_kv_norm_g'])
    if ctx is None:
        keys_c, keys_pe = ckv, kpe
    else:
        cos, sin = _axial_rope_cos_sin(T)
        q_pe = _apply_axial_rope(q_pe, cos[:, None, :], sin[:, None, :])
        keys_c = jnp.concatenate([ctx[0].astype(ckv.dtype), ckv], axis=1)
        keys_pe = jnp.concatenate([ctx[1].astype(kpe.dtype), _apply_axial_rope(kpe, cos, sin)], axis=1)
    S = keys_c.shape[1]
    k_nope = (keys_c @ lw['mla_w_uk']).reshape(B, S, MLA_HEADS, QK_NOPE)
    v = (keys_c @ lw['mla_w_uv']).reshape(B, S, MLA_HEADS, V_HEAD)
    y_c = _mla_attend(q_nope, q_pe, k_nope, keys_pe, v) @ lw['mla_w_out']

    merged = jax.nn.sigmoid(m_a) * y_a + jax.nn.sigmoid(m_b) * y_b + jax.nn.sigmoid(m_c) * y_c
    x = _layernorm(DEEPNORM_ALPHA * x + g_a * (merged @ lw['w_out']), lw['ln1_g'], lw['ln1_b'])

    h2 = _layernorm(x) * (1 + sc_f) + sh_f
    y2 = _moe(h2.reshape(B * T, D), lw['router_w'], lw['router_b'], lw['exp_w_gate'], lw['exp_b_gate'],
              lw['exp_w_up'], lw['exp_b_up'], lw['exp_w_down'], lw['exp_b_down']).reshape(B, T, D)
    x = _layernorm(DEEPNORM_ALPHA * x + g_f * y2, lw['ln2_g'], lw['ln2_b'])
    return x, (ckv, kpe, s_f, s_b)


def setup_inputs(seed: int = 0) -> dict:
    key = jax.random.key(seed)
    ks = iter(jax.random.split(key, 64))

    def nrm(shape, scale):
        return jax.random.normal(next(ks), shape, jnp.float32) * scale

    L, D, E, F = DEPTH, D_MODEL, N_EXPERTS, D_FF
    beta = DEEPNORM_BETA
    return {
        'x_prompt': nrm((BATCH, SEQ, D), 1.0),
        'x_sample': nrm((DEC_BATCH, DEC_SEQ, D), 1.0),
        'c': nrm((DEC_BATCH, D), 1.0),
        'cache_ckv': nrm((DEC_BATCH, L, PAST_LEN, KV_RANK), 1.0),
        'cache_kpe': nrm((DEC_BATCH, L, PAST_LEN, QK_ROPE), 1.0),
        'state_gla_fwd': nrm((DEC_BATCH, L, GLA_HEADS, GLA_DK, GLA_DV), 0.5),
        'state_gla_bwd': nrm((DEC_BATCH, L, GLA_HEADS, GLA_DK, GLA_DV), 0.5),
        'c_ctx': nrm((D,), 1.0),
        'w_mod': nrm((L, D, 6 * D), 0.5 * D ** -0.5),
        'b_mod': nrm((L, 6 * D), 0.02),
        'w_in': nrm((L, D, IN_DIM), D ** -0.5),
        'conv_w_dw': nrm((L, CONV_WIDTH, CONV_DIM), CONV_WIDTH ** -0.5),
        'conv_b_dw': nrm((L, CONV_DIM), 0.02),
        'conv_ln_g': 1.0 + nrm((L, CONV_DIM), 0.02),
        'conv_ln_b': nrm((L, CONV_DIM), 0.02),
        'conv_w_out': nrm((L, CONV_DIM, D), beta * CONV_DIM ** -0.5),
        'gla_w_gate_fwd': nrm((L, GLA_GATE_RANK, GLA_QK), GLA_GATE_RANK ** -0.5),
        'gla_b_gate_fwd': nrm((L, GLA_QK), 0.1),
        'gla_w_gate_bwd': nrm((L, GLA_GATE_RANK, GLA_QK), GLA_GATE_RANK ** -0.5),
        'gla_b_gate_bwd': nrm((L, GLA_QK), 0.1),
        'gla_norm_g': 1.0 + nrm((L, GLA_V), 0.02),
        'gla_w_out': nrm((L, GLA_V, D), beta * GLA_V ** -0.5),
        'mla_q_norm_g': 1.0 + nrm((L, Q_RANK), 0.02),
        'mla_w_uq': nrm((L, Q_RANK, MLA_HEADS * (QK_NOPE + QK_ROPE)), Q_RANK ** -0.5),
        'mla_kv_norm_g': 1.0 + nrm((L, KV_RANK), 0.02),
        'mla_w_uk': nrm((L, KV_RANK, MLA_HEADS * QK_NOPE), KV_RANK ** -0.5),
        'mla_w_uv': nrm((L, KV_RANK, MLA_HEADS * V_HEAD), KV_RANK ** -0.5),
        'mla_w_out': nrm((L, MLA_HEADS * V_HEAD, D), beta * (MLA_HEADS * V_HEAD) ** -0.5),
        'w_out': nrm((L, D, D), beta * D ** -0.5),
        'ln1_g': 1.0 + nrm((L, D), 0.02),
        'ln1_b': nrm((L, D), 0.02),
        'router_w': nrm((L, D, E), D ** -0.5),
        'router_b': nrm((L, E), 0.01),
        'exp_w_gate': nrm((L, E, D, F), D ** -0.5),
        'exp_b_gate': nrm((L, E, F), 0.01),
        'exp_w_up': nrm((L, E, D, F), D ** -0.5),
        'exp_b_up': nrm((L, E, F), 0.01),
        'exp_w_down': nrm((L, E, F, D), beta * F ** -0.5),
        'exp_b_down': nrm((L, E, D), 0.01),
        'ln2_g': 1.0 + nrm((L, D), 0.02),
        'ln2_b': nrm((L, D), 0.02),
    }


def reference(x_prompt, x_sample, c, cache_ckv, cache_kpe, state_gla_fwd, state_gla_bwd, c_ctx,
              w_mod, b_mod, w_in,
              conv_w_dw, conv_b_dw, conv_ln_g, conv_ln_b, conv_w_out,
              gla_w_gate_fwd, gla_b_gate_fwd, gla_w_gate_bwd, gla_b_gate_bwd, gla_norm_g, gla_w_out,
              mla_q_norm_g, mla_w_uq, mla_kv_norm_g, mla_w_uk, mla_w_uv, mla_w_out,
              w_out, ln1_g, ln1_b,
              router_w, router_b, exp_w_gate, exp_b_gate, exp_w_up, exp_b_up, exp_w_down, exp_b_down,
              ln2_g, ln2_b):
    x_p = x_prompt
    x_s = x_sample
    ckv_l, kpe_l, sf_l, sb_l = [], [], [], []
    for l in range(DEPTH):
        lw = dict(w_in=w_in[l],
                  conv_w_dw=conv_w_dw[l], conv_b_dw=conv_b_dw[l], conv_ln_g=conv_ln_g[l],
                  conv_ln_b=conv_ln_b[l], conv_w_out=conv_w_out[l],
                  gla_w_gate_fwd=gla_w_gate_fwd[l], gla_b_gate_fwd=gla_b_gate_fwd[l],
                  gla_w_gate_bwd=gla_w_gate_bwd[l], gla_b_gate_bwd=gla_b_gate_bwd[l],
                  gla_norm_g=gla_norm_g[l], gla_w_out=gla_w_out[l],
                  mla_q_norm_g=mla_q_norm_g[l], mla_w_uq=mla_w_uq[l], mla_kv_norm_g=mla_kv_norm_g[l],
                  mla_w_uk=mla_w_uk[l], mla_w_uv=mla_w_uv[l], mla_w_out=mla_w_out[l],
                  w_out=w_out[l], ln1_g=ln1_g[l], ln1_b=ln1_b[l],
                  router_w=router_w[l], router_b=router_b[l],
                  exp_w_gate=exp_w_gate[l], exp_b_gate=exp_b_gate[l], exp_w_up=exp_w_up[l],
                  exp_b_up=exp_b_up[l], exp_w_down=exp_w_down[l], exp_b_down=exp_b_down[l],
                  ln2_g=ln2_g[l], ln2_b=ln2_b[l])
        mod_ctx = (jax.nn.silu(c_ctx) @ w_mod[l] + b_mod[l])[None, None, :]
        x_p, (ckv, kpe, s_f, s_b) = _layer(x_p, mod_ctx, lw, None)
        ckv_l.append(ckv)
        kpe_l.append(kpe)
        sf_l.append(s_f)
        sb_l.append(s_b)
        mod_lat = (jax.nn.silu(c) @ w_mod[l] + b_mod[l])[:, None, :]
        ctx = (cache_ckv[:, l], cache_kpe[:, l], state_gla_fwd[:, l], state_gla_bwd[:, l])
        x_s, _ = _layer(x_s, mod_lat, lw, ctx)
    new_ckv = jnp.stack(ckv_l, axis=1)
    new_kpe = jnp.stack(kpe_l, axis=1)
    new_gla_fwd = jnp.stack(sf_l, axis=1)
    new_gla_bwd = jnp.stack(sb_l, axis=1)
    return (x_p, x_s, new_ckv, new_kpe, new_gla_fwd, new_gla_bwd)
```

```python
import functools

import jax
import jax.numpy as jnp
from jax import lax
from jax.experimental import pallas as pl
from jax.experimental.pallas import tpu as pltpu

F32 = jnp.float32
BF16 = jnp.bfloat16

LANES = 128
VMEM_LIMIT_BYTES = 56 * 1024 * 1024

GRID_W = 64
GLA_TAU = 16.0
GLA_CHUNK = 64
MLA_HEADS = 8
ROPE_BASE = 10000.0
TOP_K = 4
SWIGLU_LIMIT = 7.0
SWIGLU_ALPHA = 1.702
LN_EPS = 1e-5
RMS_EPS = 1e-6

CONV_PAD = 16
MOE_ROWS = 1024
MOE_SUB = 256


def _params(*sem):
    return pltpu.CompilerParams(dimension_semantics=sem, vmem_limit_bytes=VMEM_LIMIT_BYTES)


def _tile(n, pref):
    if n <= pref:
        return n
    t = (pref // LANES) * LANES
    while t > LANES and n % t:
        t -= LANES
    assert n % t == 0, (n, pref)
    return t


def _ln(x):
    mu = jnp.mean(x, axis=-1, keepdims=True)
    xc = x - mu
    return xc * lax.rsqrt(jnp.mean(xc * xc, axis=-1, keepdims=True) + LN_EPS)


def _rms(x):
    return x * lax.rsqrt(jnp.mean(x * x, axis=-1, keepdims=True) + RMS_EPS)


def _silu(x):
    return x * jax.nn.sigmoid(x)


def _dot(a, b):
    return jnp.dot(a, b, preferred_element_type=F32)


def _dot_nt(a, b):
    return lax.dot_general(a, b, (((1,), (1,)), ((), ())), preferred_element_type=F32)


def _dot_tn(a, b):
    return lax.dot_general(a, b, (((0,), (0,)), ((), ())), preferred_element_type=F32)


def _mod_kernel(c_ref, w_ref, b_ref, o_ref):
    s = _silu(c_ref[...])
    o_ref[0] = _dot(s.astype(BF16), w_ref[0].astype(BF16)) + b_ref[0]


def _modulation(cc, w_mod, b_mod):
    n_layers, d, d6 = w_mod.shape
    mp = cc.shape[0]
    tn = _tile(d6, 1024)
    return pl.pallas_call(
        _mod_kernel,
        grid=(n_layers, d6 // tn),
        in_specs=[
            pl.BlockSpec((mp, d), lambda l, j: (0, 0)),
            pl.BlockSpec((1, d, tn), lambda l, j: (l, 0, j)),
            pl.BlockSpec((1, 1, tn), lambda l, j: (l, 0, j)),
        ],
        out_specs=pl.BlockSpec((1, mp, tn), lambda l, j: (l, 0, j)),
        out_shape=jax.ShapeDtypeStruct((n_layers, mp, d6), F32),
        compiler_params=_params("parallel", "parallel"),
        name="modulation",
    )(cc, w_mod, b_mod.reshape(n_layers, 1, d6))


def _mod_row(i, tm, n_prompt, t_sample):
    r = i * tm
    return jnp.where(r < n_prompt, 0, 1 + (r - n_prompt) // t_sample)


def _gemm_in_kernel(x_ref, mod_ref, w_ref, o_ref, h_ref):
    @pl.when(pl.program_id(1) == 0)
    def _():
        y = _ln(x_ref[...])
        h_ref[...] = (y * (1.0 + mod_ref[0, 1:2, :]) + mod_ref[0, 0:1, :]).astype(BF16)

    o_ref[...] = _dot(h_ref[...], w_ref[...])


def _gemm_in(x, mod, w, n_prompt, t_sample, tm=1024, tn=896):
    m, d = x.shape
    n = w.shape[1]
    tm = min(tm, t_sample, n_prompt)
    tn = _tile(n, tn)
    assert m % tm == 0 and n_prompt % tm == 0 and t_sample % tm == 0
    return pl.pallas_call(
        _gemm_in_kernel,
        grid=(m // tm, n // tn),
        in_specs=[
            pl.BlockSpec((tm, d), lambda i, j: (i, 0)),
            pl.BlockSpec((1, 6, d), lambda i, j: (_mod_row(i, tm, n_prompt, t_sample), 0, 0)),
            pl.BlockSpec((d, tn), lambda i, j: (0, j)),
        ],
        out_specs=pl.BlockSpec((tm, tn), lambda i, j: (i, j)),
        out_shape=jax.ShapeDtypeStruct((m, n), F32),
        scratch_shapes=[pltpu.VMEM((tm, d), BF16)],
        compiler_params=_params("parallel", "arbitrary"),
        name="gemm_in",
    )(x, mod, w)


def _gemm_kernel(*refs, n_x, nb_first, has_gate, has_prev):
    it = iter(refs)
    x_refs = [next(it) for _ in range(n_x)]
    w_ref = next(it)
    g_ref = next(it) if has_gate else None
    p_ref = next(it) if has_prev else None
    o_ref = next(it)

    def compute(x_ref):
        acc = _dot(x_ref[...].astype(BF16), w_ref[...].astype(BF16))
        if has_gate:
            acc = jax.nn.sigmoid(g_ref[...]) * acc
        if has_prev:
            acc = acc + p_ref[...]
        o_ref[...] = acc.astype(o_ref.dtype)

    if n_x == 1:
        compute(x_refs[0])
    else:
        i = pl.program_id(0)
        pl.when(i < nb_first)(lambda: compute(x_refs[0]))
        pl.when(i >= nb_first)(lambda: compute(x_refs[1]))


def _gemm(x, w, *, gate=None, gate_col=0, prev=None, out_dtype=F32, rows=None, tm=512, tn=1024, name="gemm"):
    xs = x if isinstance(x, tuple) else (x,)
    k, n = w.shape
    m = sum(xi.shape[0] for xi in xs) if rows is None else rows
    tm = min([tm, m] + [xi.shape[0] for xi in xs])
    tn = _tile(n, tn)
    assert m % tm == 0 and all(xi.shape[1] == k and xi.shape[0] % tm == 0 for xi in xs)
    nb_first = xs[0].shape[0] // tm
    if len(xs) == 1:
        in_specs = [pl.BlockSpec((tm, k), lambda i, j: (i, 0))]
    else:
        in_specs = [pl.BlockSpec((tm, k), lambda i, j: (jnp.minimum(i, nb_first - 1), 0)),
                    pl.BlockSpec((tm, k), lambda i, j: (jnp.maximum(i - nb_first, 0), 0))]
    in_specs.append(pl.BlockSpec((k, tn), lambda i, j: (0, j)))
    args = list(xs) + [w]
    if gate is not None:
        assert gate_col % tn == 0
        goff = gate_col // tn
        in_specs.append(pl.BlockSpec((tm, tn), lambda i, j: (i, goff + j)))
        args.append(gate)
    if prev is not None:
        in_specs.append(pl.BlockSpec((tm, tn), lambda i, j: (i, j)))
        args.append(prev)
    return pl.pallas_call(
        functools.partial(_gemm_kernel, n_x=len(xs), nb_first=nb_first, has_gate=gate is not None,
                          has_prev=prev is not None),
        grid=(m // tm, n // tn),
        in_specs=in_specs,
        out_specs=pl.BlockSpec((tm, tn), lambda i, j: (i, j)),
        out_shape=jax.ShapeDtypeStruct((m, n), out_dtype),
        compiler_params=_params("parallel", "parallel"),
        name=name,
    )(*args)


def _conv_kernel(a_ref, g_ref, wdw_ref, bdw_ref, lng_ref, lnb_ref, o_ref, upad_ref, yc_ref, *, seq, rb, rc, width):
    c = a_ref.shape[1]
    r = pl.program_id(1)
    fill_rows = min(seq, 64)

    @pl.when(r == 0)
    def _():
        zeros = jnp.zeros((CONV_PAD, c), F32)
        upad_ref[0:CONV_PAD, :] = zeros
        upad_ref[CONV_PAD + seq:CONV_PAD + seq + CONV_PAD, :] = zeros

        def fill(i, carry):
            src = pl.ds(pl.multiple_of(i * fill_rows, 8), fill_rows)
            dst = pl.ds(pl.multiple_of(CONV_PAD + i * fill_rows, 8), fill_rows)
            upad_ref[dst, :] = a_ref[src, :] * jax.nn.sigmoid(g_ref[src, :])
            return carry

        lax.fori_loop(0, seq // fill_rows, fill, 0)

    lead = CONV_PAD - width // 2
    n_shift = 8
    n_al = (lead + width - 1) // n_shift + 1
    win = rc + n_shift * n_al

    def row_chunk(ci, carry):
        base = pl.multiple_of(r * rb + ci * rc, 8)
        for cj in range(c // LANES):
            lanes = slice(cj * LANES, (cj + 1) * LANES)
            w_all = upad_ref[pl.ds(base, win), lanes]
            acc = jnp.zeros((rc, LANES), F32)
            for b in range(n_shift):
                taps = [(a8, n_shift * a8 + b - lead) for a8 in range(n_al)]
                taps = [(a8, k) for a8, k in taps if 0 <= k < width]
                if not taps:
                    continue
                w_b = w_all[b:b + rc + n_shift * (n_al - 1)]
                part = None
                for a8, k in taps:
                    term = wdw_ref[k:k + 1, lanes] * w_b[n_shift * a8:n_shift * a8 + rc]
                    part = term if part is None else part + term
                acc = acc + part
            yc_ref[pl.ds(pl.multiple_of(ci * rc, 8), rc), lanes] = acc + bdw_ref[:, lanes]
        return carry

    lax.fori_loop(0, rb // rc, row_chunk, 0)
    y = _ln(yc_ref[...]) * lng_ref[...] + lnb_ref[...]
    o_ref[...] = _silu(y).astype(o_ref.dtype)


def _conv_branch(hmat, col_a, col_g, w_dw, b_dw, ln_g, ln_b, *, row_off, n_seq, seq):
    width, c = w_dw.shape
    rb = min(seq, 256)
    rc = min(rb, 64)
    assert width // 2 <= CONV_PAD and row_off % seq == 0 and col_a % c == 0 and col_g % c == 0
    s_off = row_off // seq
    in_specs = [
        pl.BlockSpec((seq, c), lambda s, r: (s_off + s, col_a // c)),
        pl.BlockSpec((seq, c), lambda s, r: (s_off + s, col_g // c)),
        pl.BlockSpec((width, c), lambda s, r: (0, 0)),
        pl.BlockSpec((1, c), lambda s, r: (0, 0)),
        pl.BlockSpec((1, c), lambda s, r: (0, 0)),
        pl.BlockSpec((1, c), lambda s, r: (0, 0)),
    ]
    args = [hmat, hmat, w_dw, b_dw.reshape(1, c), ln_g.reshape(1, c), ln_b.reshape(1, c)]
    return pl.pallas_call(
        functools.partial(_conv_kernel, seq=seq, rb=rb, rc=rc, width=width),
        grid=(n_seq, seq // rb),
        in_specs=in_specs,
        out_specs=pl.BlockSpec((rb, c), lambda s, r: (s * (seq // rb) + r, 0)),
        out_shape=jax.ShapeDtypeStruct((n_seq * seq, c), BF16),
        scratch_shapes=[pltpu.VMEM((seq + 2 * CONV_PAD, c), F32), pltpu.VMEM((rb, c), F32)],
        compiler_params=_params("parallel", "arbitrary"),
        name="conv_branch",
    )(*args)


def _cumsum3(tri, la):
    hi = la.astype(BF16)
    r1 = la - hi.astype(F32)
    mid = r1.astype(BF16)
    lo = (r1 - mid.astype(F32)).astype(BF16)
    return _dot(tri, hi) + _dot(tri, mid) + _dot(tri, lo)


def _log_sigmoid(x):
    return jnp.minimum(x, 0.0) - jnp.log(1.0 + jnp.exp(-jnp.abs(x)))


def _gla_kernel(*refs, seq, chunk, has_state):
    it = iter(refs)
    q_ref, k_ref, v_ref, r_ref, z_ref = (next(it) for _ in range(5))
    wgf_ref, bgf_ref, wgb_ref, bgb_ref, ng_ref = (next(it) for _ in range(5))
    if has_state:
        s0f_ref, s0b_ref = next(it), next(it)
        o_ref = next(it)
        sf_ref = sb_ref = None
    else:
        o_ref, sf_ref, sb_ref = next(it), next(it), next(it)
    laf_ref, lab_ref, oacc_ref, st_ref = (next(it) for _ in range(4))

    dk = q_ref.shape[1]
    dv = v_ref.shape[1]
    n_chunks = seq // chunk
    z = z_ref[...].astype(BF16)
    laf_ref[...] = _log_sigmoid(_dot(z, wgf_ref[...].astype(BF16)) + bgf_ref[...]) / GLA_TAU
    lab_ref[...] = _log_sigmoid(_dot(z, wgb_ref[...].astype(BF16)) + bgb_ref[...]) / GLA_TAU

    row = lax.broadcasted_iota(jnp.int32, (chunk, chunk), 0)
    col = lax.broadcasted_iota(jnp.int32, (chunk, chunk), 1)
    lower = row >= col
    upper = row <= col
    tri_f = jnp.where(lower, 1.0, 0.0).astype(BF16)
    tri_b = jnp.where(upper, 1.0, 0.0).astype(BF16)
    q_scale = dk ** -0.5

    def chunk_step(rows, la_ref, tri, mask, edge):
        b = _cumsum3(tri, la_ref[rows, :])
        b_edge = b[edge:edge + 1, :]
        q = q_ref[rows, :] * q_scale
        k = k_ref[rows, :]
        v = v_ref[rows, :].astype(BF16)
        qd = (q * jnp.exp(b)).astype(BF16)
        kd = (k * jnp.exp(-b)).astype(BF16)
        ke = (k * jnp.exp(b_edge - b)).astype(BF16)
        att = jnp.where(mask, _dot_nt(qd, kd), 0.0).astype(BF16)
        st = st_ref[...]
        o = _dot(att, v) + _dot_nt(qd, st.astype(BF16))
        st_ref[...] = st * jnp.exp(b_edge) + _dot_tn(v, ke)
        return o

    if has_state:
        st_ref[...] = s0f_ref[0, 0].T
    else:
        st_ref[...] = jnp.zeros((dv, dk), F32)

    def fwd(n, carry):
        rows = pl.ds(pl.multiple_of(n * chunk, chunk), chunk)
        oacc_ref[rows, :] = chunk_step(rows, laf_ref, tri_f, lower, chunk - 1)
        return carry

    lax.fori_loop(0, n_chunks, fwd, 0)
    if not has_state:
        sf_ref[0, 0] = st_ref[...].T

    if has_state:
        st_ref[...] = s0b_ref[0, 0].T
    else:
        st_ref[...] = jnp.zeros((dv, dk), F32)

    def bwd(j, carry):
        n = n_chunks - 1 - j
        rows = pl.ds(pl.multiple_of(n * chunk, chunk), chunk)
        o = oacc_ref[rows, :] + chunk_step(rows, lab_ref, tri_b, upper, 0)
        o = _rms(o) * ng_ref[...]
        o_ref[rows, :] = (o * _silu(r_ref[rows, :])).astype(o_ref.dtype)
        return carry

    lax.fori_loop(0, n_chunks, bwd, 0)
    if not has_state:
        sb_ref[0, 0] = st_ref[...].T


def _gla_branch(hmat, cols, wgf, bgf, wgb, bgb, norm_g, *, heads, dk, dv, row_off, n_seq, seq, state=None):
    col_q, col_k, col_v, col_r, col_z = cols
    s_off = row_off // seq
    has_state = state is not None
    in_specs = [
        pl.BlockSpec((seq, dk), lambda s, h: (s_off + s, col_q // dk + h)),
        pl.BlockSpec((seq, dk), lambda s, h: (s_off + s, col_k // dk + h)),
        pl.BlockSpec((seq, dv), lambda s, h: (s_off + s, col_v // dv + h)),
        pl.BlockSpec((seq, dv), lambda s, h: (s_off + s, col_r // dv + h)),
        pl.BlockSpec((seq, LANES), lambda s, h: (s_off + s, col_z // LANES)),
        pl.BlockSpec((LANES, dk), lambda s, h: (0, h)),
        pl.BlockSpec((1, dk), lambda s, h: (0, h)),
        pl.BlockSpec((LANES, dk), lambda s, h: (0, h)),
        pl.BlockSpec((1, dk), lambda s, h: (0, h)),
        pl.BlockSpec((1, dv), lambda s, h: (0, h)),
    ]
    args = [hmat] * 5 + [wgf, bgf.reshape(1, -1), wgb, bgb.reshape(1, -1), norm_g.reshape(1, -1)]
    o_spec = pl.BlockSpec((seq, dv), lambda s, h: (s, h))
    o_shape = jax.ShapeDtypeStruct((n_seq * seq, heads * dv), BF16)
    if has_state:
        st_spec = pl.BlockSpec((1, 1, dk, dv), lambda s, h: (s, h, 0, 0))
        in_specs += [st_spec, st_spec]
        args += [state[0], state[1]]
        out_specs, out_shape = o_spec, o_shape
    else:
        st_spec = pl.BlockSpec((1, 1, dk, dv), lambda s, h: (s, h, 0, 0))
        st_shape = jax.ShapeDtypeStruct((n_seq, heads, dk, dv), F32)
        out_specs, out_shape = [o_spec, st_spec, st_spec], [o_shape, st_shape, st_shape]
    return pl.pallas_call(
        functools.partial(_gla_kernel, seq=seq, chunk=min(GLA_CHUNK, seq), has_state=has_state),
        grid=(n_seq, heads),
        in_specs=in_specs,
        out_specs=out_specs,
        out_shape=out_shape,
        scratch_shapes=[pltpu.VMEM((seq, dk), F32), pltpu.VMEM((seq, dk), F32), pltpu.VMEM((seq, dv), F32),
                        pltpu.VMEM((dv, dk), F32)],
        compiler_params=_params("parallel", "parallel"),
        name="gla_branch",
    )(*args)


def _mla_proj_kernel(cq_ref, ckv_ref, pe_ref, qg_ref, kvg_ref, wq_ref, wqr_ref, tq_ref, tk_ref,
                     q_ref, ckvn_ref, kpe_ref, *, hw):
    cqn = (_rms(cq_ref[...]) * qg_ref[...]).astype(BF16)
    a = _dot(cqn, wq_ref[...])
    ar = _dot(cqn, wqr_ref[...])
    cos = tq_ref[:, 0:hw]
    sin = tq_ref[:, hw:2 * hw]
    for h in range(a.shape[1] // hw):
        sl = slice(h * hw, (h + 1) * hw)
        q_ref[:, sl] = (a[:, sl] * cos + ar[:, sl] * sin).astype(q_ref.dtype)
    ckvn_ref[...] = _rms(ckv_ref[...]) * kvg_ref[...]
    v = pe_ref[...] * tk_ref[...]
    v = v + pltpu.roll(v, LANES // 2, axis=1)
    lane = lax.broadcasted_iota(jnp.int32, v.shape, 1)
    kpe_ref[...] = jnp.where(lane < LANES // 2, v, 0.0)


def _mla_proj(hmat, col_cq, col_ckv, col_pe, q_g, kv_g, wq, wqr, tq, tk, tm=512):
    m = hmat.shape[0]
    rq = q_g.shape[0]
    rkv = kv_g.shape[0]
    nq = wq.shape[1]
    hw = tq.shape[1] // 2
    tm = min(tm, m)
    return pl.pallas_call(
        functools.partial(_mla_proj_kernel, hw=hw),
        grid=(m // tm,),
        in_specs=[
            pl.BlockSpec((tm, rq), lambda i: (i, col_cq // rq)),
            pl.BlockSpec((tm, rkv), lambda i: (i, col_ckv // rkv)),
            pl.BlockSpec((tm, LANES), lambda i: (i, col_pe // LANES)),
            pl.BlockSpec((1, rq), lambda i: (0, 0)),
            pl.BlockSpec((1, rkv), lambda i: (0, 0)),
            pl.BlockSpec((rq, nq), lambda i: (0, 0)),
            pl.BlockSpec((rq, nq), lambda i: (0, 0)),
            pl.BlockSpec((tm, 2 * hw), lambda i: (i, 0)),
            pl.BlockSpec((tm, LANES), lambda i: (i, 0)),
        ],
        out_specs=[
            pl.BlockSpec((tm, nq), lambda i: (i, 0)),
            pl.BlockSpec((tm, rkv), lambda i: (i, 0)),
            pl.BlockSpec((tm, LANES), lambda i: (i, 0)),
        ],
        out_shape=[
            jax.ShapeDtypeStruct((m, nq), BF16),
            jax.ShapeDtypeStruct((m, rkv), F32),
            jax.ShapeDtypeStruct((m, LANES), F32),
        ],
        compiler_params=_params("parallel"),
        name="mla_proj",
    )(hmat, hmat, hmat, q_g.reshape(1, rq), kv_g.reshape(1, rkv), wq, wqr, tq, tk)


def _attn_kernel(*refs, heads, hw, dh, scale):
    q_ref, kv_ref, kpe_ref = refs[:3]
    o_ref, kcat_ref = refs[-2:]

    @pl.when(pl.program_id(1) == 0)
    def _():
        kp = kpe_ref[0].astype(BF16)
        for h in range(heads):
            kcat_ref[:, h * hw:h * hw + dh] = kv_ref[0, :, h * dh:(h + 1) * dh]
            kcat_ref[:, h * hw + dh:(h + 1) * hw] = kp

    v_off = heads * dh
    for h in range(heads):
        s = _dot_nt(q_ref[:, h * hw:(h + 1) * hw], kcat_ref[:, h * hw:(h + 1) * hw]) * scale
        p = jnp.exp(s - jnp.max(s, axis=-1, keepdims=True))
        l = jnp.sum(p, axis=-1, keepdims=True)
        o = _dot(p.astype(BF16), kv_ref[0, :, v_off + h * dh:v_off + (h + 1) * dh]) / l
        o_ref[:, h * dh:(h + 1) * dh] = o.astype(o_ref.dtype)


def _attention(q_all, kv, kpe, *, heads, dh, scale, row_off, n_seq, seq, tq=256):
    s_len = kv.shape[1]
    hw = q_all.shape[1] // heads
    tq = min(tq, seq)
    q_off = row_off // tq
    in_specs = [
        pl.BlockSpec((tq, heads * hw), lambda b, i: (q_off + b * (seq // tq) + i, 0)),
        pl.BlockSpec((1, s_len, 2 * heads * dh), lambda b, i: (b, 0, 0)),
        pl.BlockSpec((1, s_len, LANES), lambda b, i: (b, 0, 0)),
    ]
    args = [q_all, kv, kpe]
    return pl.pallas_call(
        functools.partial(_attn_kernel, heads=heads, hw=hw, dh=dh, scale=scale),
        grid=(n_seq, seq // tq),
        in_specs=in_specs,
        out_specs=pl.BlockSpec((tq, heads * dh), lambda b, i: (b * (seq // tq) + i, 0)),
        out_shape=jax.ShapeDtypeStruct((n_seq * seq, heads * dh), BF16),
        scratch_shapes=[pltpu.VMEM((s_len, heads * hw), BF16)],
        compiler_params=_params("parallel", "arbitrary"),
        name="mla_attention",
    )(*args)


def _split3(x):
    hi = x.astype(BF16)
    lo = (x - hi.astype(F32)).astype(BF16)
    return hi, lo


def _post_attn_kernel(m_ref, x_ref, w_ref, mod_ref, g1_ref, b1_ref, rw_ref, rb_ref,
                      x1_ref, h2_ref, ri_ref, rg_ref, cnt_ref, run_ref, tri_ref, *, alpha, n_exp):
    tm = x_ref.shape[0]
    step = pl.program_id(0)

    @pl.when(step == 0)
    def _():
        run_ref[...] = jnp.zeros_like(run_ref)
        row = lax.broadcasted_iota(jnp.int32, (tm, tm), 0)
        col = lax.broadcasted_iota(jnp.int32, (tm, tm), 1)
        tri_ref[...] = jnp.where(row > col, 1.0, 0.0).astype(BF16)

    y = _dot(m_ref[...], w_ref[...])
    x1 = _ln(alpha * x_ref[...] + mod_ref[0, 2:3, :] * y) * g1_ref[...] + b1_ref[...]
    x1_ref[...] = x1
    h2 = _ln(x1) * (1.0 + mod_ref[0, 4:5, :]) + mod_ref[0, 3:4, :]
    h2_ref[...] = h2

    h_hi, h_lo = _split3(h2)
    w_hi, w_lo = _split3(rw_ref[...])
    logits = _dot(h_hi, w_hi) + _dot(h_lo, w_hi) + _dot(h_hi, w_lo) + rb_ref[...]
    lane = lax.broadcasted_iota(jnp.int32, logits.shape, 1).astype(F32)
    neg = jnp.float32(-jnp.inf)
    logits = jnp.where(lane < n_exp, logits, neg)

    counts = jnp.zeros(logits.shape, F32)
    vals, idxs = [], []
    for _ in range(TOP_K):
        mx = jnp.max(logits, axis=-1, keepdims=True)
        idx = jnp.min(jnp.where(logits == mx, lane, float(LANES)), axis=-1, keepdims=True)
        hit = lane == idx
        counts = counts + jnp.where(hit, 1.0, 0.0)
        logits = jnp.where(hit, neg, logits)
        vals.append(mx)
        idxs.append(idx)

    es = [jnp.exp(v - vals[0]) for v in vals]
    denom = es[0]
    for e in es[1:]:
        denom = denom + e
    before = _dot(tri_ref[...], counts.astype(BF16)) + run_ref[...]
    out_i = jnp.zeros(logits.shape, jnp.int32)
    out_g = jnp.zeros(logits.shape, F32)
    for k in range(TOP_K):
        rank = jnp.sum(jnp.where(lane == idxs[k], before, 0.0), axis=-1, keepdims=True)
        out_i = jnp.where(lane == k, idxs[k].astype(jnp.int32), out_i)
        out_i = jnp.where(lane == TOP_K + k, rank.astype(jnp.int32), out_i)
        out_g = jnp.where(lane == k, es[k] / denom, out_g)
    ri_ref[...] = out_i[:, 0:2 * TOP_K]
    rg_ref[...] = out_g[:, 0:TOP_K]
    run_ref[...] = run_ref[...] + jnp.sum(counts, axis=0, keepdims=True)
    cnt_ref[...] = run_ref[...]


def _post_attn(merged, x, w_out, mod, ln_g, ln_b, router_w, router_b, n_prompt, t_sample, alpha, n_exp, tm=256):
    m, d = x.shape
    tm = min(tm, t_sample, n_prompt)
    row_spec = pl.BlockSpec((tm, d), lambda i: (i, 0))
    vec_spec = pl.BlockSpec((1, d), lambda i: (0, 0))
    return pl.pallas_call(
        functools.partial(_post_attn_kernel, alpha=alpha, n_exp=n_exp),
        grid=(m // tm,),
        in_specs=[
            row_spec, row_spec,
            pl.BlockSpec((d, d), lambda i: (0, 0)),
            pl.BlockSpec((1, 6, d), lambda i: (_mod_row(i, tm, n_prompt, t_sample), 0, 0)),
            vec_spec, vec_spec,
            pl.BlockSpec((d, LANES), lambda i: (0, 0)),
            pl.BlockSpec((1, LANES), lambda i: (0, 0)),
        ],
        out_specs=[
            row_spec, row_spec,
            pl.BlockSpec((tm, 2 * TOP_K), lambda i: (i, 0)),
            pl.BlockSpec((tm, TOP_K), lambda i: (i, 0)),
            pl.BlockSpec((1, LANES), lambda i: (0, 0)),
        ],
        out_shape=[
            jax.ShapeDtypeStruct((m, d), F32),
            jax.ShapeDtypeStruct((m, d), F32),
            jax.ShapeDtypeStruct((m, 2 * TOP_K), jnp.int32),
            jax.ShapeDtypeStruct((m, TOP_K), F32),
            jax.ShapeDtypeStruct((1, LANES), F32),
        ],
        scratch_shapes=[pltpu.VMEM((1, LANES), F32), pltpu.VMEM((tm, tm), BF16)],
        compiler_params=_params("arbitrary"),
        name="post_attn_router",
    )(merged, x, w_out, mod, ln_g.reshape(1, d), ln_b.reshape(1, d), router_w, router_b)


def _row_copy(src_hbm, dst_ref, sem, src_row, dst_row):
    return pltpu.make_async_copy(src_hbm.at[pl.ds(src_row, 1), :], dst_ref.at[pl.ds(dst_row, 1), :], sem)


def _moe_gather_kernel(used_ref, idx_ref, h_hbm, o_ref, buf_ref, sem):
    rows = buf_ref.shape[0]
    blk = pl.program_id(0)

    @pl.when(blk < used_ref[0])
    def _():
        def issue(r, carry):
            _row_copy(h_hbm, buf_ref, sem, idx_ref[0, 0, r], r).start()
            return carry

        lax.fori_loop(0, rows, issue, 0)

        def drain(r, carry):
            _row_copy(h_hbm, buf_ref, sem, 0, r).wait()
            return carry

        lax.fori_loop(0, rows, drain, 0)
        o_ref[...] = buf_ref[...].astype(o_ref.dtype)

    @pl.when(blk >= used_ref[0])
    def _():
        o_ref[...] = jnp.zeros_like(o_ref)


def _moe_gather(h2, buf_tok, n_used_sub, rows=MOE_SUB):
    cap = buf_tok.shape[0]
    d = h2.shape[1]
    return pl.pallas_call(
        _moe_gather_kernel,
        grid_spec=pltpu.PrefetchScalarGridSpec(
            num_scalar_prefetch=1,
            grid=(cap // rows,),
            in_specs=[
                pl.BlockSpec((1, 1, rows), lambda i, u: (i, 0, 0), memory_space=pltpu.SMEM),
                pl.BlockSpec(memory_space=pl.ANY),
            ],
            out_specs=pl.BlockSpec((rows, d), lambda i, u: (i, 0)),
            scratch_shapes=[pltpu.VMEM((rows, d), F32), pltpu.SemaphoreType.DMA],
        ),
        out_shape=jax.ShapeDtypeStruct((cap, d), BF16),
        compiler_params=_params("arbitrary"),
        name="moe_gather",
    )(n_used_sub, buf_tok.reshape(cap // rows, 1, rows), h2)


def _moe_expert_kernel(be_ref, bv_ref, x_ref, wg_ref, bg_ref, wu_ref, bu_ref, wd_ref, bd_ref,
                       o_ref, wgc_ref, wuc_ref, wdc_ref):
    i = pl.program_id(0)
    f = pl.program_id(1)
    valid = bv_ref[i]
    del be_ref

    @pl.when(valid > 0)
    def _():
        wgc_ref[...] = wg_ref[0].astype(BF16)
        wuc_ref[...] = wu_ref[0].astype(BF16)
        wdc_ref[...] = wd_ref[0].astype(BF16)

    for sb in range(x_ref.shape[0] // MOE_SUB):
        rows = slice(sb * MOE_SUB, (sb + 1) * MOE_SUB)

        @pl.when(sb * MOE_SUB < valid)
        def _():
            x = x_ref[rows, :]
            gt = jnp.minimum(_dot(x, wgc_ref[...]) + bg_ref[0], SWIGLU_LIMIT)
            up = jnp.clip(_dot(x, wuc_ref[...]) + bu_ref[0], -SWIGLU_LIMIT, SWIGLU_LIMIT)
            act = gt * jax.nn.sigmoid(SWIGLU_ALPHA * gt) * (up + 1.0)
            y = _dot(act.astype(BF16), wdc_ref[...])

            @pl.when(f == 0)
            def _():
                o_ref[rows, :] = y + bd_ref[0]

            @pl.when(f > 0)
            def _():
                o_ref[rows, :] = o_ref[rows, :] + y

        @pl.when(jnp.logical_and(sb * MOE_SUB >= valid, f == 0))
        def _():
            o_ref[rows, :] = jnp.zeros((MOE_SUB, o_ref.shape[1]), F32)


def _moe_experts(xs, blk_e, blk_valid, w_gate, b_gate, w_up, b_up, w_down, b_down, tf=256):
    cap, d = xs.shape
    n_exp, _, ff = w_gate.shape
    tf = _tile(ff, tf)
    nf = ff // tf
    nblk = cap // MOE_ROWS

    def f_idx(i, f, bv):
        return jnp.where(bv[i] > 0, f, nf - 1)

    return pl.pallas_call(
        _moe_expert_kernel,
        grid_spec=pltpu.PrefetchScalarGridSpec(
            num_scalar_prefetch=2,
            grid=(nblk, nf),
            in_specs=[
                pl.BlockSpec((MOE_ROWS, d), lambda i, f, be, bv: (i, 0)),
                pl.BlockSpec((1, d, tf), lambda i, f, be, bv: (be[i], 0, f_idx(i, f, bv))),
                pl.BlockSpec((1, 1, tf), lambda i, f, be, bv: (be[i], 0, f_idx(i, f, bv))),
                pl.BlockSpec((1, d, tf), lambda i, f, be, bv: (be[i], 0, f_idx(i, f, bv))),
                pl.BlockSpec((1, 1, tf), lambda i, f, be, bv: (be[i], 0, f_idx(i, f, bv))),
                pl.BlockSpec((1, tf, d), lambda i, f, be, bv: (be[i], f_idx(i, f, bv), 0)),
                pl.BlockSpec((1, 1, d), lambda i, f, be, bv: (be[i], 0, 0)),
            ],
            out_specs=pl.BlockSpec((MOE_ROWS, d), lambda i, f, be, bv: (i, 0)),
            scratch_shapes=[pltpu.VMEM((d, tf), BF16), pltpu.VMEM((d, tf), BF16), pltpu.VMEM((tf, d), BF16)],
        ),
        out_shape=jax.ShapeDtypeStruct((cap, d), F32),
        compiler_params=_params("arbitrary", "arbitrary"),
        name="moe_experts",
    )(blk_e, blk_valid, xs, w_gate, b_gate.reshape(n_exp, 1, ff), w_up, b_up.reshape(n_exp, 1, ff),
      w_down, b_down.reshape(n_exp, 1, d))


def _moe_combine_kernel(idx_ref, yb_hbm, g_ref, x_ref, mod_ref, lg_ref, lb_ref, o_ref, buf_ref, sem, *, alpha):
    tc = x_ref.shape[0]

    def issue(r, carry):
        for k in range(TOP_K):
            _row_copy(yb_hbm, buf_ref.at[k], sem, idx_ref[0, 0, r * TOP_K + k], r).start()
        return carry

    lax.fori_loop(0, tc, issue, 0)

    def drain(r, carry):
        for k in range(TOP_K):
            _row_copy(yb_hbm, buf_ref.at[k], sem, 0, r).wait()
        return carry

    lax.fori_loop(0, tc, drain, 0)
    g = g_ref[...]
    y = g[:, 0:1] * buf_ref[0]
    for k in range(1, TOP_K):
        y = y + g[:, k:k + 1] * buf_ref[k]
    x2 = _ln(alpha * x_ref[...] + mod_ref[0, 5:6, :] * y) * lg_ref[...] + lb_ref[...]
    o_ref[...] = x2


def _moe_combine(yb, dest, gates, x1, mod, ln_g, ln_b, n_prompt, t_sample, alpha, tc=128):
    m, d = x1.shape
    tc = min(tc, t_sample, n_prompt)
    return pl.pallas_call(
        functools.partial(_moe_combine_kernel, alpha=alpha),
        grid=(m // tc,),
        in_specs=[
            pl.BlockSpec((1, 1, tc * TOP_K), lambda i: (i, 0, 0), memory_space=pltpu.SMEM),
            pl.BlockSpec(memory_space=pl.ANY),
            pl.BlockSpec((tc, TOP_K), lambda i: (i, 0)),
            pl.BlockSpec((tc, d), lambda i: (i, 0)),
            pl.BlockSpec((1, 6, d), lambda i: (_mod_row(i, tc, n_prompt, t_sample), 0, 0)),
            pl.BlockSpec((1, d), lambda i: (0, 0)),
            pl.BlockSpec((1, d), lambda i: (0, 0)),
        ],
        out_specs=pl.BlockSpec((tc, d), lambda i: (i, 0)),
        out_shape=jax.ShapeDtypeStruct((m, d), F32),
        scratch_shapes=[pltpu.VMEM((TOP_K, tc, d), F32), pltpu.SemaphoreType.DMA],
        compiler_params=_params("arbitrary"),
        name="moe_combine",
    )(dest.reshape(m // tc, 1, tc * TOP_K), yb, gates, x1, mod, ln_g.reshape(1, d), ln_b.reshape(1, d))


def _moe_plan(route_i, counts, n_exp):
    m = route_i.shape[0]
    top_e = route_i[:, :TOP_K]
    rank = route_i[:, TOP_K:]
    counts = counts[0, :n_exp].astype(jnp.int32)
    padded = (counts + MOE_ROWS - 1) // MOE_ROWS * MOE_ROWS
    pend = jnp.cumsum(padded)
    pstart = pend - padded
    dest = pstart[top_e] + rank
    nblk = (m * TOP_K + n_exp * (MOE_ROWS - 1)) // MOE_ROWS
    cap = nblk * MOE_ROWS
    tok = jnp.broadcast_to(jnp.arange(m, dtype=jnp.int32)[:, None], (m, TOP_K))
    buf_tok = jnp.zeros((cap,), jnp.int32).at[dest.reshape(-1)].set(tok.reshape(-1))
    blk_row = jnp.arange(nblk, dtype=jnp.int32) * MOE_ROWS
    n_used = pend[-1] // MOE_ROWS
    blk_e = jnp.minimum(jnp.searchsorted(pend, blk_row, side="right"), n_exp - 1).astype(jnp.int32)
    blk_valid = jnp.clip(counts[blk_e] - (blk_row - pstart[blk_e]), 0, MOE_ROWS).astype(jnp.int32)
    used = jnp.arange(nblk, dtype=jnp.int32) < n_used
    blk_valid = jnp.where(used, blk_valid, 0)
    blk_e = jnp.where(used, blk_e, blk_e[jnp.maximum(n_used - 1, 0)])
    n_used_sub = (pend[-1] // MOE_SUB).astype(jnp.int32).reshape(1)
    return dest.astype(jnp.int32), buf_tok, blk_e, blk_valid, n_used_sub


def _rot_half_cols(w):
    w1, w2, w3, w4 = jnp.split(w, 4, axis=-1)
    return jnp.concatenate([-w2, w1, -w4, w3], axis=-1)


def _rope_tables(n_prompt, n_batch, t_sample, rope, dh):
    rows = t_sample // GRID_W
    row = jnp.repeat(jnp.arange(rows), GRID_W).astype(F32)
    col = jnp.tile(jnp.arange(GRID_W), rows).astype(F32)
    n_freq = rope // 4
    inv = ROPE_BASE ** (-jnp.arange(n_freq, dtype=F32) / n_freq)
    ar = row[:, None] * inv
    ac = col[:, None] * inv
    ang = jnp.concatenate([ar, ar, ac, ac], axis=-1)
    cos = jnp.concatenate([jnp.ones((n_prompt, rope), F32), jnp.tile(jnp.cos(ang), (n_batch, 1))], axis=0)
    sin = jnp.concatenate([jnp.zeros((n_prompt, rope), F32), jnp.tile(jnp.sin(ang), (n_batch, 1))], axis=0)
    n = cos.shape[0]
    hw = 2 * dh
    pad = jnp.zeros((n, hw - dh - rope), F32)
    tq = jnp.concatenate([jnp.ones((n, dh), F32), cos, pad, jnp.zeros((n, dh), F32), sin, pad], axis=-1)
    tk = jnp.concatenate([cos, sin], axis=-1)
    return tq, tk


def kernel(x_prompt, x_sample, c, cache_ckv, cache_kpe, state_gla_fwd, state_gla_bwd, c_ctx, w_mod, b_mod, w_in,
           conv_w_dw, conv_b_dw, conv_ln_g, conv_ln_b, conv_w_out, gla_w_gate_fwd, gla_b_gate_fwd,
           gla_w_gate_bwd, gla_b_gate_bwd, gla_norm_g, gla_w_out, mla_q_norm_g, mla_w_uq, mla_kv_norm_g,
           mla_w_uk, mla_w_uv, mla_w_out, w_out, ln1_g, ln1_b, router_w, router_b, exp_w_gate, exp_b_gate,
           exp_w_up, exp_b_up, exp_w_down, exp_b_down, ln2_g, ln2_b):
    bp, tp, d = x_prompt.shape
    bs, ts, _ = x_sample.shape
    depth = w_in.shape[0]
    n_prompt, n_sample = bp * tp, bs * ts
    n_tok = n_prompt + n_sample
    conv_dim = conv_w_dw.shape[2]
    gla_heads, gla_dk, gla_dv = state_gla_fwd.shape[2:]
    gla_qk, gla_v = gla_heads * gla_dk, gla_heads * gla_dv
    gate_rank = gla_w_gate_fwd.shape[1]
    q_rank = mla_w_uq.shape[1]
    kv_rank = mla_w_uk.shape[1]
    rope = cache_kpe.shape[3]
    dh = mla_w_uk.shape[2] // MLA_HEADS
    n_exp = router_w.shape[2]
    past = cache_ckv.shape[2]
    alpha = (2 * depth) ** 0.25
    assert 2 * dh == 2 * LANES and rope == LANES // 2 and 2 * gate_rank <= LANES

    sizes = (conv_dim, conv_dim, gla_qk, gla_qk, gla_v, gla_v, gate_rank, gate_rank, q_rank, kv_rank, rope, d, d, d)
    offs = [0]
    for s in sizes:
        offs.append(offs[-1] + s)
    o_zf, o_cq, o_kpe, o_ma = offs[6], offs[8], offs[10], offs[11]
    c_ca, c_cg, c_gq, c_gk, c_gv, c_gr = offs[0], offs[1], offs[2], offs[3], offs[4], offs[5]
    c_cq = o_zf
    c_ckv = c_cq + q_rank
    c_ma = c_ckv + kv_rank
    c_pe = c_ma + 3 * d
    c_z = c_pe + LANES
    n_in = c_z + LANES

    cc = jnp.concatenate([c_ctx[None, :], c], axis=0)
    mp = (cc.shape[0] + 7) // 8 * 8
    cc = jnp.pad(cc, ((0, mp - cc.shape[0]), (0, 0)))
    mod_all = _modulation(cc, w_mod, b_mod).reshape(depth, mp, 6, d)

    tq_tab, tk_tab = _rope_tables(n_prompt, bs, ts, rope, dh)
    x = jnp.concatenate([x_prompt.reshape(n_prompt, d), x_sample.reshape(n_sample, d)], axis=0)
    scale = (dh + rope) ** -0.5

    ckv_l, kpe_l, sf_l, sb_l = [], [], [], []
    for l in range(depth):
        mod = mod_all[l]
        wl = w_in[l]
        kpe_cols = wl[:, o_kpe:o_kpe + rope]
        w_in_r = jnp.concatenate(
            [wl[:, :o_zf], wl[:, o_cq:o_kpe], wl[:, o_ma:], kpe_cols, _rot_half_cols(kpe_cols),
             wl[:, o_zf:o_cq], jnp.zeros((d, LANES - 2 * gate_rank), F32)], axis=1).astype(BF16)
        assert w_in_r.shape[1] == n_in
        zpad = jnp.zeros((LANES - 2 * gate_rank, gla_qk), F32)
        wgf = jnp.concatenate([gla_w_gate_fwd[l], jnp.zeros((gate_rank, gla_qk), F32), zpad], axis=0)
        wgb = jnp.concatenate([jnp.zeros((gate_rank, gla_qk), F32), gla_w_gate_bwd[l], zpad], axis=0)
        wq3 = mla_w_uq[l].reshape(q_rank, MLA_HEADS, dh + rope)
        zq = jnp.zeros((q_rank, MLA_HEADS, dh - rope), F32)
        wq = jnp.concatenate([wq3, zq], axis=-1).reshape(q_rank, -1).astype(BF16)
        wqr = jnp.concatenate([jnp.zeros((q_rank, MLA_HEADS, dh), F32), _rot_half_cols(wq3[..., dh:]), zq],
                              axis=-1).reshape(q_rank, -1).astype(BF16)
        wkv = jnp.concatenate([mla_w_uk[l], mla_w_uv[l]], axis=1).astype(BF16)
        rw = jnp.pad(router_w[l], ((0, 0), (0, LANES - n_exp)))
        rb = jnp.pad(router_b[l], (0, LANES - n_exp)).reshape(1, LANES)

        hmat = _gemm_in(x, mod, w_in_r, n_prompt, ts)

        conv_args = (hmat, c_ca, c_cg, conv_w_dw[l], conv_b_dw[l], conv_ln_g[l], conv_ln_b[l])
        ya = (_conv_branch(*conv_args, row_off=0, n_seq=bp, seq=tp),
              _conv_branch(*conv_args, row_off=n_prompt, n_seq=bs, seq=ts))
        merged = _gemm(ya, conv_w_out[l].astype(BF16), gate=hmat, gate_col=c_ma, name="conv_out")

        gla_args = (hmat, (c_gq, c_gk, c_gv, c_gr, c_z), wgf, gla_b_gate_fwd[l], wgb, gla_b_gate_bwd[l],
                    gla_norm_g[l])
        gla_kw = dict(heads=gla_heads, dk=gla_dk, dv=gla_dv)
        og_p, s_f, s_b = _gla_branch(*gla_args, row_off=0, n_seq=bp, seq=tp, **gla_kw)
        og_s = _gla_branch(*gla_args, row_off=n_prompt, n_seq=bs, seq=ts, **gla_kw,
                           state=(state_gla_fwd[:, l], state_gla_bwd[:, l]))
        merged = _gemm((og_p, og_s), gla_w_out[l].astype(BF16), gate=hmat, gate_col=c_ma + d, prev=merged, name="gla_out")

        q_all, ckv, kpe = _mla_proj(hmat, c_cq, c_ckv, c_pe, mla_q_norm_g[l], mla_kv_norm_g[l], wq, wqr,
                                    tq_tab, tk_tab)
        kv_p = _gemm(ckv, wkv, out_dtype=BF16, rows=n_prompt, name="mla_kv").reshape(bp, tp, -1)
        keys_s = jnp.concatenate([cache_ckv[:, l], ckv[n_prompt:].reshape(bs, ts, kv_rank)], axis=1)
        kv_s = _gemm(keys_s.reshape(bs * (past + ts), kv_rank), wkv, out_dtype=BF16, name="mla_kv")
        kv_s = kv_s.reshape(bs, past + ts, -1)
        kpe_p = kpe[:n_prompt].reshape(bp, tp, LANES)
        kpe_s = jnp.concatenate([jnp.pad(cache_kpe[:, l], ((0, 0), (0, 0), (0, LANES - rope))),
                                 kpe[n_prompt:].reshape(bs, ts, LANES)], axis=1)
        att_kw = dict(heads=MLA_HEADS, dh=dh, scale=scale)
        att = (_attention(q_all, kv_p, kpe_p, row_off=0, n_seq=bp, seq=tp, **att_kw),
               _attention(q_all, kv_s, kpe_s, row_off=n_prompt, n_seq=bs, seq=ts, **att_kw))
        merged = _gemm(att, mla_w_out[l].astype(BF16), gate=hmat, gate_col=c_ma + 2 * d, prev=merged,
                       out_dtype=BF16, name="mla_out")

        x1, h2, route_i, route_g, counts = _post_attn(merged, x, w_out[l].astype(BF16), mod, ln1_g[l], ln1_b[l],
                                                      rw, rb, n_prompt, ts, alpha, n_exp)

        dest, buf_tok, blk_e, blk_valid, n_used_sub = _moe_plan(route_i, counts, n_exp)
        xs = _moe_gather(h2, buf_tok, n_used_sub)
        yb = _moe_experts(xs, blk_e, blk_valid, exp_w_gate[l], exp_b_gate[l], exp_w_up[l], exp_b_up[l],
                          exp_w_down[l], exp_b_down[l])
        x = _moe_combine(yb, dest, route_g, x1, mod, ln2_g[l], ln2_b[l], n_prompt, ts, alpha)

        ckv_l.append(ckv[:n_prompt].reshape(bp, tp, kv_rank))
        kpe_l.append(kpe_p[..., :rope])
        sf_l.append(s_f)
        sb_l.append(s_b)

    y_prompt = x[:n_prompt].reshape(bp, tp, d)
    y_sample = x[n_prompt:].reshape(bs, ts, d)
    return (y_prompt, y_sample, jnp.stack(ckv_l, axis=1), jnp.stack(kpe_l, axis=1),
            jnp.stack(sf_l, axis=1), jnp.stack(sb_l, axis=1))
```

```python
import functools

import jax
import jax.numpy as jnp
from jax import lax
from jax.experimental import pallas as pl
from jax.experimental.pallas import tpu as pltpu

F32 = jnp.float32
BF16 = jnp.bfloat16

LANES = 128
VMEM_LIMIT_BYTES = 56 * 1024 * 1024

GRID_W = 64
GLA_TAU = 16.0
GLA_CHUNK = 64
MLA_HEADS = 8
ROPE_BASE = 10000.0
TOP_K = 4
SWIGLU_LIMIT = 7.0
SWIGLU_ALPHA = 1.702
LN_EPS = 1e-5
RMS_EPS = 1e-6

CONV_PAD = 16
MOE_ROWS = 1024
MOE_SUB = 512
MOE_GATHER_ROWS = 256


def _params(*sem):
    return pltpu.CompilerParams(dimension_semantics=sem, vmem_limit_bytes=VMEM_LIMIT_BYTES)


def _tile(n, pref):
    if n <= pref:
        return n
    t = (pref // LANES) * LANES
    while t > LANES and n % t:
        t -= LANES
    assert n % t == 0, (n, pref)
    return t


def _ln(x):
    mu = jnp.mean(x, axis=-1, keepdims=True)
    xc = x - mu
    return xc * lax.rsqrt(jnp.mean(xc * xc, axis=-1, keepdims=True) + LN_EPS)


def _rms(x):
    return x * lax.rsqrt(jnp.mean(x * x, axis=-1, keepdims=True) + RMS_EPS)


def _silu(x):
    return x * jax.nn.sigmoid(x)


def _dot(a, b):
    return jnp.dot(a, b, preferred_element_type=F32)


def _dot_nt(a, b):
    return lax.dot_general(a, b, (((1,), (1,)), ((), ())), preferred_element_type=F32)


def _dot_tn(a, b):
    return lax.dot_general(a, b, (((0,), (0,)), ((), ())), preferred_element_type=F32)


def _mod_kernel(c_ref, w_ref, b_ref, o_ref):
    s = _silu(c_ref[...])
    o_ref[0] = _dot(s.astype(BF16), w_ref[0].astype(BF16)) + b_ref[0]


def _modulation(cc, w_mod, b_mod):
    n_layers, d, d6 = w_mod.shape
    mp = cc.shape[0]
    tn = _tile(d6, 1024)
    return pl.pallas_call(
        _mod_kernel,
        grid=(n_layers, d6 // tn),
        in_specs=[
            pl.BlockSpec((mp, d), lambda l, j: (0, 0)),
            pl.BlockSpec((1, d, tn), lambda l, j: (l, 0, j)),
            pl.BlockSpec((1, 1, tn), lambda l, j: (l, 0, j)),
        ],
        out_specs=pl.BlockSpec((1, mp, tn), lambda l, j: (l, 0, j)),
        out_shape=jax.ShapeDtypeStruct((n_layers, mp, d6), F32),
        compiler_params=_params("parallel", "parallel"),
        name="modulation",
    )(cc, w_mod, b_mod.reshape(n_layers, 1, d6))


def _mod_row(i, tm, n_prompt, t_sample):
    r = i * tm
    return jnp.where(r < n_prompt, 0, 1 + (r - n_prompt) // t_sample)


def _gemm_in_kernel(x_ref, mod_ref, w_ref, o_ref, h_ref):
    @pl.when(pl.program_id(1) == 0)
    def _():
        y = _ln(x_ref[...])
        h_ref[...] = (y * (1.0 + mod_ref[0, 1:2, :]) + mod_ref[0, 0:1, :]).astype(BF16)

    o_ref[...] = _dot(h_ref[...], w_ref[...])


def _gemm_in(x, mod, w, n_prompt, t_sample, tm=1024, tn=896):
    m, d = x.shape
    n = w.shape[1]
    tm = min(tm, t_sample, n_prompt)
    tn = _tile(n, tn)
    assert m % tm == 0 and n_prompt % tm == 0 and t_sample % tm == 0
    return pl.pallas_call(
        _gemm_in_kernel,
        grid=(m // tm, n // tn),
        in_specs=[
            pl.BlockSpec((tm, d), lambda i, j: (i, 0)),
            pl.BlockSpec((1, 6, d), lambda i, j: (_mod_row(i, tm, n_prompt, t_sample), 0, 0)),
            pl.BlockSpec((d, tn), lambda i, j: (0, j)),
        ],
        out_specs=pl.BlockSpec((tm, tn), lambda i, j: (i, j)),
        out_shape=jax.ShapeDtypeStruct((m, n), F32),
        scratch_shapes=[pltpu.VMEM((tm, d), BF16)],
        compiler_params=_params("parallel", "arbitrary"),
        name="gemm_in",
    )(x, mod, w)


def _gemm_kernel(*refs, n_x, nb_first, has_gate, has_prev):
    it = iter(refs)
    x_refs = [next(it) for _ in range(n_x)]
    w_ref = next(it)
    g_ref = next(it) if has_gate else None
    p_ref = next(it) if has_prev else None
    o_ref = next(it)

    def compute(x_ref):
        acc = _dot(x_ref[...].astype(BF16), w_ref[...].astype(BF16))
        if has_gate:
            acc = jax.nn.sigmoid(g_ref[...]) * acc
        if has_prev:
            acc = acc + p_ref[...]
        o_ref[...] = acc.astype(o_ref.dtype)

    if n_x == 1:
        compute(x_refs[0])
    else:
        i = pl.program_id(0)
        pl.when(i < nb_first)(lambda: compute(x_refs[0]))
        pl.when(i >= nb_first)(lambda: compute(x_refs[1]))


def _gemm(x, w, *, gate=None, gate_col=0, prev=None, out_dtype=F32, rows=None, tm=512, tn=1024, name="gemm"):
    xs = x if isinstance(x, tuple) else (x,)
    k, n = w.shape
    m = sum(xi.shape[0] for xi in xs) if rows is None else rows
    tm = min([tm, m] + [xi.shape[0] for xi in xs])
    tn = _tile(n, tn)
    assert m % tm == 0 and all(xi.shape[1] == k and xi.shape[0] % tm == 0 for xi in xs)
    nb_first = xs[0].shape[0] // tm
    if len(xs) == 1:
        in_specs = [pl.BlockSpec((tm, k), lambda i, j: (i, 0))]
    else:
        in_specs = [pl.BlockSpec((tm, k), lambda i, j: (jnp.minimum(i, nb_first - 1), 0)),
                    pl.BlockSpec((tm, k), lambda i, j: (jnp.maximum(i - nb_first, 0), 0))]
    in_specs.append(pl.BlockSpec((k, tn), lambda i, j: (0, j)))
    args = list(xs) + [w]
    if gate is not None:
        assert gate_col % tn == 0
        goff = gate_col // tn
        in_specs.append(pl.BlockSpec((tm, tn), lambda i, j: (i, goff + j)))
        args.append(gate)
    if prev is not None:
        in_specs.append(pl.BlockSpec((tm, tn), lambda i, j: (i, j)))
        args.append(prev)
    return pl.pallas_call(
        functools.partial(_gemm_kernel, n_x=len(xs), nb_first=nb_first, has_gate=gate is not None,
                          has_prev=prev is not None),
        grid=(m // tm, n // tn),
        in_specs=in_specs,
        out_specs=pl.BlockSpec((tm, tn), lambda i, j: (i, j)),
        out_shape=jax.ShapeDtypeStruct((m, n), out_dtype),
        compiler_params=_params("parallel", "parallel"),
        name=name,
    )(*args)


def _conv_kernel(a_ref, g_ref, wdw_ref, bdw_ref, lng_ref, lnb_ref, o_ref, upad_ref, yc_ref, *, seq, rb, rc, width):
    c = a_ref.shape[1]
    r = pl.program_id(1)
    fill_rows = min(seq, 64)

    @pl.when(r == 0)
    def _():
        zeros = jnp.zeros((CONV_PAD, c), F32)
        upad_ref[0:CONV_PAD, :] = zeros
        upad_ref[CONV_PAD + seq:CONV_PAD + seq + CONV_PAD, :] = zeros

        def fill(i, carry):
            src = pl.ds(pl.multiple_of(i * fill_rows, 8), fill_rows)
            dst = pl.ds(pl.multiple_of(CONV_PAD + i * fill_rows, 8), fill_rows)
            upad_ref[dst, :] = a_ref[src, :] * jax.nn.sigmoid(g_ref[src, :])
            return carry

        lax.fori_loop(0, seq // fill_rows, fill, 0)

    lead = CONV_PAD - width // 2
    n_shift = 8
    n_al = (lead + width - 1) // n_shift + 1
    win = rc + n_shift * n_al

    def row_chunk(ci, carry):
        base = pl.multiple_of(r * rb + ci * rc, 8)
        for cj in range(c // LANES):
            lanes = slice(cj * LANES, (cj + 1) * LANES)
            w_all = upad_ref[pl.ds(base, win), lanes]
            acc = jnp.zeros((rc, LANES), F32)
            for b in range(n_shift):
                taps = [(a8, n_shift * a8 + b - lead) for a8 in range(n_al)]
                taps = [(a8, k) for a8, k in taps if 0 <= k < width]
                if not taps:
                    continue
                w_b = w_all[b:b + rc + n_shift * (n_al - 1)]
                part = None
                for a8, k in taps:
                    term = wdw_ref[k:k + 1, lanes] * w_b[n_shift * a8:n_shift * a8 + rc]
                    part = term if part is None else part + term
                acc = acc + part
            yc_ref[pl.ds(pl.multiple_of(ci * rc, 8), rc), lanes] = acc + bdw_ref[:, lanes]
        return carry

    lax.fori_loop(0, rb // rc, row_chunk, 0)
    y = _ln(yc_ref[...]) * lng_ref[...] + lnb_ref[...]
    o_ref[...] = _silu(y).astype(o_ref.dtype)


def _conv_branch(hmat, col_a, col_g, w_dw, b_dw, ln_g, ln_b, *, row_off, n_seq, seq):
    width, c = w_dw.shape
    rb = min(seq, 256)
    rc = min(rb, 64)
    assert width // 2 <= CONV_PAD and row_off % seq == 0 and col_a % c == 0 and col_g % c == 0
    s_off = row_off // seq
    in_specs = [
        pl.BlockSpec((seq, c), lambda s, r: (s_off + s, col_a // c)),
        pl.BlockSpec((seq, c), lambda s, r: (s_off + s, col_g // c)),
        pl.BlockSpec((width, c), lambda s, r: (0, 0)),
        pl.BlockSpec((1, c), lambda s, r: (0, 0)),
        pl.BlockSpec((1, c), lambda s, r: (0, 0)),
        pl.BlockSpec((1, c), lambda s, r: (0, 0)),
    ]
    args = [hmat, hmat, w_dw, b_dw.reshape(1, c), ln_g.reshape(1, c), ln_b.reshape(1, c)]
    return pl.pallas_call(
        functools.partial(_conv_kernel, seq=seq, rb=rb, rc=rc, width=width),
        grid=(n_seq, seq // rb),
        in_specs=in_specs,
        out_specs=pl.BlockSpec((rb, c), lambda s, r: (s * (seq // rb) + r, 0)),
        out_shape=jax.ShapeDtypeStruct((n_seq * seq, c), BF16),
        scratch_shapes=[pltpu.VMEM((seq + 2 * CONV_PAD, c), F32), pltpu.VMEM((rb, c), F32)],
        compiler_params=_params("parallel", "arbitrary"),
        name="conv_branch",
    )(*args)


def _cumsum3(tri, la):
    hi = la.astype(BF16)
    r1 = la - hi.astype(F32)
    mid = r1.astype(BF16)
    lo = (r1 - mid.astype(F32)).astype(BF16)
    return _dot(tri, hi) + _dot(tri, mid) + _dot(tri, lo)


def _log_sigmoid(x):
    return jnp.minimum(x, 0.0) - jnp.log(1.0 + jnp.exp(-jnp.abs(x)))


def _gla_kernel(*refs, seq, chunk, has_state):
    it = iter(refs)
    q_ref, k_ref, v_ref, r_ref, z_ref = (next(it) for _ in range(5))
    wgf_ref, bgf_ref, wgb_ref, bgb_ref, ng_ref = (next(it) for _ in range(5))
    if has_state:
        s0f_ref, s0b_ref = next(it), next(it)
        o_ref = next(it)
        sf_ref = sb_ref = None
    else:
        o_ref, sf_ref, sb_ref = next(it), next(it), next(it)
    laf_ref, lab_ref, oacc_ref, kvf_ref, kvb_ref, qdf_ref, qdb_ref, decf_ref, decb_ref = (next(it) for _ in range(9))

    dk = q_ref.shape[1]
    dv = v_ref.shape[1]
    n_chunks = seq // chunk
    z = z_ref[...].astype(BF16)
    laf_ref[...] = _log_sigmoid(_dot(z, wgf_ref[...].astype(BF16)) + bgf_ref[...]) / GLA_TAU
    lab_ref[...] = _log_sigmoid(_dot(z, wgb_ref[...].astype(BF16)) + bgb_ref[...]) / GLA_TAU

    row = lax.broadcasted_iota(jnp.int32, (chunk, chunk), 0)
    col = lax.broadcasted_iota(jnp.int32, (chunk, chunk), 1)
    lower = row >= col
    upper = row <= col
    tri_f = jnp.where(lower, 1.0, 0.0).astype(BF16)
    tri_b = jnp.where(upper, 1.0, 0.0).astype(BF16)
    q_scale = dk ** -0.5

    def intra(n, rows, la_ref, tri, mask, edge, qd_ref, kv_ref, dec_ref):
        b = _cumsum3(tri, la_ref[rows, :])
        b_edge = b[edge:edge + 1, :]
        q = q_ref[rows, :] * q_scale
        k = k_ref[rows, :]
        v = v_ref[rows, :].astype(BF16)
        qd = (q * jnp.exp(b)).astype(BF16)
        kd = (k * jnp.exp(-b)).astype(BF16)
        ke = (k * jnp.exp(b_edge - b)).astype(BF16)
        att = jnp.where(mask, _dot_nt(qd, kd), 0.0).astype(BF16)
        qd_ref[rows, :] = qd
        kv_ref[n] = _dot_tn(v, ke)
        dec_ref[n] = jnp.broadcast_to(jnp.exp(b_edge), dec_ref.shape[1:])
        return _dot(att, v)

    def phase1(n, carry):
        rows = pl.ds(pl.multiple_of(n * chunk, chunk), chunk)
        oacc_ref[rows, :] = (intra(n, rows, laf_ref, tri_f, lower, chunk - 1, qdf_ref, kvf_ref, decf_ref)
                             + intra(n, rows, lab_ref, tri_b, upper, 0, qdb_ref, kvb_ref, decb_ref))
        return carry

    lax.fori_loop(0, n_chunks, phase1, 0, unroll=2)

    def scan(order, kv_ref, dec_ref, s0):
        def body(j, s):
            n = order(j)
            kv = kv_ref[n]
            kv_ref[n] = s
            return s * dec_ref[n, 0:1, :] + kv

        return lax.fori_loop(0, n_chunks, body, s0)

    zero = jnp.zeros((dv, dk), F32)
    s_f = scan(lambda j: j, kvf_ref, decf_ref, s0f_ref[0, 0].T if has_state else zero)
    s_b = scan(lambda j: n_chunks - 1 - j, kvb_ref, decb_ref, s0b_ref[0, 0].T if has_state else zero)
    if not has_state:
        sf_ref[0, 0] = s_f.T
        sb_ref[0, 0] = s_b.T

    def phase3(n, carry):
        rows = pl.ds(pl.multiple_of(n * chunk, chunk), chunk)
        o = (oacc_ref[rows, :] + _dot_nt(qdf_ref[rows, :], kvf_ref[n].astype(BF16))
             + _dot_nt(qdb_ref[rows, :], kvb_ref[n].astype(BF16)))
        o = _rms(o) * ng_ref[...]
        o_ref[rows, :] = (o * _silu(r_ref[rows, :])).astype(o_ref.dtype)
        return carry

    lax.fori_loop(0, n_chunks, phase3, 0, unroll=2)


def _gla_branch(hmat, cols, wgf, bgf, wgb, bgb, norm_g, *, heads, dk, dv, row_off, n_seq, seq, state=None):
    col_q, col_k, col_v, col_r, col_z = cols
    s_off = row_off // seq
    has_state = state is not None
    in_specs = [
        pl.BlockSpec((seq, dk), lambda s, h: (s_off + s, col_q // dk + h)),
        pl.BlockSpec((seq, dk), lambda s, h: (s_off + s, col_k // dk + h)),
        pl.BlockSpec((seq, dv), lambda s, h: (s_off + s, col_v // dv + h)),
        pl.BlockSpec((seq, dv), lambda s, h: (s_off + s, col_r // dv + h)),
        pl.BlockSpec((seq, LANES), lambda s, h: (s_off + s, col_z // LANES)),
        pl.BlockSpec((LANES, dk), lambda s, h: (0, h)),
        pl.BlockSpec((1, dk), lambda s, h: (0, h)),
        pl.BlockSpec((LANES, dk), lambda s, h: (0, h)),
        pl.BlockSpec((1, dk), lambda s, h: (0, h)),
        pl.BlockSpec((1, dv), lambda s, h: (0, h)),
    ]
    args = [hmat] * 5 + [wgf, bgf.reshape(1, -1), wgb, bgb.reshape(1, -1), norm_g.reshape(1, -1)]
    o_spec = pl.BlockSpec((seq, dv), lambda s, h: (s, h))
    o_shape = jax.ShapeDtypeStruct((n_seq * seq, heads * dv), BF16)
    if has_state:
        st_spec = pl.BlockSpec((1, 1, dk, dv), lambda s, h: (s, h, 0, 0))
        in_specs += [st_spec, st_spec]
        args += [state[0], state[1]]
        out_specs, out_shape = o_spec, o_shape
    else:
        st_spec = pl.BlockSpec((1, 1, dk, dv), lambda s, h: (s, h, 0, 0))
        st_shape = jax.ShapeDtypeStruct((n_seq, heads, dk, dv), F32)
        out_specs, out_shape = [o_spec, st_spec, st_spec], [o_shape, st_shape, st_shape]
    chunk = min(GLA_CHUNK, seq)
    n_chunks = seq // chunk
    return pl.pallas_call(
        functools.partial(_gla_kernel, seq=seq, chunk=chunk, has_state=has_state),
        grid=(n_seq, heads),
        in_specs=in_specs,
        out_specs=out_specs,
        out_shape=out_shape,
        scratch_shapes=[pltpu.VMEM((seq, dk), F32), pltpu.VMEM((seq, dk), F32), pltpu.VMEM((seq, dv), F32),
                        pltpu.VMEM((n_chunks, dv, dk), F32), pltpu.VMEM((n_chunks, dv, dk), F32),
                        pltpu.VMEM((seq, dk), BF16), pltpu.VMEM((seq, dk), BF16),
                        pltpu.VMEM((n_chunks, 8, dk), F32), pltpu.VMEM((n_chunks, 8, dk), F32)],
        compiler_params=_params("parallel", "parallel"),
        name="gla_branch",
    )(*args)


def _mla_proj_kernel(cq_ref, ckv_ref, pe_ref, qg_ref, kvg_ref, wq_ref, wqr_ref, tq_ref, tk_ref,
                     q_ref, ckvn_ref, kpe_ref, *, hw):
    cqn = (_rms(cq_ref[...]) * qg_ref[...]).astype(BF16)
    a = _dot(cqn, wq_ref[...])
    ar = _dot(cqn, wqr_ref[...])
    cos = tq_ref[:, 0:hw]
    sin = tq_ref[:, hw:2 * hw]
    for h in range(a.shape[1] // hw):
        sl = slice(h * hw, (h + 1) * hw)
        q_ref[:, sl] = (a[:, sl] * cos + ar[:, sl] * sin).astype(q_ref.dtype)
    ckvn_ref[...] = _rms(ckv_ref[...]) * kvg_ref[...]
    v = pe_ref[...] * tk_ref[...]
    v = v + pltpu.roll(v, LANES // 2, axis=1)
    lane = lax.broadcasted_iota(jnp.int32, v.shape, 1)
    kpe_ref[...] = jnp.where(lane < LANES // 2, v, 0.0)


def _mla_proj(hmat, col_cq, col_ckv, col_pe, q_g, kv_g, wq, wqr, tq, tk, tm=512):
    m = hmat.shape[0]
    rq = q_g.shape[0]
    rkv = kv_g.shape[0]
    nq = wq.shape[1]
    hw = tq.shape[1] // 2
    tm = min(tm, m)
    return pl.pallas_call(
        functools.partial(_mla_proj_kernel, hw=hw),
        grid=(m // tm,),
        in_specs=[
            pl.BlockSpec((tm, rq), lambda i: (i, col_cq // rq)),
            pl.BlockSpec((tm, rkv), lambda i: (i, col_ckv // rkv)),
            pl.BlockSpec((tm, LANES), lambda i: (i, col_pe // LANES)),
            pl.BlockSpec((1, rq), lambda i: (0, 0)),
            pl.BlockSpec((1, rkv), lambda i: (0, 0)),
            pl.BlockSpec((rq, nq), lambda i: (0, 0)),
            pl.BlockSpec((rq, nq), lambda i: (0, 0)),
            pl.BlockSpec((tm, 2 * hw), lambda i: (i, 0)),
            pl.BlockSpec((tm, LANES), lambda i: (i, 0)),
        ],
        out_specs=[
            pl.BlockSpec((tm, nq), lambda i: (i, 0)),
            pl.BlockSpec((tm, rkv), lambda i: (i, 0)),
            pl.BlockSpec((tm, LANES), lambda i: (i, 0)),
        ],
        out_shape=[
            jax.ShapeDtypeStruct((m, nq), BF16),
            jax.ShapeDtypeStruct((m, rkv), F32),
            jax.ShapeDtypeStruct((m, LANES), F32),
        ],
        compiler_params=_params("parallel"),
        name="mla_proj",
    )(hmat, hmat, hmat, q_g.reshape(1, rq), kv_g.reshape(1, rkv), wq, wqr, tq, tk)


def _attn_kernel(*refs, heads, hw, dh, scale):
    q_ref, kv_ref, kpe_ref = refs[:3]
    o_ref, kcat_ref = refs[-2:]

    @pl.when(pl.program_id(1) == 0)
    def _():
        kp = kpe_ref[0].astype(BF16)
        for h in range(heads):
            kcat_ref[:, h * hw:h * hw + dh] = kv_ref[0, :, h * dh:(h + 1) * dh]
            kcat_ref[:, h * hw + dh:(h + 1) * hw] = kp

    v_off = heads * dh
    for h in range(heads):
        s = _dot_nt(q_ref[:, h * hw:(h + 1) * hw], kcat_ref[:, h * hw:(h + 1) * hw]) * scale
        p = jnp.exp(s - jnp.max(s, axis=-1, keepdims=True))
        l = jnp.sum(p, axis=-1, keepdims=True)
        o = _dot(p.astype(BF16), kv_ref[0, :, v_off + h * dh:v_off + (h + 1) * dh]) / l
        o_ref[:, h * dh:(h + 1) * dh] = o.astype(o_ref.dtype)


def _attention(q_all, kv, kpe, *, heads, dh, scale, row_off, n_seq, seq, tq=256):
    s_len = kv.shape[1]
    hw = q_all.shape[1] // heads
    tq = min(tq, seq)
    q_off = row_off // tq
    in_specs = [
        pl.BlockSpec((tq, heads * hw), lambda b, i: (q_off + b * (seq // tq) + i, 0)),
        pl.BlockSpec((1, s_len, 2 * heads * dh), lambda b, i: (b, 0, 0)),
        pl.BlockSpec((1, s_len, LANES), lambda b, i: (b, 0, 0)),
    ]
    args = [q_all, kv, kpe]
    return pl.pallas_call(
        functools.partial(_attn_kernel, heads=heads, hw=hw, dh=dh, scale=scale),
        grid=(n_seq, seq // tq),
        in_specs=in_specs,
        out_specs=pl.BlockSpec((tq, heads * dh), lambda b, i: (b * (seq // tq) + i, 0)),
        out_shape=jax.ShapeDtypeStruct((n_seq * seq, heads * dh), BF16),
        scratch_shapes=[pltpu.VMEM((s_len, heads * hw), BF16)],
        compiler_params=_params("parallel", "arbitrary"),
        name="mla_attention",
    )(*args)


def _split3(x):
    hi = x.astype(BF16)
    lo = (x - hi.astype(F32)).astype(BF16)
    return hi, lo


def _post_attn_kernel(m_ref, x_ref, w_ref, mod_ref, g1_ref, b1_ref, rw_ref, rb_ref,
                      x1_ref, h2_ref, ri_ref, rg_ref, cnt_ref, run_ref, tri_ref, *, alpha, n_exp):
    tm = x_ref.shape[0]
    step = pl.program_id(0)

    @pl.when(step == 0)
    def _():
        run_ref[...] = jnp.zeros_like(run_ref)
        row = lax.broadcasted_iota(jnp.int32, (tm, tm), 0)
        col = lax.broadcasted_iota(jnp.int32, (tm, tm), 1)
        tri_ref[...] = jnp.where(row > col, 1.0, 0.0).astype(BF16)

    y = _dot(m_ref[...], w_ref[...])
    x1 = _ln(alpha * x_ref[...] + mod_ref[0, 2:3, :] * y) * g1_ref[...] + b1_ref[...]
    x1_ref[...] = x1
    h2 = _ln(x1) * (1.0 + mod_ref[0, 4:5, :]) + mod_ref[0, 3:4, :]
    half = h2_ref.shape[1]
    hi = pltpu.bitcast(h2[:, :half].astype(BF16).astype(F32), jnp.uint32)
    lo = pltpu.bitcast(h2[:, half:].astype(BF16).astype(F32), jnp.uint32)
    h2_ref[...] = hi | (lo >> 16)

    h_hi, h_lo = _split3(h2)
    w_hi, w_lo = _split3(rw_ref[...])
    logits = _dot(h_hi, w_hi) + _dot(h_lo, w_hi) + _dot(h_hi, w_lo) + rb_ref[...]
    lane = lax.broadcasted_iota(jnp.int32, logits.shape, 1).astype(F32)
    neg = jnp.float32(-jnp.inf)
    logits = jnp.where(lane < n_exp, logits, neg)

    counts = jnp.zeros(logits.shape, F32)
    vals, idxs = [], []
    for _ in range(TOP_K):
        mx = jnp.max(logits, axis=-1, keepdims=True)
        idx = jnp.min(jnp.where(logits == mx, lane, float(LANES)), axis=-1, keepdims=True)
        hit = lane == idx
        counts = counts + jnp.where(hit, 1.0, 0.0)
        logits = jnp.where(hit, neg, logits)
        vals.append(mx)
        idxs.append(idx)

    es = [jnp.exp(v - vals[0]) for v in vals]
    denom = es[0]
    for e in es[1:]:
        denom = denom + e
    before = _dot(tri_ref[...], counts.astype(BF16)) + run_ref[...]
    out_i = jnp.zeros(logits.shape, jnp.int32)
    out_g = jnp.zeros(logits.shape, F32)
    for k in range(TOP_K):
        rank = jnp.sum(jnp.where(lane == idxs[k], before, 0.0), axis=-1, keepdims=True)
        out_i = jnp.where(lane == k, idxs[k].astype(jnp.int32), out_i)
        out_i = jnp.where(lane == TOP_K + k, rank.astype(jnp.int32), out_i)
        out_g = jnp.where(lane == k, es[k] / denom, out_g)
    ri_ref[...] = out_i[:, 0:2 * TOP_K]
    rg_ref[...] = out_g[:, 0:TOP_K]
    run_ref[...] = run_ref[...] + jnp.sum(counts, axis=0, keepdims=True)
    cnt_ref[...] = run_ref[...]


def _post_attn(merged, x, w_out, mod, ln_g, ln_b, router_w, router_b, n_prompt, t_sample, alpha, n_exp, tm=256):
    m, d = x.shape
    tm = min(tm, t_sample, n_prompt)
    row_spec = pl.BlockSpec((tm, d), lambda i: (i, 0))
    vec_spec = pl.BlockSpec((1, d), lambda i: (0, 0))
    return pl.pallas_call(
        functools.partial(_post_attn_kernel, alpha=alpha, n_exp=n_exp),
        grid=(m // tm,),
        in_specs=[
            row_spec, row_spec,
            pl.BlockSpec((d, d), lambda i: (0, 0)),
            pl.BlockSpec((1, 6, d), lambda i: (_mod_row(i, tm, n_prompt, t_sample), 0, 0)),
            vec_spec, vec_spec,
            pl.BlockSpec((d, LANES), lambda i: (0, 0)),
            pl.BlockSpec((1, LANES), lambda i: (0, 0)),
        ],
        out_specs=[
            row_spec,
            pl.BlockSpec((tm, d // 2), lambda i: (i, 0)),
            pl.BlockSpec((tm, 2 * TOP_K), lambda i: (i, 0)),
            pl.BlockSpec((tm, TOP_K), lambda i: (i, 0)),
            pl.BlockSpec((1, LANES), lambda i: (0, 0)),
        ],
        out_shape=[
            jax.ShapeDtypeStruct((m, d), F32),
            jax.ShapeDtypeStruct((m, d // 2), jnp.uint32),
            jax.ShapeDtypeStruct((m, 2 * TOP_K), jnp.int32),
            jax.ShapeDtypeStruct((m, TOP_K), F32),
            jax.ShapeDtypeStruct((1, LANES), F32),
        ],
        scratch_shapes=[pltpu.VMEM((1, LANES), F32), pltpu.VMEM((tm, tm), BF16)],
        compiler_params=_params("arbitrary"),
        name="post_attn_router",
    )(merged, x, w_out, mod, ln_g.reshape(1, d), ln_b.reshape(1, d), router_w, router_b)


def _wait_slot(buf_ref, sem, slot):
    pltpu.make_async_copy(buf_ref.at[slot], buf_ref.at[slot], sem.at[slot]).wait()


def _moe_gather_kernel(used_ref, idx_ref, nxt_ref, h_hbm, o_ref, buf_ref, sem):
    rows, half = buf_ref.shape[1], buf_ref.shape[2]
    blk = pl.program_id(0)
    n_used = used_ref[0]
    slot = blk % 2

    def issue(ids_ref, s):
        def body(r, carry):
            pltpu.make_async_copy(h_hbm.at[pl.ds(ids_ref[0, 0, r], 1), :], buf_ref.at[s, pl.ds(r, 1), :],
                                  sem.at[s]).start()
            return carry

        lax.fori_loop(0, rows, body, 0, unroll=8)

    @pl.when(jnp.logical_and(blk == 0, n_used > 0))
    def _():
        issue(idx_ref, 0)

    @pl.when(blk + 1 < n_used)
    def _():
        issue(nxt_ref, 1 - slot)

    @pl.when(blk < n_used)
    def _():
        _wait_slot(buf_ref, sem, slot)
        u = buf_ref[slot]
        o_ref[:, :half] = pltpu.bitcast(u & jnp.uint32(0xFFFF0000), F32).astype(o_ref.dtype)
        o_ref[:, half:] = pltpu.bitcast(u << 16, F32).astype(o_ref.dtype)

    @pl.when(blk >= n_used)
    def _():
        o_ref[...] = jnp.zeros_like(o_ref)


def _moe_gather(h2p, buf_tok, n_used_sub, rows=MOE_GATHER_ROWS):
    cap = buf_tok.shape[0]
    half = h2p.shape[1]
    d = 2 * half
    nb = cap // rows
    return pl.pallas_call(
        _moe_gather_kernel,
        grid_spec=pltpu.PrefetchScalarGridSpec(
            num_scalar_prefetch=1,
            grid=(nb,),
            in_specs=[
                pl.BlockSpec((1, 1, rows), lambda i, u: (i, 0, 0), memory_space=pltpu.SMEM),
                pl.BlockSpec((1, 1, rows), lambda i, u: (jnp.minimum(i + 1, nb - 1), 0, 0), memory_space=pltpu.SMEM),
                pl.BlockSpec(memory_space=pl.ANY),
            ],
            out_specs=pl.BlockSpec((rows, d), lambda i, u: (i, 0)),
            scratch_shapes=[pltpu.VMEM((2, rows, half), jnp.uint32), pltpu.SemaphoreType.DMA((2,))],
        ),
        out_shape=jax.ShapeDtypeStruct((cap, d), BF16),
        compiler_params=_params("arbitrary"),
        name="moe_gather",
    )(n_used_sub, buf_tok.reshape(nb, 1, rows), buf_tok.reshape(nb, 1, rows), h2p)


def _moe_expert_kernel(be_ref, bv_ref, x_ref, wg_ref, bg_ref, wu_ref, bu_ref, wd_ref, bd_ref,
                       o_ref):
    i = pl.program_id(0)
    f = pl.program_id(1)
    valid = bv_ref[i]
    del be_ref

    @pl.when(f == 0)
    def _():
        o_ref[...] = jnp.broadcast_to(bd_ref[0, 0], o_ref.shape)

    for sb in range(x_ref.shape[0] // MOE_SUB):
        rows = slice(sb * MOE_SUB, (sb + 1) * MOE_SUB)

        @pl.when(sb * MOE_SUB < valid)
        def _():
            x = x_ref[rows, :]
            gt = jnp.minimum(_dot(x, wg_ref[0, 0].astype(BF16)) + bg_ref[0, 0], SWIGLU_LIMIT)
            up = jnp.clip(_dot(x, wu_ref[0, 0].astype(BF16)) + bu_ref[0, 0], -SWIGLU_LIMIT, SWIGLU_LIMIT)
            act = gt * jax.nn.sigmoid(SWIGLU_ALPHA * gt) * (up + 1.0)
            o_ref[rows, :] = o_ref[rows, :] + _dot(act.astype(BF16), wd_ref[0, 0].astype(BF16))


def _moe_experts(xs, blk_e, blk_valid, layer, w_gate, b_gate, w_up, b_up, w_down, b_down, tf=256):
    cap, d = xs.shape
    n_layers, n_exp, _, ff = w_gate.shape
    tf = _tile(ff, tf)
    nf = ff // tf
    nblk = cap // MOE_ROWS

    def f_idx(i, f, bv):
        return jnp.where(bv[i] > 0, f, nf - 1)

    return pl.pallas_call(
        _moe_expert_kernel,
        grid_spec=pltpu.PrefetchScalarGridSpec(
            num_scalar_prefetch=2,
            grid=(nblk, nf),
            in_specs=[
                pl.BlockSpec((MOE_ROWS, d), lambda i, f, be, bv: (i, 0)),
                pl.BlockSpec((1, 1, d, tf), lambda i, f, be, bv: (layer, be[i], 0, f_idx(i, f, bv))),
                pl.BlockSpec((1, 1, 1, tf), lambda i, f, be, bv: (layer, be[i], 0, f_idx(i, f, bv))),
                pl.BlockSpec((1, 1, d, tf), lambda i, f, be, bv: (layer, be[i], 0, f_idx(i, f, bv))),
                pl.BlockSpec((1, 1, 1, tf), lambda i, f, be, bv: (layer, be[i], 0, f_idx(i, f, bv))),
                pl.BlockSpec((1, 1, tf, d), lambda i, f, be, bv: (layer, be[i], f_idx(i, f, bv), 0)),
                pl.BlockSpec((1, 1, 1, d), lambda i, f, be, bv: (layer, be[i], 0, 0)),
            ],
            out_specs=pl.BlockSpec((MOE_ROWS, d), lambda i, f, be, bv: (i, 0)),
        ),
        out_shape=jax.ShapeDtypeStruct((cap, d), F32),
        compiler_params=_params("arbitrary", "arbitrary"),
        name="moe_experts",
    )(blk_e, blk_valid, xs, w_gate, b_gate.reshape(n_layers, n_exp, 1, ff), w_up, b_up.reshape(n_layers, n_exp, 1, ff),
      w_down, b_down.reshape(n_layers, n_exp, 1, d))


def _moe_combine_kernel(idx_ref, nxt_ref, yb_hbm, g_ref, x_ref, mod_ref, lg_ref, lb_ref, o_ref, buf_ref, sem, *,
                        alpha):
    tc = x_ref.shape[0]
    blk = pl.program_id(0)
    slot = blk % 2

    def issue(ids_ref, s):
        def body(r, carry):
            for k in range(TOP_K):
                pltpu.make_async_copy(yb_hbm.at[pl.ds(ids_ref[0, 0, r * TOP_K + k], 1), :],
                                      buf_ref.at[s, k, pl.ds(r, 1), :], sem.at[s]).start()
            return carry

        lax.fori_loop(0, tc, body, 0, unroll=2)

    @pl.when(blk == 0)
    def _():
        issue(idx_ref, 0)

    @pl.when(blk + 1 < pl.num_programs(0))
    def _():
        issue(nxt_ref, 1 - slot)

    _wait_slot(buf_ref, sem, slot)
    g = g_ref[...]
    y = g[:, 0:1] * buf_ref[slot, 0]
    for k in range(1, TOP_K):
        y = y + g[:, k:k + 1] * buf_ref[slot, k]
    x2 = _ln(alpha * x_ref[...] + mod_ref[0, 5:6, :] * y) * lg_ref[...] + lb_ref[...]
    o_ref[...] = x2


def _moe_combine(yb, dest, gates, x1, mod, ln_g, ln_b, n_prompt, t_sample, alpha, tc=128):
    m, d = x1.shape
    tc = min(tc, t_sample, n_prompt)
    nb = m // tc
    ids = dest.reshape(nb, 1, tc * TOP_K)
    return pl.pallas_call(
        functools.partial(_moe_combine_kernel, alpha=alpha),
        grid=(nb,),
        in_specs=[
            pl.BlockSpec((1, 1, tc * TOP_K), lambda i: (i, 0, 0), memory_space=pltpu.SMEM),
            pl.BlockSpec((1, 1, tc * TOP_K), lambda i: (jnp.minimum(i + 1, nb - 1), 0, 0), memory_space=pltpu.SMEM),
            pl.BlockSpec(memory_space=pl.ANY),
            pl.BlockSpec((tc, TOP_K), lambda i: (i, 0)),
            pl.BlockSpec((tc, d), lambda i: (i, 0)),
            pl.BlockSpec((1, 6, d), lambda i: (_mod_row(i, tc, n_prompt, t_sample), 0, 0)),
            pl.BlockSpec((1, d), lambda i: (0, 0)),
            pl.BlockSpec((1, d), lambda i: (0, 0)),
        ],
        out_specs=pl.BlockSpec((tc, d), lambda i: (i, 0)),
        out_shape=jax.ShapeDtypeStruct((m, d), F32),
        scratch_shapes=[pltpu.VMEM((2, TOP_K, tc, d), F32), pltpu.SemaphoreType.DMA((2,))],
        compiler_params=_params("arbitrary"),
        name="moe_combine",
    )(ids, ids, yb, gates, x1, mod, ln_g.reshape(1, d), ln_b.reshape(1, d))


def _moe_plan(route_i, counts, n_exp):
    m = route_i.shape[0]
    top_e = route_i[:, :TOP_K]
    rank = route_i[:, TOP_K:]
    counts = counts[0, :n_exp].astype(jnp.int32)
    padded = (counts + MOE_ROWS - 1) // MOE_ROWS * MOE_ROWS
    pend = jnp.cumsum(padded)
    pstart = pend - padded
    dest = pstart[top_e] + rank
    nblk = (m * TOP_K + n_exp * (MOE_ROWS - 1)) // MOE_ROWS
    cap = nblk * MOE_ROWS
    tok = jnp.broadcast_to(jnp.arange(m, dtype=jnp.int32)[:, None], (m, TOP_K))
    buf_tok = jnp.zeros((cap,), jnp.int32).at[dest.reshape(-1)].set(tok.reshape(-1))
    blk_row = jnp.arange(nblk, dtype=jnp.int32) * MOE_ROWS
    n_used = pend[-1] // MOE_ROWS
    blk_e = jnp.minimum(jnp.searchsorted(pend, blk_row, side="right"), n_exp - 1).astype(jnp.int32)
    blk_valid = jnp.clip(counts[blk_e] - (blk_row - pstart[blk_e]), 0, MOE_ROWS).astype(jnp.int32)
    used = jnp.arange(nblk, dtype=jnp.int32) < n_used
    blk_valid = jnp.where(used, blk_valid, 0)
    blk_e = jnp.where(used, blk_e, blk_e[jnp.maximum(n_used - 1, 0)])
    n_used_sub = (pend[-1] // MOE_GATHER_ROWS).astype(jnp.int32).reshape(1)
    return dest.astype(jnp.int32), buf_tok, blk_e, blk_valid, n_used_sub


def _rot_half_cols(w):
    w1, w2, w3, w4 = jnp.split(w, 4, axis=-1)
    return jnp.concatenate([-w2, w1, -w4, w3], axis=-1)


def _rope_tables(n_prompt, n_batch, t_sample, rope, dh):
    rows = t_sample // GRID_W
    row = jnp.repeat(jnp.arange(rows), GRID_W).astype(F32)
    col = jnp.tile(jnp.arange(GRID_W), rows).astype(F32)
    n_freq = rope // 4
    inv = ROPE_BASE ** (-jnp.arange(n_freq, dtype=F32) / n_freq)
    ar = row[:, None] * inv
    ac = col[:, None] * inv
    ang = jnp.concatenate([ar, ar, ac, ac], axis=-1)
    cos = jnp.concatenate([jnp.ones((n_prompt, rope), F32), jnp.tile(jnp.cos(ang), (n_batch, 1))], axis=0)
    sin = jnp.concatenate([jnp.zeros((n_prompt, rope), F32), jnp.tile(jnp.sin(ang), (n_batch, 1))], axis=0)
    n = cos.shape[0]
    hw = 2 * dh
    pad = jnp.zeros((n, hw - dh - rope), F32)
    tq = jnp.concatenate([jnp.ones((n, dh), F32), cos, pad, jnp.zeros((n, dh), F32), sin, pad], axis=-1)
    tk = jnp.concatenate([cos, sin], axis=-1)
    return tq, tk


def kernel(x_prompt, x_sample, c, cache_ckv, cache_kpe, state_gla_fwd, state_gla_bwd, c_ctx, w_mod, b_mod, w_in,
           conv_w_dw, conv_b_dw, conv_ln_g, conv_ln_b, conv_w_out, gla_w_gate_fwd, gla_b_gate_fwd,
           gla_w_gate_bwd, gla_b_gate_bwd, gla_norm_g, gla_w_out, mla_q_norm_g, mla_w_uq, mla_kv_norm_g,
           mla_w_uk, mla_w_uv, mla_w_out, w_out, ln1_g, ln1_b, router_w, router_b, exp_w_gate, exp_b_gate,
           exp_w_up, exp_b_up, exp_w_down, exp_b_down, ln2_g, ln2_b):
    bp, tp, d = x_prompt.shape
    bs, ts, _ = x_sample.shape
    depth = w_in.shape[0]
    n_prompt, n_sample = bp * tp, bs * ts
    n_tok = n_prompt + n_sample
    conv_dim = conv_w_dw.shape[2]
    gla_heads, gla_dk, gla_dv = state_gla_fwd.shape[2:]
    gla_qk, gla_v = gla_heads * gla_dk, gla_heads * gla_dv
    gate_rank = gla_w_gate_fwd.shape[1]
    q_rank = mla_w_uq.shape[1]
    kv_rank = mla_w_uk.shape[1]
    rope = cache_kpe.shape[3]
    dh = mla_w_uk.shape[2] // MLA_HEADS
    n_exp = router_w.shape[2]
    past = cache_ckv.shape[2]
    alpha = (2 * depth) ** 0.25
    assert 2 * dh == 2 * LANES and rope == LANES // 2 and 2 * gate_rank <= LANES

    sizes = (conv_dim, conv_dim, gla_qk, gla_qk, gla_v, gla_v, gate_rank, gate_rank, q_rank, kv_rank, rope, d, d, d)
    offs = [0]
    for s in sizes:
        offs.append(offs[-1] + s)
    o_zf, o_cq, o_kpe, o_ma = offs[6], offs[8], offs[10], offs[11]
    c_ca, c_cg, c_gq, c_gk, c_gv, c_gr = offs[0], offs[1], offs[2], offs[3], offs[4], offs[5]
    c_cq = o_zf
    c_ckv = c_cq + q_rank
    c_ma = c_ckv + kv_rank
    c_pe = c_ma + 3 * d
    c_z = c_pe + LANES
    n_in = c_z + LANES

    cc = jnp.concatenate([c_ctx[None, :], c], axis=0)
    mp = (cc.shape[0] + 7) // 8 * 8
    cc = jnp.pad(cc, ((0, mp - cc.shape[0]), (0, 0)))
    mod_all = _modulation(cc, w_mod, b_mod).reshape(depth, mp, 6, d)

    tq_tab, tk_tab = _rope_tables(n_prompt, bs, ts, rope, dh)
    x = jnp.concatenate([x_prompt.reshape(n_prompt, d), x_sample.reshape(n_sample, d)], axis=0)
    scale = (dh + rope) ** -0.5

    ckv_l, kpe_l, sf_l, sb_l = [], [], [], []
    for l in range(depth):
        mod = mod_all[l]
        wl = w_in[l]
        kpe_cols = wl[:, o_kpe:o_kpe + rope]
        w_in_r = jnp.concatenate(
            [wl[:, :o_zf], wl[:, o_cq:o_kpe], wl[:, o_ma:], kpe_cols, _rot_half_cols(kpe_cols),
             wl[:, o_zf:o_cq], jnp.zeros((d, LANES - 2 * gate_rank), F32)], axis=1).astype(BF16)
        assert w_in_r.shape[1] == n_in
        zpad = jnp.zeros((LANES - 2 * gate_rank, gla_qk), F32)
        wgf = jnp.concatenate([gla_w_gate_fwd[l], jnp.zeros((gate_rank, gla_qk), F32), zpad], axis=0)
        wgb = jnp.concatenate([jnp.zeros((gate_rank, gla_qk), F32), gla_w_gate_bwd[l], zpad], axis=0)
        wq3 = mla_w_uq[l].reshape(q_rank, MLA_HEADS, dh + rope)
        zq = jnp.zeros((q_rank, MLA_HEADS, dh - rope), F32)
        wq = jnp.concatenate([wq3, zq], axis=-1).reshape(q_rank, -1).astype(BF16)
        wqr = jnp.concatenate([jnp.zeros((q_rank, MLA_HEADS, dh), F32), _rot_half_cols(wq3[..., dh:]), zq],
                              axis=-1).reshape(q_rank, -1).astype(BF16)
        wkv = jnp.concatenate([mla_w_uk[l], mla_w_uv[l]], axis=1).astype(BF16)
        rw = jnp.pad(router_w[l], ((0, 0), (0, LANES - n_exp)))
        rb = jnp.pad(router_b[l], (0, LANES - n_exp)).reshape(1, LANES)

        hmat = _gemm_in(x, mod, w_in_r, n_prompt, ts)

        conv_args = (hmat, c_ca, c_cg, conv_w_dw[l], conv_b_dw[l], conv_ln_g[l], conv_ln_b[l])
        ya = (_conv_branch(*conv_args, row_off=0, n_seq=bp, seq=tp),
              _conv_branch(*conv_args, row_off=n_prompt, n_seq=bs, seq=ts))
        merged = _gemm(ya, conv_w_out[l].astype(BF16), gate=hmat, gate_col=c_ma, name="conv_out")

        gla_args = (hmat, (c_gq, c_gk, c_gv, c_gr, c_z), wgf, gla_b_gate_fwd[l], wgb, gla_b_gate_bwd[l],
                    gla_norm_g[l])
        gla_kw = dict(heads=gla_heads, dk=gla_dk, dv=gla_dv)
        og_p, s_f, s_b = _gla_branch(*gla_args, row_off=0, n_seq=bp, seq=tp, **gla_kw)
        og_s = _gla_branch(*gla_args, row_off=n_prompt, n_seq=bs, seq=ts, **gla_kw,
                           state=(state_gla_fwd[:, l], state_gla_bwd[:, l]))
        merged = _gemm((og_p, og_s), gla_w_out[l].astype(BF16), gate=hmat, gate_col=c_ma + d, prev=merged, name="gla_out")

        q_all, ckv, kpe = _mla_proj(hmat, c_cq, c_ckv, c_pe, mla_q_norm_g[l], mla_kv_norm_g[l], wq, wqr,
                                    tq_tab, tk_tab)
        kv_p = _gemm(ckv, wkv, out_dtype=BF16, rows=n_prompt, name="mla_kv").reshape(bp, tp, -1)
        keys_s = jnp.concatenate([cache_ckv[:, l], ckv[n_prompt:].reshape(bs, ts, kv_rank)], axis=1)
        kv_s = _gemm(keys_s.reshape(bs * (past + ts), kv_rank), wkv, out_dtype=BF16, name="mla_kv")
        kv_s = kv_s.reshape(bs, past + ts, -1)
        kpe_p = kpe[:n_prompt].reshape(bp, tp, LANES)
        kpe_s = jnp.concatenate([jnp.pad(cache_kpe[:, l], ((0, 0), (0, 0), (0, LANES - rope))),
                                 kpe[n_prompt:].reshape(bs, ts, LANES)], axis=1)
        att_kw = dict(heads=MLA_HEADS, dh=dh, scale=scale)
        att = (_attention(q_all, kv_p, kpe_p, row_off=0, n_seq=bp, seq=tp, **att_kw),
               _attention(q_all, kv_s, kpe_s, row_off=n_prompt, n_seq=bs, seq=ts, **att_kw))
        merged = _gemm(att, mla_w_out[l].astype(BF16), gate=hmat, gate_col=c_ma + 2 * d, prev=merged,
                       out_dtype=BF16, name="mla_out")

        x1, h2, route_i, route_g, counts = _post_attn(merged, x, w_out[l].astype(BF16), mod, ln1_g[l], ln1_b[l],
                                                      rw, rb, n_prompt, ts, alpha, n_exp)

        dest, buf_tok, blk_e, blk_valid, n_used_sub = _moe_plan(route_i, counts, n_exp)
        xs = _moe_gather(h2, buf_tok, n_used_sub)
        yb = _moe_experts(xs, blk_e, blk_valid, l, exp_w_gate, exp_b_gate, exp_w_up, exp_b_up, exp_w_down,
                          exp_b_down)
        x = _moe_combine(yb, dest, route_g, x1, mod, ln2_g[l], ln2_b[l], n_prompt, ts, alpha)

        ckv_l.append(ckv[:n_prompt].reshape(bp, tp, kv_rank))
        kpe_l.append(kpe_p[..., :rope])
        sf_l.append(s_f)
        sb_l.append(s_b)

    y_prompt = x[:n_prompt].reshape(bp, tp, d)
    y_sample = x[n_prompt:].reshape(bs, ts, d)
    return (y_prompt, y_sample, jnp.stack(ckv_l, axis=1), jnp.stack(kpe_l, axis=1),
            jnp.stack(sf_l, axis=1), jnp.stack(sb_l, axis=1))
```

```python
import functools

import jax
import jax.numpy as jnp
from jax import lax
from jax.experimental import pallas as pl
from jax.experimental.pallas import tpu as pltpu

F32 = jnp.float32
BF16 = jnp.bfloat16

LANES = 128
VMEM_LIMIT_BYTES = 56 * 1024 * 1024

GRID_W = 64
GLA_TAU = 16.0
GLA_CHUNK = 64
MLA_HEADS = 8
ROPE_BASE = 10000.0
TOP_K = 4
SWIGLU_LIMIT = 7.0
SWIGLU_ALPHA = 1.702
LN_EPS = 1e-5
RMS_EPS = 1e-6

CONV_PAD = 16
MOE_ROWS = 1024
MOE_SUB = 512


def _params(*sem):
    return pltpu.CompilerParams(dimension_semantics=sem, vmem_limit_bytes=VMEM_LIMIT_BYTES)


def _tile(n, pref):
    if n <= pref:
        return n
    t = (pref // LANES) * LANES
    while t > LANES and n % t:
        t -= LANES
    assert n % t == 0, (n, pref)
    return t


def _ln(x):
    mu = jnp.mean(x, axis=-1, keepdims=True)
    xc = x - mu
    return xc * lax.rsqrt(jnp.mean(xc * xc, axis=-1, keepdims=True) + LN_EPS)


def _rms(x):
    return x * lax.rsqrt(jnp.mean(x * x, axis=-1, keepdims=True) + RMS_EPS)


def _silu(x):
    return x * jax.nn.sigmoid(x)


def _dot(a, b):
    return jnp.dot(a, b, preferred_element_type=F32)


def _dot_nt(a, b):
    return lax.dot_general(a, b, (((1,), (1,)), ((), ())), preferred_element_type=F32)


def _dot_tn(a, b):
    return lax.dot_general(a, b, (((0,), (0,)), ((), ())), preferred_element_type=F32)


def _mod_kernel(c_ref, w_ref, b_ref, o_ref):
    s = _silu(c_ref[...])
    o_ref[0] = _dot(s.astype(BF16), w_ref[0].astype(BF16)) + b_ref[0]


def _modulation(cc, w_mod, b_mod):
    n_layers, d, d6 = w_mod.shape
    mp = cc.shape[0]
    tn = _tile(d6, 1024)
    return pl.pallas_call(
        _mod_kernel,
        grid=(n_layers, d6 // tn),
        in_specs=[
            pl.BlockSpec((mp, d), lambda l, j: (0, 0)),
            pl.BlockSpec((1, d, tn), lambda l, j: (l, 0, j)),
            pl.BlockSpec((1, 1, tn), lambda l, j: (l, 0, j)),
        ],
        out_specs=pl.BlockSpec((1, mp, tn), lambda l, j: (l, 0, j)),
        out_shape=jax.ShapeDtypeStruct((n_layers, mp, d6), F32),
        compiler_params=_params("parallel", "parallel"),
        name="modulation",
    )(cc, w_mod, b_mod.reshape(n_layers, 1, d6))


def _mod_row(i, tm, n_prompt, t_sample):
    r = i * tm
    return jnp.where(r < n_prompt, 0, 1 + (r - n_prompt) // t_sample)


def _gemm_in_kernel(x_ref, mod_ref, w_ref, o_ref, h_ref):
    @pl.when(pl.program_id(1) == 0)
    def _():
        y = _ln(x_ref[...])
        h_ref[...] = (y * (1.0 + mod_ref[0, 1:2, :]) + mod_ref[0, 0:1, :]).astype(BF16)

    o_ref[...] = _dot(h_ref[...], w_ref[...])


def _gemm_in(x, mod, w, n_prompt, t_sample, tm=1024, tn=896):
    m, d = x.shape
    n = w.shape[1]
    tm = min(tm, t_sample, n_prompt)
    tn = _tile(n, tn)
    assert m % tm == 0 and n_prompt % tm == 0 and t_sample % tm == 0
    return pl.pallas_call(
        _gemm_in_kernel,
        grid=(m // tm, n // tn),
        in_specs=[
            pl.BlockSpec((tm, d), lambda i, j: (i, 0)),
            pl.BlockSpec((1, 6, d), lambda i, j: (_mod_row(i, tm, n_prompt, t_sample), 0, 0)),
            pl.BlockSpec((d, tn), lambda i, j: (0, j)),
        ],
        out_specs=pl.BlockSpec((tm, tn), lambda i, j: (i, j)),
        out_shape=jax.ShapeDtypeStruct((m, n), F32),
        scratch_shapes=[pltpu.VMEM((tm, d), BF16)],
        compiler_params=_params("parallel", "arbitrary"),
        name="gemm_in",
    )(x, mod, w)


def _gemm_kernel(x_ref, w_ref, o_ref):
    o_ref[...] = _dot(x_ref[...].astype(BF16), w_ref[...]).astype(o_ref.dtype)


def _gemm(x, w, *, rows=None, tm=512, tn=1024, name="gemm"):
    k, n = w.shape
    m = x.shape[0] if rows is None else rows
    tm = min(tm, m)
    tn = _tile(n, tn)
    assert m % tm == 0 and x.shape[1] == k
    return pl.pallas_call(
        _gemm_kernel,
        grid=(m // tm, n // tn),
        in_specs=[pl.BlockSpec((tm, k), lambda i, j: (i, 0)), pl.BlockSpec((k, tn), lambda i, j: (0, j))],
        out_specs=pl.BlockSpec((tm, tn), lambda i, j: (i, j)),
        out_shape=jax.ShapeDtypeStruct((m, n), BF16),
        compiler_params=_params("parallel", "parallel"),
        name=name,
    )(x, w)


def _merge_kernel(*refs, n_br, nb_first):
    x_refs = refs[:2 * n_br]
    w_refs = refs[2 * n_br:3 * n_br]
    g_refs = refs[3 * n_br:4 * n_br]
    o_ref = refs[4 * n_br]

    def compute(group):
        acc = None
        for b in range(n_br):
            term = jax.nn.sigmoid(g_refs[b][...]) * _dot(x_refs[2 * b + group][...], w_refs[b][...])
            acc = term if acc is None else acc + term
        o_ref[...] = acc.astype(o_ref.dtype)

    i = pl.program_id(0)
    pl.when(i < nb_first)(lambda: compute(0))
    pl.when(i >= nb_first)(lambda: compute(1))


def _merge(ys, ws, gate, gate_col, tm=512, tn=1024):
    n_br = len(ys)
    n = ws[0].shape[1]
    m = ys[0][0].shape[0] + ys[0][1].shape[0]
    tm = min(tm, ys[0][0].shape[0], ys[0][1].shape[0])
    tn = _tile(n, tn)
    nb_first = ys[0][0].shape[0] // tm
    assert gate_col % tn == 0 and n % tn == 0 and all(y[0].shape[0] % tm == 0 and y[1].shape[0] % tm == 0 for y in ys)
    in_specs, args = [], []
    for y, w in zip(ys, ws):
        assert w.shape == (y[0].shape[1], n) and y[1].shape[1] == w.shape[0]
        in_specs += [pl.BlockSpec((tm, w.shape[0]), lambda i, j: (jnp.minimum(i, nb_first - 1), 0)),
                     pl.BlockSpec((tm, w.shape[0]), lambda i, j: (jnp.maximum(i - nb_first, 0), 0))]
        args += [y[0], y[1]]
    in_specs += [pl.BlockSpec((w.shape[0], tn), lambda i, j: (0, j)) for w in ws]
    args += list(ws)
    for b in range(n_br):
        goff = (gate_col + b * n) // tn
        in_specs.append(pl.BlockSpec((tm, tn), lambda i, j, goff=goff: (i, goff + j)))
        args.append(gate)
    return pl.pallas_call(
        functools.partial(_merge_kernel, n_br=n_br, nb_first=nb_first),
        grid=(m // tm, n // tn),
        in_specs=in_specs,
        out_specs=pl.BlockSpec((tm, tn), lambda i, j: (i, j)),
        out_shape=jax.ShapeDtypeStruct((m, n), BF16),
        compiler_params=_params("parallel", "parallel"),
        name="merge_out",
    )(*args)


def _conv_kernel(a_ref, g_ref, wdw_ref, bdw_ref, lng_ref, lnb_ref, o_ref, upad_ref, yc_ref, *, seq, rb, rc, width):
    c = a_ref.shape[1]
    r = pl.program_id(1)
    fill_rows = min(seq, 64)

    @pl.when(r == 0)
    def _():
        zeros = jnp.zeros((CONV_PAD, c), F32)
        upad_ref[0:CONV_PAD, :] = zeros
        upad_ref[CONV_PAD + seq:CONV_PAD + seq + CONV_PAD, :] = zeros

        def fill(i, carry):
            src = pl.ds(pl.multiple_of(i * fill_rows, 8), fill_rows)
            dst = pl.ds(pl.multiple_of(CONV_PAD + i * fill_rows, 8), fill_rows)
            upad_ref[dst, :] = a_ref[src, :] * jax.nn.sigmoid(g_ref[src, :])
            return carry

        lax.fori_loop(0, seq // fill_rows, fill, 0)

    lead = CONV_PAD - width // 2
    n_shift = 8
    n_al = (lead + width - 1) // n_shift + 1
    win = rc + n_shift * n_al

    def row_chunk(ci, carry):
        base = pl.multiple_of(r * rb + ci * rc, 8)
        for cj in range(c // LANES):
            lanes = slice(cj * LANES, (cj + 1) * LANES)
            w_all = upad_ref[pl.ds(base, win), lanes]
            acc = jnp.zeros((rc, LANES), F32)
            for b in range(n_shift):
                taps = [(a8, n_shift * a8 + b - lead) for a8 in range(n_al)]
                taps = [(a8, k) for a8, k in taps if 0 <= k < width]
                if not taps:
                    continue
                w_b = w_all[b:b + rc + n_shift * (n_al - 1)]
                part = None
                for a8, k in taps:
                    term = wdw_ref[k:k + 1, lanes] * w_b[n_shift * a8:n_shift * a8 + rc]
                    part = term if part is None else part + term
                acc = acc + part
            yc_ref[pl.ds(pl.multiple_of(ci * rc, 8), rc), lanes] = acc + bdw_ref[:, lanes]
        return carry

    lax.fori_loop(0, rb // rc, row_chunk, 0)
    y = _ln(yc_ref[...]) * lng_ref[...] + lnb_ref[...]
    o_ref[...] = _silu(y).astype(o_ref.dtype)


def _conv_branch(hmat, col_a, col_g, w_dw, b_dw, ln_g, ln_b, *, row_off, n_seq, seq):
    width, c = w_dw.shape
    rb = min(seq, 256)
    rc = min(rb, 64)
    assert width // 2 <= CONV_PAD and row_off % seq == 0 and col_a % c == 0 and col_g % c == 0
    s_off = row_off // seq
    in_specs = [
        pl.BlockSpec((seq, c), lambda s, r: (s_off + s, col_a // c)),
        pl.BlockSpec((seq, c), lambda s, r: (s_off + s, col_g // c)),
        pl.BlockSpec((width, c), lambda s, r: (0, 0)),
        pl.BlockSpec((1, c), lambda s, r: (0, 0)),
        pl.BlockSpec((1, c), lambda s, r: (0, 0)),
        pl.BlockSpec((1, c), lambda s, r: (0, 0)),
    ]
    args = [hmat, hmat, w_dw, b_dw.reshape(1, c), ln_g.reshape(1, c), ln_b.reshape(1, c)]
    return pl.pallas_call(
        functools.partial(_conv_kernel, seq=seq, rb=rb, rc=rc, width=width),
        grid=(n_seq, seq // rb),
        in_specs=in_specs,
        out_specs=pl.BlockSpec((rb, c), lambda s, r: (s * (seq // rb) + r, 0)),
        out_shape=jax.ShapeDtypeStruct((n_seq * seq, c), BF16),
        scratch_shapes=[pltpu.VMEM((seq + 2 * CONV_PAD, c), F32), pltpu.VMEM((rb, c), F32)],
        compiler_params=_params("parallel", "arbitrary"),
        name="conv_branch",
    )(*args)


def _cumsum3(tri, la):
    hi = la.astype(BF16)
    r1 = la - hi.astype(F32)
    mid = r1.astype(BF16)
    lo = (r1 - mid.astype(F32)).astype(BF16)
    return _dot(tri, hi) + _dot(tri, mid) + _dot(tri, lo)


def _log_sigmoid(x):
    return jnp.minimum(x, 0.0) - jnp.log(1.0 + jnp.exp(-jnp.abs(x)))


def _gla_kernel(*refs, seq, chunk, has_state):
    it = iter(refs)
    q_ref, k_ref, v_ref, r_ref, z_ref = (next(it) for _ in range(5))
    wgf_ref, bgf_ref, wgb_ref, bgb_ref, ng_ref = (next(it) for _ in range(5))
    if has_state:
        s0f_ref, s0b_ref = next(it), next(it)
        o_ref = next(it)
        sf_ref = sb_ref = None
    else:
        o_ref, sf_ref, sb_ref = next(it), next(it), next(it)
    laf_ref, lab_ref, oacc_ref, kvf_ref, kvb_ref, qdf_ref, qdb_ref, decf_ref, decb_ref = (next(it) for _ in range(9))

    dk = q_ref.shape[1]
    dv = v_ref.shape[1]
    n_chunks = seq // chunk
    z = z_ref[...].astype(BF16)
    laf_ref[...] = _log_sigmoid(_dot(z, wgf_ref[...].astype(BF16)) + bgf_ref[...]) / GLA_TAU
    lab_ref[...] = _log_sigmoid(_dot(z, wgb_ref[...].astype(BF16)) + bgb_ref[...]) / GLA_TAU

    row = lax.broadcasted_iota(jnp.int32, (chunk, chunk), 0)
    col = lax.broadcasted_iota(jnp.int32, (chunk, chunk), 1)
    lower = row >= col
    upper = row <= col
    tri_f = jnp.where(lower, 1.0, 0.0).astype(BF16)
    tri_b = jnp.where(upper, 1.0, 0.0).astype(BF16)
    q_scale = dk ** -0.5

    def intra(n, rows, la_ref, tri, mask, edge, qd_ref, kv_ref, dec_ref):
        b = _cumsum3(tri, la_ref[rows, :])
        b_edge = b[edge:edge + 1, :]
        q = q_ref[rows, :] * q_scale
        k = k_ref[rows, :]
        v = v_ref[rows, :].astype(BF16)
        qd = (q * jnp.exp(b)).astype(BF16)
        kd = (k * jnp.exp(-b)).astype(BF16)
        ke = (k * jnp.exp(b_edge - b)).astype(BF16)
        att = jnp.where(mask, _dot_nt(qd, kd), 0.0).astype(BF16)
        qd_ref[rows, :] = qd
        kv_ref[n] = _dot_tn(v, ke)
        dec_ref[n] = jnp.broadcast_to(jnp.exp(b_edge), dec_ref.shape[1:])
        return _dot(att, v)

    def phase1(n, carry):
        rows = pl.ds(pl.multiple_of(n * chunk, chunk), chunk)
        oacc_ref[rows, :] = (intra(n, rows, laf_ref, tri_f, lower, chunk - 1, qdf_ref, kvf_ref, decf_ref)
                             + intra(n, rows, lab_ref, tri_b, upper, 0, qdb_ref, kvb_ref, decb_ref))
        return carry

    lax.fori_loop(0, n_chunks, phase1, 0, unroll=2)

    def scan(order, kv_ref, dec_ref, s0):
        def body(j, s):
            n = order(j)
            kv = kv_ref[n]
            kv_ref[n] = s
            return s * dec_ref[n, 0:1, :] + kv

        return lax.fori_loop(0, n_chunks, body, s0)

    zero = jnp.zeros((dv, dk), F32)
    s_f = scan(lambda j: j, kvf_ref, decf_ref, s0f_ref[0, 0].T if has_state else zero)
    s_b = scan(lambda j: n_chunks - 1 - j, kvb_ref, decb_ref, s0b_ref[0, 0].T if has_state else zero)
    if not has_state:
        sf_ref[0, 0] = s_f.T
        sb_ref[0, 0] = s_b.T

    def phase3(n, carry):
        rows = pl.ds(pl.multiple_of(n * chunk, chunk), chunk)
        o = (oacc_ref[rows, :] + _dot_nt(qdf_ref[rows, :], kvf_ref[n].astype(BF16))
             + _dot_nt(qdb_ref[rows, :], kvb_ref[n].astype(BF16)))
        o = _rms(o) * ng_ref[...]
        o_ref[rows, :] = (o * _silu(r_ref[rows, :])).astype(o_ref.dtype)
        return carry

    lax.fori_loop(0, n_chunks, phase3, 0, unroll=2)


def _gla_branch(hmat, cols, wgf, bgf, wgb, bgb, norm_g, *, heads, dk, dv, row_off, n_seq, seq, state=None):
    col_q, col_k, col_v, col_r, col_z = cols
    s_off = row_off // seq
    has_state = state is not None
    in_specs = [
        pl.BlockSpec((seq, dk), lambda s, h: (s_off + s, col_q // dk + h)),
        pl.BlockSpec((seq, dk), lambda s, h: (s_off + s, col_k // dk + h)),
        pl.BlockSpec((seq, dv), lambda s, h: (s_off + s, col_v // dv + h)),
        pl.BlockSpec((seq, dv), lambda s, h: (s_off + s, col_r // dv + h)),
        pl.BlockSpec((seq, LANES), lambda s, h: (s_off + s, col_z // LANES)),
        pl.BlockSpec((LANES, dk), lambda s, h: (0, h)),
        pl.BlockSpec((1, dk), lambda s, h: (0, h)),
        pl.BlockSpec((LANES, dk), lambda s, h: (0, h)),
        pl.BlockSpec((1, dk), lambda s, h: (0, h)),
        pl.BlockSpec((1, dv), lambda s, h: (0, h)),
    ]
    args = [hmat] * 5 + [wgf, bgf.reshape(1, -1), wgb, bgb.reshape(1, -1), norm_g.reshape(1, -1)]
    o_spec = pl.BlockSpec((seq, dv), lambda s, h: (s, h))
    o_shape = jax.ShapeDtypeStruct((n_seq * seq, heads * dv), BF16)
    if has_state:
        st_spec = pl.BlockSpec((1, 1, dk, dv), lambda s, h: (s, h, 0, 0))
        in_specs += [st_spec, st_spec]
        args += [state[0], state[1]]
        out_specs, out_shape = o_spec, o_shape
    else:
        st_spec = pl.BlockSpec((1, 1, dk, dv), lambda s, h: (s, h, 0, 0))
        st_shape = jax.ShapeDtypeStruct((n_seq, heads, dk, dv), F32)
        out_specs, out_shape = [o_spec, st_spec, st_spec], [o_shape, st_shape, st_shape]
    chunk = min(GLA_CHUNK, seq)
    n_chunks = seq // chunk
    return pl.pallas_call(
        functools.partial(_gla_kernel, seq=seq, chunk=chunk, has_state=has_state),
        grid=(n_seq, heads),
        in_specs=in_specs,
        out_specs=out_specs,
        out_shape=out_shape,
        scratch_shapes=[pltpu.VMEM((seq, dk), F32), pltpu.VMEM((seq, dk), F32), pltpu.VMEM((seq, dv), F32),
                        pltpu.VMEM((n_chunks, dv, dk), F32), pltpu.VMEM((n_chunks, dv, dk), F32),
                        pltpu.VMEM((seq, dk), BF16), pltpu.VMEM((seq, dk), BF16),
                        pltpu.VMEM((n_chunks, 8, dk), F32), pltpu.VMEM((n_chunks, 8, dk), F32)],
        compiler_params=_params("parallel", "parallel"),
        name="gla_branch",
    )(*args)


def _mla_proj_kernel(cq_ref, ckv_ref, pe_ref, qg_ref, kvg_ref, wq_ref, wqr_ref, tq_ref, tk_ref,
                     q_ref, ckvn_ref, kpe_ref, *, hw):
    cqn = (_rms(cq_ref[...]) * qg_ref[...]).astype(BF16)
    a = _dot(cqn, wq_ref[...])
    ar = _dot(cqn, wqr_ref[...])
    cos = tq_ref[:, 0:hw]
    sin = tq_ref[:, hw:2 * hw]
    for h in range(a.shape[1] // hw):
        sl = slice(h * hw, (h + 1) * hw)
        q_ref[:, sl] = (a[:, sl] * cos + ar[:, sl] * sin).astype(q_ref.dtype)
    ckvn_ref[...] = _rms(ckv_ref[...]) * kvg_ref[...]
    v = pe_ref[...] * tk_ref[...]
    v = v + pltpu.roll(v, LANES // 2, axis=1)
    lane = lax.broadcasted_iota(jnp.int32, v.shape, 1)
    kpe_ref[...] = jnp.where(lane < LANES // 2, v, 0.0)


def _mla_proj(hmat, col_cq, col_ckv, col_pe, q_g, kv_g, wq, wqr, tq, tk, tm=512):
    m = hmat.shape[0]
    rq = q_g.shape[0]
    rkv = kv_g.shape[0]
    nq = wq.shape[1]
    hw = tq.shape[1] // 2
    tm = min(tm, m)
    return pl.pallas_call(
        functools.partial(_mla_proj_kernel, hw=hw),
        grid=(m // tm,),
        in_specs=[
            pl.BlockSpec((tm, rq), lambda i: (i, col_cq // rq)),
            pl.BlockSpec((tm, rkv), lambda i: (i, col_ckv // rkv)),
            pl.BlockSpec((tm, LANES), lambda i: (i, col_pe // LANES)),
            pl.BlockSpec((1, rq), lambda i: (0, 0)),
            pl.BlockSpec((1, rkv), lambda i: (0, 0)),
            pl.BlockSpec((rq, nq), lambda i: (0, 0)),
            pl.BlockSpec((rq, nq), lambda i: (0, 0)),
            pl.BlockSpec((tm, 2 * hw), lambda i: (i, 0)),
            pl.BlockSpec((tm, LANES), lambda i: (i, 0)),
        ],
        out_specs=[
            pl.BlockSpec((tm, nq), lambda i: (i, 0)),
            pl.BlockSpec((tm, rkv), lambda i: (i, 0)),
            pl.BlockSpec((tm, LANES), lambda i: (i, 0)),
        ],
        out_shape=[
            jax.ShapeDtypeStruct((m, nq), BF16),
            jax.ShapeDtypeStruct((m, rkv), F32),
            jax.ShapeDtypeStruct((m, LANES), F32),
        ],
        compiler_params=_params("parallel"),
        name="mla_proj",
    )(hmat, hmat, hmat, q_g.reshape(1, rq), kv_g.reshape(1, rkv), wq, wqr, tq, tk)


def _attn_kernel(*refs, heads, hw, dh, scale):
    q_ref, kv_ref, kpe_ref = refs[:3]
    o_ref, kcat_ref = refs[-2:]

    @pl.when(pl.program_id(1) == 0)
    def _():
        kp = kpe_ref[0].astype(BF16)
        for h in range(heads):
            kcat_ref[:, h * hw:h * hw + dh] = kv_ref[0, :, h * dh:(h + 1) * dh]
            kcat_ref[:, h * hw + dh:(h + 1) * hw] = kp

    v_off = heads * dh
    for h in range(heads):
        s = _dot_nt(q_ref[:, h * hw:(h + 1) * hw], kcat_ref[:, h * hw:(h + 1) * hw]) * scale
        p = jnp.exp(s - jnp.max(s, axis=-1, keepdims=True))
        l = jnp.sum(p, axis=-1, keepdims=True)
        o = _dot(p.astype(BF16), kv_ref[0, :, v_off + h * dh:v_off + (h + 1) * dh]) / l
        o_ref[:, h * dh:(h + 1) * dh] = o.astype(o_ref.dtype)


def _attention(q_all, kv, kpe, *, heads, dh, scale, row_off, n_seq, seq, tq=256):
    s_len = kv.shape[1]
    hw = q_all.shape[1] // heads
    tq = min(tq, seq)
    q_off = row_off // tq
    in_specs = [
        pl.BlockSpec((tq, heads * hw), lambda b, i: (q_off + b * (seq // tq) + i, 0)),
        pl.BlockSpec((1, s_len, 2 * heads * dh), lambda b, i: (b, 0, 0)),
        pl.BlockSpec((1, s_len, LANES), lambda b, i: (b, 0, 0)),
    ]
    args = [q_all, kv, kpe]
    return pl.pallas_call(
        functools.partial(_attn_kernel, heads=heads, hw=hw, dh=dh, scale=scale),
        grid=(n_seq, seq // tq),
        in_specs=in_specs,
        out_specs=pl.BlockSpec((tq, heads * dh), lambda b, i: (b * (seq // tq) + i, 0)),
        out_shape=jax.ShapeDtypeStruct((n_seq * seq, heads * dh), BF16),
        scratch_shapes=[pltpu.VMEM((s_len, heads * hw), BF16)],
        compiler_params=_params("parallel", "arbitrary"),
        name="mla_attention",
    )(*args)


def _split3(x):
    hi = x.astype(BF16)
    lo = (x - hi.astype(F32)).astype(BF16)
    return hi, lo


def _post_attn_kernel(m_ref, x_ref, w_ref, mod_ref, g1_ref, b1_ref, rw_ref, rb_ref,
                      x1_ref, h2_ref, ri_ref, rg_ref, cnt_ref, run_ref, tri_ref, *, alpha, n_exp):
    tm = x_ref.shape[0]
    step = pl.program_id(0)

    @pl.when(step == 0)
    def _():
        run_ref[...] = jnp.zeros_like(run_ref)
        row = lax.broadcasted_iota(jnp.int32, (tm, tm), 0)
        col = lax.broadcasted_iota(jnp.int32, (tm, tm), 1)
        tri_ref[...] = jnp.where(row > col, 1.0, 0.0).astype(BF16)

    y = _dot(m_ref[...], w_ref[...])
    x1 = _ln(alpha * x_ref[...] + mod_ref[0, 2:3, :] * y) * g1_ref[...] + b1_ref[...]
    x1_ref[...] = x1
    h2 = _ln(x1) * (1.0 + mod_ref[0, 4:5, :]) + mod_ref[0, 3:4, :]
    half = h2_ref.shape[1]
    hi = pltpu.bitcast(h2[:, :half].astype(BF16).astype(F32), jnp.uint32)
    lo = pltpu.bitcast(h2[:, half:].astype(BF16).astype(F32), jnp.uint32)
    h2_ref[...] = hi | (lo >> 16)

    h_hi, h_lo = _split3(h2)
    w_hi, w_lo = _split3(rw_ref[...])
    logits = _dot(h_hi, w_hi) + _dot(h_lo, w_hi) + _dot(h_hi, w_lo) + rb_ref[...]
    lane = lax.broadcasted_iota(jnp.int32, logits.shape, 1).astype(F32)
    neg = jnp.float32(-jnp.inf)
    logits = jnp.where(lane < n_exp, logits, neg)

    counts = jnp.zeros(logits.shape, F32)
    vals, idxs = [], []
    for _ in range(TOP_K):
        mx = jnp.max(logits, axis=-1, keepdims=True)
        idx = jnp.min(jnp.where(logits == mx, lane, float(LANES)), axis=-1, keepdims=True)
        hit = lane == idx
        counts = counts + jnp.where(hit, 1.0, 0.0)
        logits = jnp.where(hit, neg, logits)
        vals.append(mx)
        idxs.append(idx)

    es = [jnp.exp(v - vals[0]) for v in vals]
    denom = es[0]
    for e in es[1:]:
        denom = denom + e
    before = _dot(tri_ref[...], counts.astype(BF16)) + run_ref[...]
    out_i = jnp.zeros(logits.shape, jnp.int32)
    out_g = jnp.zeros(logits.shape, F32)
    for k in range(TOP_K):
        rank = jnp.sum(jnp.where(lane == idxs[k], before, 0.0), axis=-1, keepdims=True)
        out_i = jnp.where(lane == k, idxs[k].astype(jnp.int32), out_i)
        out_i = jnp.where(lane == TOP_K + k, rank.astype(jnp.int32), out_i)
        out_g = jnp.where(lane == k, es[k] / denom, out_g)
    ri_ref[...] = out_i[:, 0:2 * TOP_K]
    rg_ref[...] = out_g[:, 0:TOP_K]
    run_ref[...] = run_ref[...] + jnp.sum(counts, axis=0, keepdims=True)
    cnt_ref[...] = run_ref[...]


def _post_attn(merged, x, w_out, mod, ln_g, ln_b, router_w, router_b, n_prompt, t_sample, alpha, n_exp, tm=256):
    m, d = x.shape
    tm = min(tm, t_sample, n_prompt)
    row_spec = pl.BlockSpec((tm, d), lambda i: (i, 0))
    vec_spec = pl.BlockSpec((1, d), lambda i: (0, 0))
    return pl.pallas_call(
        functools.partial(_post_attn_kernel, alpha=alpha, n_exp=n_exp),
        grid=(m // tm,),
        in_specs=[
            row_spec, row_spec,
            pl.BlockSpec((d, d), lambda i: (0, 0)),
            pl.BlockSpec((1, 6, d), lambda i: (_mod_row(i, tm, n_prompt, t_sample), 0, 0)),
            vec_spec, vec_spec,
            pl.BlockSpec((d, LANES), lambda i: (0, 0)),
            pl.BlockSpec((1, LANES), lambda i: (0, 0)),
        ],
        out_specs=[
            row_spec,
            pl.BlockSpec((tm, d // 2), lambda i: (i, 0)),
            pl.BlockSpec((tm, 2 * TOP_K), lambda i: (i, 0)),
            pl.BlockSpec((tm, TOP_K), lambda i: (i, 0)),
            pl.BlockSpec((1, LANES), lambda i: (0, 0)),
        ],
        out_shape=[
            jax.ShapeDtypeStruct((m, d), F32),
            jax.ShapeDtypeStruct((m, d // 2), jnp.uint32),
            jax.ShapeDtypeStruct((m, 2 * TOP_K), jnp.int32),
            jax.ShapeDtypeStruct((m, TOP_K), F32),
            jax.ShapeDtypeStruct((1, LANES), F32),
        ],
        scratch_shapes=[pltpu.VMEM((1, LANES), F32), pltpu.VMEM((tm, tm), BF16)],
        compiler_params=_params("arbitrary"),
        name="post_attn_router",
    )(merged, x, w_out, mod, ln_g.reshape(1, d), ln_b.reshape(1, d), router_w, router_b)


def _wait_slot(buf_ref, sem, slot):
    pltpu.make_async_copy(buf_ref.at[slot], buf_ref.at[slot], sem.at[slot]).wait()


def _moe_dispatch_kernel(idx_ref, h_hbm, init_hbm, o_hbm, sem):
    del init_hbm
    n_copy = idx_ref.shape[2]
    blk = pl.program_id(0)
    slot = blk % 2
    tok0 = blk * (n_copy // TOP_K)

    def wait_all(s):
        pltpu.make_async_copy(h_hbm.at[pl.ds(0, n_copy), :], o_hbm.at[pl.ds(0, n_copy), :], sem.at[s]).wait()

    def body(r, carry):
        for k in range(TOP_K):
            pltpu.make_async_copy(h_hbm.at[pl.ds(tok0 + r, 1), :], o_hbm.at[pl.ds(idx_ref[0, 0, r * TOP_K + k], 1), :],
                                  sem.at[slot]).start()
        return carry

    lax.fori_loop(0, n_copy // TOP_K, body, 0, unroll=2)

    @pl.when(blk > 0)
    def _():
        wait_all(1 - slot)

    @pl.when(blk == pl.num_programs(0) - 1)
    def _():
        wait_all(slot)


def _moe_dispatch(h2p, dest, cap, tc=128):
    m, half = h2p.shape
    tc = min(tc, m)
    nb = m // tc
    return pl.pallas_call(
        _moe_dispatch_kernel,
        grid=(nb,),
        in_specs=[
            pl.BlockSpec((1, 1, tc * TOP_K), lambda i: (i, 0, 0), memory_space=pltpu.SMEM),
            pl.BlockSpec(memory_space=pl.ANY),
            pl.BlockSpec(memory_space=pl.ANY),
        ],
        out_specs=pl.BlockSpec(memory_space=pl.ANY),
        out_shape=jax.ShapeDtypeStruct((cap, half), jnp.uint32),
        scratch_shapes=[pltpu.SemaphoreType.DMA((2,))],
        input_output_aliases={2: 0},
        compiler_params=_params("arbitrary"),
        name="moe_dispatch",
    )(dest.reshape(nb, 1, tc * TOP_K), h2p, jnp.zeros((cap, half), jnp.uint32))


def _moe_expert_kernel(be_ref, bv_ref, x_ref, wg_ref, bg_ref, wu_ref, bu_ref, wd_ref, bd_ref,
                       o_ref, xb_ref):
    i = pl.program_id(0)
    f = pl.program_id(1)
    valid = bv_ref[i]
    half = x_ref.shape[1]
    del be_ref

    @pl.when(f == 0)
    def _():
        o_ref[...] = jnp.broadcast_to(bd_ref[0, 0], o_ref.shape)
        u = x_ref[...]
        xb_ref[:, :half] = pltpu.bitcast(u & jnp.uint32(0xFFFF0000), F32).astype(BF16)
        xb_ref[:, half:] = pltpu.bitcast(u << 16, F32).astype(BF16)

    for sb in range(x_ref.shape[0] // MOE_SUB):
        rows = slice(sb * MOE_SUB, (sb + 1) * MOE_SUB)

        @pl.when(sb * MOE_SUB < valid)
        def _():
            x = xb_ref[rows, :]
            gt = jnp.minimum(_dot(x, wg_ref[0, 0].astype(BF16)) + bg_ref[0, 0], SWIGLU_LIMIT)
            up = jnp.clip(_dot(x, wu_ref[0, 0].astype(BF16)) + bu_ref[0, 0], -SWIGLU_LIMIT, SWIGLU_LIMIT)
            act = gt * jax.nn.sigmoid(SWIGLU_ALPHA * gt) * (up + 1.0)
            o_ref[rows, :] = o_ref[rows, :] + _dot(act.astype(BF16), wd_ref[0, 0].astype(BF16))


def _moe_experts(xs, blk_e, blk_valid, layer, w_gate, b_gate, w_up, b_up, w_down, b_down, tf=256):
    cap, half = xs.shape
    n_layers, n_exp, d, ff = w_gate.shape
    assert d == 2 * half
    tf = _tile(ff, tf)
    nf = ff // tf
    nblk = cap // MOE_ROWS

    def f_idx(i, f, bv):
        return jnp.where(bv[i] > 0, f, nf - 1)

    return pl.pallas_call(
        _moe_expert_kernel,
        grid_spec=pltpu.PrefetchScalarGridSpec(
            num_scalar_prefetch=2,
            grid=(nblk, nf),
            in_specs=[
                pl.BlockSpec((MOE_ROWS, half), lambda i, f, be, bv: (i, 0)),
                pl.BlockSpec((1, 1, d, tf), lambda i, f, be, bv: (layer, be[i], 0, f_idx(i, f, bv))),
                pl.BlockSpec((1, 1, 1, tf), lambda i, f, be, bv: (layer, be[i], 0, f_idx(i, f, bv))),
                pl.BlockSpec((1, 1, d, tf), lambda i, f, be, bv: (layer, be[i], 0, f_idx(i, f, bv))),
                pl.BlockSpec((1, 1, 1, tf), lambda i, f, be, bv: (layer, be[i], 0, f_idx(i, f, bv))),
                pl.BlockSpec((1, 1, tf, d), lambda i, f, be, bv: (layer, be[i], f_idx(i, f, bv), 0)),
                pl.BlockSpec((1, 1, 1, d), lambda i, f, be, bv: (layer, be[i], 0, 0)),
            ],
            out_specs=pl.BlockSpec((MOE_ROWS, d), lambda i, f, be, bv: (i, 0)),
            scratch_shapes=[pltpu.VMEM((MOE_ROWS, d), BF16)],
        ),
        out_shape=jax.ShapeDtypeStruct((cap, d), F32),
        compiler_params=_params("arbitrary", "arbitrary"),
        name="moe_experts",
    )(blk_e, blk_valid, xs, w_gate, b_gate.reshape(n_layers, n_exp, 1, ff), w_up, b_up.reshape(n_layers, n_exp, 1, ff),
      w_down, b_down.reshape(n_layers, n_exp, 1, d))


def _moe_combine_kernel(idx_ref, nxt_ref, yb_hbm, g_ref, x_ref, mod_ref, lg_ref, lb_ref, o_ref, buf_ref, sem, *,
                        alpha):
    tc = x_ref.shape[0]
    blk = pl.program_id(0)
    slot = blk % 2

    def issue(ids_ref, s):
        def body(r, carry):
            for k in range(TOP_K):
                pltpu.make_async_copy(yb_hbm.at[pl.ds(ids_ref[0, 0, r * TOP_K + k], 1), :],
                                      buf_ref.at[s, k, pl.ds(r, 1), :], sem.at[s]).start()
            return carry

        lax.fori_loop(0, tc, body, 0, unroll=2)

    @pl.when(blk == 0)
    def _():
        issue(idx_ref, 0)

    @pl.when(blk + 1 < pl.num_programs(0))
    def _():
        issue(nxt_ref, 1 - slot)

    _wait_slot(buf_ref, sem, slot)
    g = g_ref[...]
    y = g[:, 0:1] * buf_ref[slot, 0]
    for k in range(1, TOP_K):
        y = y + g[:, k:k + 1] * buf_ref[slot, k]
    x2 = _ln(alpha * x_ref[...] + mod_ref[0, 5:6, :] * y) * lg_ref[...] + lb_ref[...]
    o_ref[...] = x2


def _moe_combine(yb, dest, gates, x1, mod, ln_g, ln_b, n_prompt, t_sample, alpha, tc=128):
    m, d = x1.shape
    tc = min(tc, t_sample, n_prompt)
    nb = m // tc
    ids = dest.reshape(nb, 1, tc * TOP_K)
    return pl.pallas_call(
        functools.partial(_moe_combine_kernel, alpha=alpha),
        grid=(nb,),
        in_specs=[
            pl.BlockSpec((1, 1, tc * TOP_K), lambda i: (i, 0, 0), memory_space=pltpu.SMEM),
            pl.BlockSpec((1, 1, tc * TOP_K), lambda i: (jnp.minimum(i + 1, nb - 1), 0, 0), memory_space=pltpu.SMEM),
            pl.BlockSpec(memory_space=pl.ANY),
            pl.BlockSpec((tc, TOP_K), lambda i: (i, 0)),
            pl.BlockSpec((tc, d), lambda i: (i, 0)),
            pl.BlockSpec((1, 6, d), lambda i: (_mod_row(i, tc, n_prompt, t_sample), 0, 0)),
            pl.BlockSpec((1, d), lambda i: (0, 0)),
            pl.BlockSpec((1, d), lambda i: (0, 0)),
        ],
        out_specs=pl.BlockSpec((tc, d), lambda i: (i, 0)),
        out_shape=jax.ShapeDtypeStruct((m, d), F32),
        scratch_shapes=[pltpu.VMEM((2, TOP_K, tc, d), F32), pltpu.SemaphoreType.DMA((2,))],
        compiler_params=_params("arbitrary"),
        name="moe_combine",
    )(ids, ids, yb, gates, x1, mod, ln_g.reshape(1, d), ln_b.reshape(1, d))


def _moe_plan(route_i, counts, n_exp):
    m = route_i.shape[0]
    top_e = route_i[:, :TOP_K]
    rank = route_i[:, TOP_K:]
    counts = counts[0, :n_exp].astype(jnp.int32)
    padded = (counts + MOE_ROWS - 1) // MOE_ROWS * MOE_ROWS
    pend = jnp.cumsum(padded)
    pstart = pend - padded
    dest = pstart[top_e] + rank
    nblk = (m * TOP_K + n_exp * (MOE_ROWS - 1)) // MOE_ROWS
    blk_row = jnp.arange(nblk, dtype=jnp.int32) * MOE_ROWS
    n_used = pend[-1] // MOE_ROWS
    blk_e = jnp.minimum(jnp.searchsorted(pend, blk_row, side="right"), n_exp - 1).astype(jnp.int32)
    blk_valid = jnp.clip(counts[blk_e] - (blk_row - pstart[blk_e]), 0, MOE_ROWS).astype(jnp.int32)
    used = jnp.arange(nblk, dtype=jnp.int32) < n_used
    blk_valid = jnp.where(used, blk_valid, 0)
    blk_e = jnp.where(used, blk_e, blk_e[jnp.maximum(n_used - 1, 0)])
    return dest.astype(jnp.int32), blk_e, blk_valid, nblk * MOE_ROWS


def _rot_half_cols(w):
    w1, w2, w3, w4 = jnp.split(w, 4, axis=-1)
    return jnp.concatenate([-w2, w1, -w4, w3], axis=-1)


def _rope_tables(n_prompt, n_batch, t_sample, rope, dh):
    rows = t_sample // GRID_W
    row = jnp.repeat(jnp.arange(rows), GRID_W).astype(F32)
    col = jnp.tile(jnp.arange(GRID_W), rows).astype(F32)
    n_freq = rope // 4
    inv = ROPE_BASE ** (-jnp.arange(n_freq, dtype=F32) / n_freq)
    ar = row[:, None] * inv
    ac = col[:, None] * inv
    ang = jnp.concatenate([ar, ar, ac, ac], axis=-1)
    cos = jnp.concatenate([jnp.ones((n_prompt, rope), F32), jnp.tile(jnp.cos(ang), (n_batch, 1))], axis=0)
    sin = jnp.concatenate([jnp.zeros((n_prompt, rope), F32), jnp.tile(jnp.sin(ang), (n_batch, 1))], axis=0)
    n = cos.shape[0]
    hw = 2 * dh
    pad = jnp.zeros((n, hw - dh - rope), F32)
    tq = jnp.concatenate([jnp.ones((n, dh), F32), cos, pad, jnp.zeros((n, dh), F32), sin, pad], axis=-1)
    tk = jnp.concatenate([cos, sin], axis=-1)
    return tq, tk


def kernel(x_prompt, x_sample, c, cache_ckv, cache_kpe, state_gla_fwd, state_gla_bwd, c_ctx, w_mod, b_mod, w_in,
           conv_w_dw, conv_b_dw, conv_ln_g, conv_ln_b, conv_w_out, gla_w_gate_fwd, gla_b_gate_fwd,
           gla_w_gate_bwd, gla_b_gate_bwd, gla_norm_g, gla_w_out, mla_q_norm_g, mla_w_uq, mla_kv_norm_g,
           mla_w_uk, mla_w_uv, mla_w_out, w_out, ln1_g, ln1_b, router_w, router_b, exp_w_gate, exp_b_gate,
           exp_w_up, exp_b_up, exp_w_down, exp_b_down, ln2_g, ln2_b):
    bp, tp, d = x_prompt.shape
    bs, ts, _ = x_sample.shape
    depth = w_in.shape[0]
    n_prompt, n_sample = bp * tp, bs * ts
    n_tok = n_prompt + n_sample
    conv_dim = conv_w_dw.shape[2]
    gla_heads, gla_dk, gla_dv = state_gla_fwd.shape[2:]
    gla_qk, gla_v = gla_heads * gla_dk, gla_heads * gla_dv
    gate_rank = gla_w_gate_fwd.shape[1]
    q_rank = mla_w_uq.shape[1]
    kv_rank = mla_w_uk.shape[1]
    rope = cache_kpe.shape[3]
    dh = mla_w_uk.shape[2] // MLA_HEADS
    n_exp = router_w.shape[2]
    past = cache_ckv.shape[2]
    alpha = (2 * depth) ** 0.25
    assert 2 * dh == 2 * LANES and rope == LANES // 2 and 2 * gate_rank <= LANES

    sizes = (conv_dim, conv_dim, gla_qk, gla_qk, gla_v, gla_v, gate_rank, gate_rank, q_rank, kv_rank, rope, d, d, d)
    offs = [0]
    for s in sizes:
        offs.append(offs[-1] + s)
    o_zf, o_cq, o_kpe, o_ma = offs[6], offs[8], offs[10], offs[11]
    c_ca, c_cg, c_gq, c_gk, c_gv, c_gr = offs[0], offs[1], offs[2], offs[3], offs[4], offs[5]
    c_cq = o_zf
    c_ckv = c_cq + q_rank
    c_ma = c_ckv + kv_rank
    c_pe = c_ma + 3 * d
    c_z = c_pe + LANES
    n_in = c_z + LANES

    cc = jnp.concatenate([c_ctx[None, :], c], axis=0)
    mp = (cc.shape[0] + 7) // 8 * 8
    cc = jnp.pad(cc, ((0, mp - cc.shape[0]), (0, 0)))
    mod_all = _modulation(cc, w_mod, b_mod).reshape(depth, mp, 6, d)

    tq_tab, tk_tab = _rope_tables(n_prompt, bs, ts, rope, dh)
    x = jnp.concatenate([x_prompt.reshape(n_prompt, d), x_sample.reshape(n_sample, d)], axis=0)
    scale = (dh + rope) ** -0.5

    ckv_l, kpe_l, sf_l, sb_l = [], [], [], []
    for l in range(depth):
        mod = mod_all[l]
        wl = w_in[l]
        kpe_cols = wl[:, o_kpe:o_kpe + rope]
        w_in_r = jnp.concatenate(
            [wl[:, :o_zf], wl[:, o_cq:o_kpe], wl[:, o_ma:], kpe_cols, _rot_half_cols(kpe_cols),
             wl[:, o_zf:o_cq], jnp.zeros((d, LANES - 2 * gate_rank), F32)], axis=1).astype(BF16)
        assert w_in_r.shape[1] == n_in
        zpad = jnp.zeros((LANES - 2 * gate_rank, gla_qk), F32)
        wgf = jnp.concatenate([gla_w_gate_fwd[l], jnp.zeros((gate_rank, gla_qk), F32), zpad], axis=0)
        wgb = jnp.concatenate([jnp.zeros((gate_rank, gla_qk), F32), gla_w_gate_bwd[l], zpad], axis=0)
        wq3 = mla_w_uq[l].reshape(q_rank, MLA_HEADS, dh + rope)
        zq = jnp.zeros((q_rank, MLA_HEADS, dh - rope), F32)
        wq = jnp.concatenate([wq3, zq], axis=-1).reshape(q_rank, -1).astype(BF16)
        wqr = jnp.concatenate([jnp.zeros((q_rank, MLA_HEADS, dh), F32), _rot_half_cols(wq3[..., dh:]), zq],
                              axis=-1).reshape(q_rank, -1).astype(BF16)
        wkv = jnp.concatenate([mla_w_uk[l], mla_w_uv[l]], axis=1).astype(BF16)
        rw = jnp.pad(router_w[l], ((0, 0), (0, LANES - n_exp)))
        rb = jnp.pad(router_b[l], (0, LANES - n_exp)).reshape(1, LANES)

        hmat = _gemm_in(x, mod, w_in_r, n_prompt, ts)

        conv_args = (hmat, c_ca, c_cg, conv_w_dw[l], conv_b_dw[l], conv_ln_g[l], conv_ln_b[l])
        ya = (_conv_branch(*conv_args, row_off=0, n_seq=bp, seq=tp),
              _conv_branch(*conv_args, row_off=n_prompt, n_seq=bs, seq=ts))

        gla_args = (hmat, (c_gq, c_gk, c_gv, c_gr, c_z), wgf, gla_b_gate_fwd[l], wgb, gla_b_gate_bwd[l],
                    gla_norm_g[l])
        gla_kw = dict(heads=gla_heads, dk=gla_dk, dv=gla_dv)
        og_p, s_f, s_b = _gla_branch(*gla_args, row_off=0, n_seq=bp, seq=tp, **gla_kw)
        og_s = _gla_branch(*gla_args, row_off=n_prompt, n_seq=bs, seq=ts, **gla_kw,
                           state=(state_gla_fwd[:, l], state_gla_bwd[:, l]))

        q_all, ckv, kpe = _mla_proj(hmat, c_cq, c_ckv, c_pe, mla_q_norm_g[l], mla_kv_norm_g[l], wq, wqr,
                                    tq_tab, tk_tab)
        kv_p = _gemm(ckv, wkv, rows=n_prompt, name="mla_kv").reshape(bp, tp, -1)
        keys_s = jnp.concatenate([cache_ckv[:, l], ckv[n_prompt:].reshape(bs, ts, kv_rank)], axis=1)
        kv_s = _gemm(keys_s.reshape(bs * (past + ts), kv_rank), wkv, name="mla_kv")
        kv_s = kv_s.reshape(bs, past + ts, -1)
        kpe_p = kpe[:n_prompt].reshape(bp, tp, LANES)
        kpe_s = jnp.concatenate([jnp.pad(cache_kpe[:, l], ((0, 0), (0, 0), (0, LANES - rope))),
                                 kpe[n_prompt:].reshape(bs, ts, LANES)], axis=1)
        att_kw = dict(heads=MLA_HEADS, dh=dh, scale=scale)
        att = (_attention(q_all, kv_p, kpe_p, row_off=0, n_seq=bp, seq=tp, **att_kw),
               _attention(q_all, kv_s, kpe_s, row_off=n_prompt, n_seq=bs, seq=ts, **att_kw))
        merged = _merge([ya, (og_p, og_s), att],
                        [conv_w_out[l].astype(BF16), gla_w_out[l].astype(BF16), mla_w_out[l].astype(BF16)],
                        hmat, c_ma)

        x1, h2, route_i, route_g, counts = _post_attn(merged, x, w_out[l].astype(BF16), mod, ln1_g[l], ln1_b[l],
                                                      rw, rb, n_prompt, ts, alpha, n_exp)

        dest, blk_e, blk_valid, cap = _moe_plan(route_i, counts, n_exp)
        xs = _moe_dispatch(h2, dest, cap)
        yb = _moe_experts(xs, blk_e, blk_valid, l, exp_w_gate, exp_b_gate, exp_w_up, exp_b_up, exp_w_down,
                          exp_b_down)
        x = _moe_combine(yb, dest, route_g, x1, mod, ln2_g[l], ln2_b[l], n_prompt, ts, alpha)

        ckv_l.append(ckv[:n_prompt].reshape(bp, tp, kv_rank))
        kpe_l.append(kpe_p[..., :rope])
        sf_l.append(s_f)
        sb_l.append(s_b)

    y_prompt = x[:n_prompt].reshape(bp, tp, d)
    y_sample = x[n_prompt:].reshape(bs, ts, d)
    return (y_prompt, y_sample, jnp.stack(ckv_l, axis=1), jnp.stack(kpe_l, axis=1),
            jnp.stack(sf_l, axis=1), jnp.stack(sb_l, axis=1))
```

```python
import functools

import jax
import jax.numpy as jnp
from jax import lax
from jax.experimental import pallas as pl
from jax.experimental.pallas import tpu as pltpu

F32 = jnp.float32
BF16 = jnp.bfloat16

LANES = 128
VMEM_LIMIT_BYTES = 56 * 1024 * 1024

GRID_W = 64
GLA_TAU = 16.0
GLA_CHUNK = 64
MLA_HEADS = 8
ROPE_BASE = 10000.0
TOP_K = 4
SWIGLU_LIMIT = 7.0
SWIGLU_ALPHA = 1.702
LN_EPS = 1e-5
RMS_EPS = 1e-6

CONV_PAD = 16
MOE_ROWS = 1024
MOE_SUB = 512


def _params(*sem):
    return pltpu.CompilerParams(dimension_semantics=sem, vmem_limit_bytes=VMEM_LIMIT_BYTES)


def _tile(n, pref):
    if n <= pref:
        return n
    t = (pref // LANES) * LANES
    while t > LANES and n % t:
        t -= LANES
    assert n % t == 0, (n, pref)
    return t


def _ln(x):
    mu = jnp.mean(x, axis=-1, keepdims=True)
    xc = x - mu
    return xc * lax.rsqrt(jnp.mean(xc * xc, axis=-1, keepdims=True) + LN_EPS)


def _rms(x):
    return x * lax.rsqrt(jnp.mean(x * x, axis=-1, keepdims=True) + RMS_EPS)


def _silu(x):
    return x * jax.nn.sigmoid(x)


def _dot(a, b):
    return jnp.dot(a, b, preferred_element_type=F32)


def _dot_nt(a, b):
    return lax.dot_general(a, b, (((1,), (1,)), ((), ())), preferred_element_type=F32)


def _dot_tn(a, b):
    return lax.dot_general(a, b, (((0,), (0,)), ((), ())), preferred_element_type=F32)


def _mod_kernel(c_ref, w_ref, b_ref, o_ref):
    s = _silu(c_ref[...])
    o_ref[0] = _dot(s.astype(BF16), w_ref[0].astype(BF16)) + b_ref[0]


def _modulation(cc, w_mod, b_mod):
    n_layers, d, d6 = w_mod.shape
    mp = cc.shape[0]
    tn = _tile(d6, 1024)
    return pl.pallas_call(
        _mod_kernel,
        grid=(n_layers, d6 // tn),
        in_specs=[
            pl.BlockSpec((mp, d), lambda l, j: (0, 0)),
            pl.BlockSpec((1, d, tn), lambda l, j: (l, 0, j)),
            pl.BlockSpec((1, 1, tn), lambda l, j: (l, 0, j)),
        ],
        out_specs=pl.BlockSpec((1, mp, tn), lambda l, j: (l, 0, j)),
        out_shape=jax.ShapeDtypeStruct((n_layers, mp, d6), F32),
        compiler_params=_params("parallel", "parallel"),
        name="modulation",
    )(cc, w_mod, b_mod.reshape(n_layers, 1, d6))


def _mod_row(i, tm, n_prompt, t_sample):
    r = i * tm
    return jnp.where(r < n_prompt, 0, 1 + (r - n_prompt) // t_sample)


def _gemm_in_kernel(x_ref, mod_ref, w_ref, o_ref, h_ref):
    @pl.when(pl.program_id(1) == 0)
    def _():
        y = _ln(x_ref[...])
        h_ref[...] = (y * (1.0 + mod_ref[0, 1:2, :]) + mod_ref[0, 0:1, :]).astype(BF16)

    o_ref[...] = _dot(h_ref[...], w_ref[...])


def _gemm_in(x, mod, w, n_prompt, t_sample, tm=1024, tn=896):
    m, d = x.shape
    n = w.shape[1]
    tm = min(tm, t_sample, n_prompt)
    tn = _tile(n, tn)
    assert m % tm == 0 and n_prompt % tm == 0 and t_sample % tm == 0
    return pl.pallas_call(
        _gemm_in_kernel,
        grid=(m // tm, n // tn),
        in_specs=[
            pl.BlockSpec((tm, d), lambda i, j: (i, 0)),
            pl.BlockSpec((1, 6, d), lambda i, j: (_mod_row(i, tm, n_prompt, t_sample), 0, 0)),
            pl.BlockSpec((d, tn), lambda i, j: (0, j)),
        ],
        out_specs=pl.BlockSpec((tm, tn), lambda i, j: (i, j)),
        out_shape=jax.ShapeDtypeStruct((m, n), F32),
        scratch_shapes=[pltpu.VMEM((tm, d), BF16)],
        compiler_params=_params("parallel", "arbitrary"),
        name="gemm_in",
    )(x, mod, w)


def _gemm_kernel(x_ref, w_ref, o_ref):
    o_ref[...] = _dot(x_ref[...].astype(BF16), w_ref[...]).astype(o_ref.dtype)


def _gemm(x, w, *, rows=None, tm=512, tn=1024, name="gemm"):
    k, n = w.shape
    m = x.shape[0] if rows is None else rows
    tm = min(tm, m)
    tn = _tile(n, tn)
    assert m % tm == 0 and x.shape[1] == k
    return pl.pallas_call(
        _gemm_kernel,
        grid=(m // tm, n // tn),
        in_specs=[pl.BlockSpec((tm, k), lambda i, j: (i, 0)), pl.BlockSpec((k, tn), lambda i, j: (0, j))],
        out_specs=pl.BlockSpec((tm, tn), lambda i, j: (i, j)),
        out_shape=jax.ShapeDtypeStruct((m, n), BF16),
        compiler_params=_params("parallel", "parallel"),
        name=name,
    )(x, w)


def _merge_kernel(*refs, n_br, nb_first):
    x_refs = refs[:2 * n_br]
    w_refs = refs[2 * n_br:3 * n_br]
    g_refs = refs[3 * n_br:4 * n_br]
    o_ref = refs[4 * n_br]

    def compute(group):
        acc = None
        for b in range(n_br):
            term = jax.nn.sigmoid(g_refs[b][...]) * _dot(x_refs[2 * b + group][...], w_refs[b][...])
            acc = term if acc is None else acc + term
        o_ref[...] = acc.astype(o_ref.dtype)

    i = pl.program_id(0)
    pl.when(i < nb_first)(lambda: compute(0))
    pl.when(i >= nb_first)(lambda: compute(1))


def _merge(ys, ws, gate, gate_col, tm=512, tn=1024):
    n_br = len(ys)
    n = ws[0].shape[1]
    m = ys[0][0].shape[0] + ys[0][1].shape[0]
    tm = min(tm, ys[0][0].shape[0], ys[0][1].shape[0])
    tn = _tile(n, tn)
    nb_first = ys[0][0].shape[0] // tm
    assert gate_col % tn == 0 and n % tn == 0 and all(y[0].shape[0] % tm == 0 and y[1].shape[0] % tm == 0 for y in ys)
    in_specs, args = [], []
    for y, w in zip(ys, ws):
        assert w.shape == (y[0].shape[1], n) and y[1].shape[1] == w.shape[0]
        in_specs += [pl.BlockSpec((tm, w.shape[0]), lambda i, j: (jnp.minimum(i, nb_first - 1), 0)),
                     pl.BlockSpec((tm, w.shape[0]), lambda i, j: (jnp.maximum(i - nb_first, 0), 0))]
        args += [y[0], y[1]]
    in_specs += [pl.BlockSpec((w.shape[0], tn), lambda i, j: (0, j)) for w in ws]
    args += list(ws)
    for b in range(n_br):
        goff = (gate_col + b * n) // tn
        in_specs.append(pl.BlockSpec((tm, tn), lambda i, j, goff=goff: (i, goff + j)))
        args.append(gate)
    return pl.pallas_call(
        functools.partial(_merge_kernel, n_br=n_br, nb_first=nb_first),
        grid=(m // tm, n // tn),
        in_specs=in_specs,
        out_specs=pl.BlockSpec((tm, tn), lambda i, j: (i, j)),
        out_shape=jax.ShapeDtypeStruct((m, n), BF16),
        compiler_params=_params("parallel", "parallel"),
        name="merge_out",
    )(*args)


def _conv_kernel(a_ref, g_ref, wdw_ref, bdw_ref, lng_ref, lnb_ref, o_ref, upad_ref, yc_ref, *, seq, rb, rc, width):
    c = a_ref.shape[1]
    r = pl.program_id(1)
    fill_rows = min(seq, 64)

    @pl.when(r == 0)
    def _():
        zeros = jnp.zeros((CONV_PAD, c), F32)
        upad_ref[0:CONV_PAD, :] = zeros
        upad_ref[CONV_PAD + seq:CONV_PAD + seq + CONV_PAD, :] = zeros

        def fill(i, carry):
            src = pl.ds(pl.multiple_of(i * fill_rows, 8), fill_rows)
            dst = pl.ds(pl.multiple_of(CONV_PAD + i * fill_rows, 8), fill_rows)
            upad_ref[dst, :] = a_ref[src, :] * jax.nn.sigmoid(g_ref[src, :])
            return carry

        lax.fori_loop(0, seq // fill_rows, fill, 0)

    lead = CONV_PAD - width // 2
    n_shift = 8
    n_al = (lead + width - 1) // n_shift + 1
    win = rc + n_shift * n_al

    def row_chunk(ci, carry):
        base = pl.multiple_of(r * rb + ci * rc, 8)
        for cj in range(c // LANES):
            lanes = slice(cj * LANES, (cj + 1) * LANES)
            w_all = upad_ref[pl.ds(base, win), lanes]
            acc = jnp.zeros((rc, LANES), F32)
            for b in range(n_shift):
                taps = [(a8, n_shift * a8 + b - lead) for a8 in range(n_al)]
                taps = [(a8, k) for a8, k in taps if 0 <= k < width]
                if not taps:
                    continue
                w_b = w_all[b:b + rc + n_shift * (n_al - 1)]
                part = None
                for a8, k in taps:
                    term = wdw_ref[k:k + 1, lanes] * w_b[n_shift * a8:n_shift * a8 + rc]
                    part = term if part is None else part + term
                acc = acc + part
            yc_ref[pl.ds(pl.multiple_of(ci * rc, 8), rc), lanes] = acc + bdw_ref[:, lanes]
        return carry

    lax.fori_loop(0, rb // rc, row_chunk, 0)
    y = _ln(yc_ref[...]) * lng_ref[...] + lnb_ref[...]
    o_ref[...] = _silu(y).astype(o_ref.dtype)


def _conv_branch(hmat, col_a, col_g, w_dw, b_dw, ln_g, ln_b, *, row_off, n_seq, seq):
    width, c = w_dw.shape
    rb = min(seq, 256)
    rc = min(rb, 64)
    assert width // 2 <= CONV_PAD and row_off % seq == 0 and col_a % c == 0 and col_g % c == 0
    s_off = row_off // seq
    in_specs = [
        pl.BlockSpec((seq, c), lambda s, r: (s_off + s, col_a // c)),
        pl.BlockSpec((seq, c), lambda s, r: (s_off + s, col_g // c)),
        pl.BlockSpec((width, c), lambda s, r: (0, 0)),
        pl.BlockSpec((1, c), lambda s, r: (0, 0)),
        pl.BlockSpec((1, c), lambda s, r: (0, 0)),
        pl.BlockSpec((1, c), lambda s, r: (0, 0)),
    ]
    args = [hmat, hmat, w_dw, b_dw.reshape(1, c), ln_g.reshape(1, c), ln_b.reshape(1, c)]
    return pl.pallas_call(
        functools.partial(_conv_kernel, seq=seq, rb=rb, rc=rc, width=width),
        grid=(n_seq, seq // rb),
        in_specs=in_specs,
        out_specs=pl.BlockSpec((rb, c), lambda s, r: (s * (seq // rb) + r, 0)),
        out_shape=jax.ShapeDtypeStruct((n_seq * seq, c), BF16),
        scratch_shapes=[pltpu.VMEM((seq + 2 * CONV_PAD, c), F32), pltpu.VMEM((rb, c), F32)],
        compiler_params=_params("parallel", "arbitrary"),
        name="conv_branch",
    )(*args)


def _cumsum3(tri, la):
    hi = la.astype(BF16)
    r1 = la - hi.astype(F32)
    mid = r1.astype(BF16)
    lo = (r1 - mid.astype(F32)).astype(BF16)
    return _dot(tri, hi) + _dot(tri, mid) + _dot(tri, lo)


def _log_sigmoid(x):
    return jnp.minimum(x, 0.0) - jnp.log(1.0 + jnp.exp(-jnp.abs(x)))


def _gla_kernel(*refs, seq, chunk, has_state):
    it = iter(refs)
    q_ref, k_ref, v_ref, r_ref, z_ref = (next(it) for _ in range(5))
    wgf_ref, bgf_ref, wgb_ref, bgb_ref, ng_ref = (next(it) for _ in range(5))
    if has_state:
        s0f_ref, s0b_ref = next(it), next(it)
        o_ref = next(it)
        sf_ref = sb_ref = None
    else:
        o_ref, sf_ref, sb_ref = next(it), next(it), next(it)
    laf_ref, lab_ref, oacc_ref, kvf_ref, kvb_ref, qdf_ref, qdb_ref, decf_ref, decb_ref = (next(it) for _ in range(9))

    dk = q_ref.shape[1]
    dv = v_ref.shape[1]
    n_chunks = seq // chunk
    z = z_ref[...].astype(BF16)
    laf_ref[...] = _log_sigmoid(_dot(z, wgf_ref[...].astype(BF16)) + bgf_ref[...]) / GLA_TAU
    lab_ref[...] = _log_sigmoid(_dot(z, wgb_ref[...].astype(BF16)) + bgb_ref[...]) / GLA_TAU

    row = lax.broadcasted_iota(jnp.int32, (chunk, chunk), 0)
    col = lax.broadcasted_iota(jnp.int32, (chunk, chunk), 1)
    lower = row >= col
    upper = row <= col
    tri_f = jnp.where(lower, 1.0, 0.0).astype(BF16)
    tri_b = jnp.where(upper, 1.0, 0.0).astype(BF16)
    q_scale = dk ** -0.5

    def intra(n, rows, la_ref, tri, mask, edge, qd_ref, kv_ref, dec_ref):
        b = _cumsum3(tri, la_ref[rows, :])
        b_edge = b[edge:edge + 1, :]
        q = q_ref[rows, :] * q_scale
        k = k_ref[rows, :]
        v = v_ref[rows, :].astype(BF16)
        qd = (q * jnp.exp(b)).astype(BF16)
        kd = (k * jnp.exp(-b)).astype(BF16)
        ke = (k * jnp.exp(b_edge - b)).astype(BF16)
        att = jnp.where(mask, _dot_nt(qd, kd), 0.0).astype(BF16)
        qd_ref[rows, :] = qd
        kv_ref[n] = _dot_tn(v, ke)
        dec_ref[n] = jnp.broadcast_to(jnp.exp(b_edge), dec_ref.shape[1:])
        return _dot(att, v)

    def phase1(n, carry):
        rows = pl.ds(pl.multiple_of(n * chunk, chunk), chunk)
        oacc_ref[rows, :] = (intra(n, rows, laf_ref, tri_f, lower, chunk - 1, qdf_ref, kvf_ref, decf_ref)
                             + intra(n, rows, lab_ref, tri_b, upper, 0, qdb_ref, kvb_ref, decb_ref))
        return carry

    lax.fori_loop(0, n_chunks, phase1, 0, unroll=2)

    def scan(order, kv_ref, dec_ref, s0):
        def body(j, s):
            n = order(j)
            kv = kv_ref[n]
            kv_ref[n] = s
            return s * dec_ref[n, 0:1, :] + kv

        return lax.fori_loop(0, n_chunks, body, s0)

    zero = jnp.zeros((dv, dk), F32)
    s_f = scan(lambda j: j, kvf_ref, decf_ref, s0f_ref[0, 0].T if has_state else zero)
    s_b = scan(lambda j: n_chunks - 1 - j, kvb_ref, decb_ref, s0b_ref[0, 0].T if has_state else zero)
    if not has_state:
        sf_ref[0, 0] = s_f.T
        sb_ref[0, 0] = s_b.T

    def phase3(n, carry):
        rows = pl.ds(pl.multiple_of(n * chunk, chunk), chunk)
        o = (oacc_ref[rows, :] + _dot_nt(qdf_ref[rows, :], kvf_ref[n].astype(BF16))
             + _dot_nt(qdb_ref[rows, :], kvb_ref[n].astype(BF16)))
        o = _rms(o) * ng_ref[...]
        o_ref[rows, :] = (o * _silu(r_ref[rows, :])).astype(o_ref.dtype)
        return carry

    lax.fori_loop(0, n_chunks, phase3, 0, unroll=2)


def _gla_branch(hmat, cols, wgf, bgf, wgb, bgb, norm_g, *, heads, dk, dv, row_off, n_seq, seq, state=None):
    col_q, col_k, col_v, col_r, col_z = cols
    s_off = row_off // seq
    has_state = state is not None
    in_specs = [
        pl.BlockSpec((seq, dk), lambda s, h: (s_off + s, col_q // dk + h)),
        pl.BlockSpec((seq, dk), lambda s, h: (s_off + s, col_k // dk + h)),
        pl.BlockSpec((seq, dv), lambda s, h: (s_off + s, col_v // dv + h)),
        pl.BlockSpec((seq, dv), lambda s, h: (s_off + s, col_r // dv + h)),
        pl.BlockSpec((seq, LANES), lambda s, h: (s_off + s, col_z // LANES)),
        pl.BlockSpec((LANES, dk), lambda s, h: (0, h)),
        pl.BlockSpec((1, dk), lambda s, h: (0, h)),
        pl.BlockSpec((LANES, dk), lambda s, h: (0, h)),
        pl.BlockSpec((1, dk), lambda s, h: (0, h)),
        pl.BlockSpec((1, dv), lambda s, h: (0, h)),
    ]
    args = [hmat] * 5 + [wgf, bgf.reshape(1, -1), wgb, bgb.reshape(1, -1), norm_g.reshape(1, -1)]
    o_spec = pl.BlockSpec((seq, dv), lambda s, h: (s, h))
    o_shape = jax.ShapeDtypeStruct((n_seq * seq, heads * dv), BF16)
    if has_state:
        st_spec = pl.BlockSpec((1, 1, dk, dv), lambda s, h: (s, h, 0, 0))
        in_specs += [st_spec, st_spec]
        args += [state[0], state[1]]
        out_specs, out_shape = o_spec, o_shape
    else:
        st_spec = pl.BlockSpec((1, 1, dk, dv), lambda s, h: (s, h, 0, 0))
        st_shape = jax.ShapeDtypeStruct((n_seq, heads, dk, dv), F32)
        out_specs, out_shape = [o_spec, st_spec, st_spec], [o_shape, st_shape, st_shape]
    chunk = min(GLA_CHUNK, seq)
    n_chunks = seq // chunk
    return pl.pallas_call(
        functools.partial(_gla_kernel, seq=seq, chunk=chunk, has_state=has_state),
        grid=(n_seq, heads),
        in_specs=in_specs,
        out_specs=out_specs,
        out_shape=out_shape,
        scratch_shapes=[pltpu.VMEM((seq, dk), F32), pltpu.VMEM((seq, dk), F32), pltpu.VMEM((seq, dv), F32),
                        pltpu.VMEM((n_chunks, dv, dk), F32), pltpu.VMEM((n_chunks, dv, dk), F32),
                        pltpu.VMEM((seq, dk), BF16), pltpu.VMEM((seq, dk), BF16),
                        pltpu.VMEM((n_chunks, 8, dk), F32), pltpu.VMEM((n_chunks, 8, dk), F32)],
        compiler_params=_params("parallel", "parallel"),
        name="gla_branch",
    )(*args)


def _mla_proj_kernel(cq_ref, ckv_ref, pe_ref, qg_ref, kvg_ref, wq_ref, wqr_ref, tq_ref, tk_ref,
                     q_ref, ckvn_ref, kpe_ref, *, hw):
    cqn = (_rms(cq_ref[...]) * qg_ref[...]).astype(BF16)
    a = _dot(cqn, wq_ref[...])
    ar = _dot(cqn, wqr_ref[...])
    cos = tq_ref[:, 0:hw]
    sin = tq_ref[:, hw:2 * hw]
    for h in range(a.shape[1] // hw):
        sl = slice(h * hw, (h + 1) * hw)
        q_ref[:, sl] = (a[:, sl] * cos + ar[:, sl] * sin).astype(q_ref.dtype)
    ckvn_ref[...] = _rms(ckv_ref[...]) * kvg_ref[...]
    v = pe_ref[...] * tk_ref[...]
    v = v + pltpu.roll(v, LANES // 2, axis=1)
    lane = lax.broadcasted_iota(jnp.int32, v.shape, 1)
    kpe_ref[...] = jnp.where(lane < LANES // 2, v, 0.0)


def _mla_proj(hmat, col_cq, col_ckv, col_pe, q_g, kv_g, wq, wqr, tq, tk, tm=512):
    m = hmat.shape[0]
    rq = q_g.shape[0]
    rkv = kv_g.shape[0]
    nq = wq.shape[1]
    hw = tq.shape[1] // 2
    tm = min(tm, m)
    return pl.pallas_call(
        functools.partial(_mla_proj_kernel, hw=hw),
        grid=(m // tm,),
        in_specs=[
            pl.BlockSpec((tm, rq), lambda i: (i, col_cq // rq)),
            pl.BlockSpec((tm, rkv), lambda i: (i, col_ckv // rkv)),
            pl.BlockSpec((tm, LANES), lambda i: (i, col_pe // LANES)),
            pl.BlockSpec((1, rq), lambda i: (0, 0)),
            pl.BlockSpec((1, rkv), lambda i: (0, 0)),
            pl.BlockSpec((rq, nq), lambda i: (0, 0)),
            pl.BlockSpec((rq, nq), lambda i: (0, 0)),
            pl.BlockSpec((tm, 2 * hw), lambda i: (i, 0)),
            pl.BlockSpec((tm, LANES), lambda i: (i, 0)),
        ],
        out_specs=[
            pl.BlockSpec((tm, nq), lambda i: (i, 0)),
            pl.BlockSpec((tm, rkv), lambda i: (i, 0)),
            pl.BlockSpec((tm, LANES), lambda i: (i, 0)),
        ],
        out_shape=[
            jax.ShapeDtypeStruct((m, nq), BF16),
            jax.ShapeDtypeStruct((m, rkv), F32),
            jax.ShapeDtypeStruct((m, LANES), F32),
        ],
        compiler_params=_params("parallel"),
        name="mla_proj",
    )(hmat, hmat, hmat, q_g.reshape(1, rq), kv_g.reshape(1, rkv), wq, wqr, tq, tk)


def _attn_kernel(*refs, heads, hw, dh, scale):
    q_ref, kv_ref, kpe_ref = refs[:3]
    o_ref, kcat_ref = refs[-2:]

    @pl.when(pl.program_id(1) == 0)
    def _():
        kp = kpe_ref[0].astype(BF16)
        for h in range(heads):
            kcat_ref[:, h * hw:h * hw + dh] = kv_ref[0, :, h * dh:(h + 1) * dh]
            kcat_ref[:, h * hw + dh:(h + 1) * hw] = kp

    v_off = heads * dh
    for h in range(heads):
        s = _dot_nt(q_ref[:, h * hw:(h + 1) * hw], kcat_ref[:, h * hw:(h + 1) * hw]) * scale
        p = jnp.exp(s - jnp.max(s, axis=-1, keepdims=True))
        l = jnp.sum(p, axis=-1, keepdims=True)
        o = _dot(p.astype(BF16), kv_ref[0, :, v_off + h * dh:v_off + (h + 1) * dh]) / l
        o_ref[:, h * dh:(h + 1) * dh] = o.astype(o_ref.dtype)


def _attention(q_all, kv, kpe, *, heads, dh, scale, row_off, n_seq, seq, tq=256):
    s_len = kv.shape[1]
    hw = q_all.shape[1] // heads
    tq = min(tq, seq)
    q_off = row_off // tq
    in_specs = [
        pl.BlockSpec((tq, heads * hw), lambda b, i: (q_off + b * (seq // tq) + i, 0)),
        pl.BlockSpec((1, s_len, 2 * heads * dh), lambda b, i: (b, 0, 0)),
        pl.BlockSpec((1, s_len, LANES), lambda b, i: (b, 0, 0)),
    ]
    args = [q_all, kv, kpe]
    return pl.pallas_call(
        functools.partial(_attn_kernel, heads=heads, hw=hw, dh=dh, scale=scale),
        grid=(n_seq, seq // tq),
        in_specs=in_specs,
        out_specs=pl.BlockSpec((tq, heads * dh), lambda b, i: (b * (seq // tq) + i, 0)),
        out_shape=jax.ShapeDtypeStruct((n_seq * seq, heads * dh), BF16),
        scratch_shapes=[pltpu.VMEM((s_len, heads * hw), BF16)],
        compiler_params=_params("parallel", "arbitrary"),
        name="mla_attention",
    )(*args)


def _split3(x):
    hi = x.astype(BF16)
    lo = (x - hi.astype(F32)).astype(BF16)
    return hi, lo


def _post_attn_kernel(m_ref, x_ref, w_ref, mod_ref, g1_ref, b1_ref, rw_ref, rb_ref,
                      x1_ref, h2_ref, ri_ref, rg_ref, cnt_ref, run_ref, tri_ref, *, alpha, n_exp):
    tm = x_ref.shape[0]
    step = pl.program_id(0)

    @pl.when(step == 0)
    def _():
        run_ref[...] = jnp.zeros_like(run_ref)
        row = lax.broadcasted_iota(jnp.int32, (tm, tm), 0)
        col = lax.broadcasted_iota(jnp.int32, (tm, tm), 1)
        tri_ref[...] = jnp.where(row > col, 1.0, 0.0).astype(BF16)

    y = _dot(m_ref[...], w_ref[...])
    x1 = _ln(alpha * x_ref[...] + mod_ref[0, 2:3, :] * y) * g1_ref[...] + b1_ref[...]
    x1_ref[...] = x1
    h2 = _ln(x1) * (1.0 + mod_ref[0, 4:5, :]) + mod_ref[0, 3:4, :]
    half = h2_ref.shape[1]
    hi = pltpu.bitcast(h2[:, :half].astype(BF16).astype(F32), jnp.uint32)
    lo = pltpu.bitcast(h2[:, half:].astype(BF16).astype(F32), jnp.uint32)
    h2_ref[...] = hi | (lo >> 16)

    h_hi, h_lo = _split3(h2)
    w_hi, w_lo = _split3(rw_ref[...])
    logits = _dot(h_hi, w_hi) + _dot(h_lo, w_hi) + _dot(h_hi, w_lo) + rb_ref[...]
    lane = lax.broadcasted_iota(jnp.int32, logits.shape, 1).astype(F32)
    neg = jnp.float32(-jnp.inf)
    logits = jnp.where(lane < n_exp, logits, neg)

    counts = jnp.zeros(logits.shape, F32)
    vals, idxs = [], []
    for _ in range(TOP_K):
        mx = jnp.max(logits, axis=-1, keepdims=True)
        idx = jnp.min(jnp.where(logits == mx, lane, float(LANES)), axis=-1, keepdims=True)
        hit = lane == idx
        counts = counts + jnp.where(hit, 1.0, 0.0)
        logits = jnp.where(hit, neg, logits)
        vals.append(mx)
        idxs.append(idx)

    es = [jnp.exp(v - vals[0]) for v in vals]
    denom = es[0]
    for e in es[1:]:
        denom = denom + e
    before = _dot(tri_ref[...], counts.astype(BF16)) + run_ref[...]
    out_i = jnp.zeros(logits.shape, jnp.int32)
    out_g = jnp.zeros(logits.shape, F32)
    for k in range(TOP_K):
        rank = jnp.sum(jnp.where(lane == idxs[k], before, 0.0), axis=-1, keepdims=True)
        out_i = jnp.where(lane == k, idxs[k].astype(jnp.int32), out_i)
        out_i = jnp.where(lane == TOP_K + k, rank.astype(jnp.int32), out_i)
        out_g = jnp.where(lane == k, es[k] / denom, out_g)
    ri_ref[...] = out_i[:, 0:2 * TOP_K]
    rg_ref[...] = out_g[:, 0:TOP_K]
    run_ref[...] = run_ref[...] + jnp.sum(counts, axis=0, keepdims=True)
    cnt_ref[...] = run_ref[...]


def _post_attn(merged, x, w_out, mod, ln_g, ln_b, router_w, router_b, n_prompt, t_sample, alpha, n_exp, tm=256):
    m, d = x.shape
    tm = min(tm, t_sample, n_prompt)
    row_spec = pl.BlockSpec((tm, d), lambda i: (i, 0))
    vec_spec = pl.BlockSpec((1, d), lambda i: (0, 0))
    return pl.pallas_call(
        functools.partial(_post_attn_kernel, alpha=alpha, n_exp=n_exp),
        grid=(m // tm,),
        in_specs=[
            row_spec, row_spec,
            pl.BlockSpec((d, d), lambda i: (0, 0)),
            pl.BlockSpec((1, 6, d), lambda i: (_mod_row(i, tm, n_prompt, t_sample), 0, 0)),
            vec_spec, vec_spec,
            pl.BlockSpec((d, LANES), lambda i: (0, 0)),
            pl.BlockSpec((1, LANES), lambda i: (0, 0)),
        ],
        out_specs=[
            row_spec,
            pl.BlockSpec((tm, d // 2), lambda i: (i, 0)),
            pl.BlockSpec((tm, 2 * TOP_K), lambda i: (i, 0)),
            pl.BlockSpec((tm, TOP_K), lambda i: (i, 0)),
            pl.BlockSpec((1, LANES), lambda i: (0, 0)),
        ],
        out_shape=[
            jax.ShapeDtypeStruct((m, d), F32),
            jax.ShapeDtypeStruct((m, d // 2), jnp.uint32),
            jax.ShapeDtypeStruct((m, 2 * TOP_K), jnp.int32),
            jax.ShapeDtypeStruct((m, TOP_K), F32),
            jax.ShapeDtypeStruct((1, LANES), F32),
        ],
        scratch_shapes=[pltpu.VMEM((1, LANES), F32), pltpu.VMEM((tm, tm), BF16)],
        compiler_params=_params("arbitrary"),
        name="post_attn_router",
    )(merged, x, w_out, mod, ln_g.reshape(1, d), ln_b.reshape(1, d), router_w, router_b)


def _wait_slot(buf_ref, sem, slot):
    pltpu.make_async_copy(buf_ref.at[slot], buf_ref.at[slot], sem.at[slot]).wait()


def _moe_expert_kernel(be_ref, bv_ref, idx_ref, nxt_ref, h_hbm, wg_ref, bg_ref, wu_ref, bu_ref, wd_ref, bd_ref,
                       o_ref, xbuf_ref, xb_ref, sem, *, n_f):
    i = pl.program_id(0)
    f = pl.program_id(1)
    n_blk = pl.num_programs(0)
    n_rows, half = xbuf_ref.shape[1], xbuf_ref.shape[2]
    share = n_rows // n_f
    valid = bv_ref[i]
    slot = i % 2
    del be_ref

    def issue(ids_ref, s, lo, n):
        def body(j, carry):
            r = lo + j
            pltpu.make_async_copy(h_hbm.at[pl.ds(ids_ref[0, 0, r], 1), :], xbuf_ref.at[s, pl.ds(r, 1), :],
                                  sem.at[s]).start()
            return carry

        lax.fori_loop(0, n, body, 0, unroll=8)

    @pl.when(jnp.logical_and(jnp.logical_and(i == 0, f == 0), valid > 0))
    def _():
        issue(idx_ref, 0, 0, n_rows)

    nxt_used = bv_ref[jnp.minimum(i + 1, n_blk - 1)] > 0

    @pl.when(jnp.logical_and(i + 1 < n_blk, nxt_used))
    def _():
        issue(nxt_ref, 1 - slot, f * share, share)

    @pl.when(f == 0)
    def _():
        o_ref[...] = jnp.broadcast_to(bd_ref[0, 0], o_ref.shape)

    @pl.when(jnp.logical_and(f == 0, valid > 0))
    def _():
        _wait_slot(xbuf_ref, sem, slot)
        u = xbuf_ref[slot]
        xb_ref[:, :half] = pltpu.bitcast(u & jnp.uint32(0xFFFF0000), F32).astype(BF16)
        xb_ref[:, half:] = pltpu.bitcast(u << 16, F32).astype(BF16)

    for sb in range(n_rows // MOE_SUB):
        rows = slice(sb * MOE_SUB, (sb + 1) * MOE_SUB)

        @pl.when(sb * MOE_SUB < valid)
        def _():
            x = xb_ref[rows, :]
            gt = jnp.minimum(_dot(x, wg_ref[0, 0].astype(BF16)) + bg_ref[0, 0], SWIGLU_LIMIT)
            up = jnp.clip(_dot(x, wu_ref[0, 0].astype(BF16)) + bu_ref[0, 0], -SWIGLU_LIMIT, SWIGLU_LIMIT)
            act = gt * jax.nn.sigmoid(SWIGLU_ALPHA * gt) * (up + 1.0)
            o_ref[rows, :] = o_ref[rows, :] + _dot(act.astype(BF16), wd_ref[0, 0].astype(BF16))


def _moe_experts(h2p, buf_tok, blk_e, blk_valid, layer, w_gate, b_gate, w_up, b_up, w_down, b_down, tf=256):
    cap = buf_tok.shape[0]
    half = h2p.shape[1]
    n_layers, n_exp, d, ff = w_gate.shape
    assert d == 2 * half
    tf = _tile(ff, tf)
    nf = ff // tf
    nblk = cap // MOE_ROWS
    assert MOE_ROWS % nf == 0
    ids = buf_tok.reshape(nblk, 1, MOE_ROWS)

    def f_idx(i, f, bv):
        return jnp.where(bv[i] > 0, f, nf - 1)

    return pl.pallas_call(
        functools.partial(_moe_expert_kernel, n_f=nf),
        grid_spec=pltpu.PrefetchScalarGridSpec(
            num_scalar_prefetch=2,
            grid=(nblk, nf),
            in_specs=[
                pl.BlockSpec((1, 1, MOE_ROWS), lambda i, f, be, bv: (i, 0, 0), memory_space=pltpu.SMEM),
                pl.BlockSpec((1, 1, MOE_ROWS), lambda i, f, be, bv: (jnp.minimum(i + 1, nblk - 1), 0, 0),
                             memory_space=pltpu.SMEM),
                pl.BlockSpec(memory_space=pl.ANY),
                pl.BlockSpec((1, 1, d, tf), lambda i, f, be, bv: (layer, be[i], 0, f_idx(i, f, bv))),
                pl.BlockSpec((1, 1, 1, tf), lambda i, f, be, bv: (layer, be[i], 0, f_idx(i, f, bv))),
                pl.BlockSpec((1, 1, d, tf), lambda i, f, be, bv: (layer, be[i], 0, f_idx(i, f, bv))),
                pl.BlockSpec((1, 1, 1, tf), lambda i, f, be, bv: (layer, be[i], 0, f_idx(i, f, bv))),
                pl.BlockSpec((1, 1, tf, d), lambda i, f, be, bv: (layer, be[i], f_idx(i, f, bv), 0)),
                pl.BlockSpec((1, 1, 1, d), lambda i, f, be, bv: (layer, be[i], 0, 0)),
            ],
            out_specs=pl.BlockSpec((MOE_ROWS, d), lambda i, f, be, bv: (i, 0)),
            scratch_shapes=[pltpu.VMEM((2, MOE_ROWS, half), jnp.uint32), pltpu.VMEM((MOE_ROWS, d), BF16),
                            pltpu.SemaphoreType.DMA((2,))],
        ),
        out_shape=jax.ShapeDtypeStruct((cap, d), F32),
        compiler_params=_params("arbitrary", "arbitrary"),
        name="moe_experts",
    )(blk_e, blk_valid, ids, ids, h2p, w_gate, b_gate.reshape(n_layers, n_exp, 1, ff), w_up, b_up.reshape(n_layers, n_exp, 1, ff),
      w_down, b_down.reshape(n_layers, n_exp, 1, d))


def _moe_combine_kernel(idx_ref, nxt_ref, yb_hbm, g_ref, x_ref, mod_ref, lg_ref, lb_ref, o_ref, buf_ref, sem, *,
                        alpha):
    tc = x_ref.shape[0]
    blk = pl.program_id(0)
    slot = blk % 2

    def issue(ids_ref, s):
        def body(r, carry):
            for k in range(TOP_K):
                pltpu.make_async_copy(yb_hbm.at[pl.ds(ids_ref[0, 0, r * TOP_K + k], 1), :],
                                      buf_ref.at[s, k, pl.ds(r, 1), :], sem.at[s]).start()
            return carry

        lax.fori_loop(0, tc, body, 0, unroll=2)

    @pl.when(blk == 0)
    def _():
        issue(idx_ref, 0)

    @pl.when(blk + 1 < pl.num_programs(0))
    def _():
        issue(nxt_ref, 1 - slot)

    _wait_slot(buf_ref, sem, slot)
    g = g_ref[...]
    y = g[:, 0:1] * buf_ref[slot, 0]
    for k in range(1, TOP_K):
        y = y + g[:, k:k + 1] * buf_ref[slot, k]
    x2 = _ln(alpha * x_ref[...] + mod_ref[0, 5:6, :] * y) * lg_ref[...] + lb_ref[...]
    o_ref[...] = x2


def _moe_combine(yb, dest, gates, x1, mod, ln_g, ln_b, n_prompt, t_sample, alpha, tc=128):
    m, d = x1.shape
    tc = min(tc, t_sample, n_prompt)
    nb = m // tc
    ids = dest.reshape(nb, 1, tc * TOP_K)
    return pl.pallas_call(
        functools.partial(_moe_combine_kernel, alpha=alpha),
        grid=(nb,),
        in_specs=[
            pl.BlockSpec((1, 1, tc * TOP_K), lambda i: (i, 0, 0), memory_space=pltpu.SMEM),
            pl.BlockSpec((1, 1, tc * TOP_K), lambda i: (jnp.minimum(i + 1, nb - 1), 0, 0), memory_space=pltpu.SMEM),
            pl.BlockSpec(memory_space=pl.ANY),
            pl.BlockSpec((tc, TOP_K), lambda i: (i, 0)),
            pl.BlockSpec((tc, d), lambda i: (i, 0)),
            pl.BlockSpec((1, 6, d), lambda i: (_mod_row(i, tc, n_prompt, t_sample), 0, 0)),
            pl.BlockSpec((1, d), lambda i: (0, 0)),
            pl.BlockSpec((1, d), lambda i: (0, 0)),
        ],
        out_specs=pl.BlockSpec((tc, d), lambda i: (i, 0)),
        out_shape=jax.ShapeDtypeStruct((m, d), F32),
        scratch_shapes=[pltpu.VMEM((2, TOP_K, tc, d), F32), pltpu.SemaphoreType.DMA((2,))],
        compiler_params=_params("arbitrary"),
        name="moe_combine",
    )(ids, ids, yb, gates, x1, mod, ln_g.reshape(1, d), ln_b.reshape(1, d))


def _moe_plan(route_i, counts, n_exp):
    m = route_i.shape[0]
    top_e = route_i[:, :TOP_K]
    rank = route_i[:, TOP_K:]
    counts = counts[0, :n_exp].astype(jnp.int32)
    padded = (counts + MOE_ROWS - 1) // MOE_ROWS * MOE_ROWS
    pend = jnp.cumsum(padded)
    pstart = pend - padded
    dest = pstart[top_e] + rank
    nblk = (m * TOP_K + n_exp * (MOE_ROWS - 1)) // MOE_ROWS
    blk_row = jnp.arange(nblk, dtype=jnp.int32) * MOE_ROWS
    n_used = pend[-1] // MOE_ROWS
    blk_e = jnp.minimum(jnp.searchsorted(pend, blk_row, side="right"), n_exp - 1).astype(jnp.int32)
    blk_valid = jnp.clip(counts[blk_e] - (blk_row - pstart[blk_e]), 0, MOE_ROWS).astype(jnp.int32)
    used = jnp.arange(nblk, dtype=jnp.int32) < n_used
    blk_valid = jnp.where(used, blk_valid, 0)
    blk_e = jnp.where(used, blk_e, blk_e[jnp.maximum(n_used - 1, 0)])
    tok = jnp.broadcast_to(jnp.arange(m, dtype=jnp.int32)[:, None], (m, TOP_K))
    buf_tok = jnp.zeros((nblk * MOE_ROWS,), jnp.int32).at[dest.reshape(-1)].set(tok.reshape(-1))
    return dest.astype(jnp.int32), buf_tok, blk_e, blk_valid


def _rot_half_cols(w):
    w1, w2, w3, w4 = jnp.split(w, 4, axis=-1)
    return jnp.concatenate([-w2, w1, -w4, w3], axis=-1)


def _rope_tables(n_prompt, n_batch, t_sample, rope, dh):
    rows = t_sample // GRID_W
    row = jnp.repeat(jnp.arange(rows), GRID_W).astype(F32)
    col = jnp.tile(jnp.arange(GRID_W), rows).astype(F32)
    n_freq = rope // 4
    inv = ROPE_BASE ** (-jnp.arange(n_freq, dtype=F32) / n_freq)
    ar = row[:, None] * inv
    ac = col[:, None] * inv
    ang = jnp.concatenate([ar, ar, ac, ac], axis=-1)
    cos = jnp.concatenate([jnp.ones((n_prompt, rope), F32), jnp.tile(jnp.cos(ang), (n_batch, 1))], axis=0)
    sin = jnp.concatenate([jnp.zeros((n_prompt, rope), F32), jnp.tile(jnp.sin(ang), (n_batch, 1))], axis=0)
    n = cos.shape[0]
    hw = 2 * dh
    pad = jnp.zeros((n, hw - dh - rope), F32)
    tq = jnp.concatenate([jnp.ones((n, dh), F32), cos, pad, jnp.zeros((n, dh), F32), sin, pad], axis=-1)
    tk = jnp.concatenate([cos, sin], axis=-1)
    return tq, tk


def kernel(x_prompt, x_sample, c, cache_ckv, cache_kpe, state_gla_fwd, state_gla_bwd, c_ctx, w_mod, b_mod, w_in,
           conv_w_dw, conv_b_dw, conv_ln_g, conv_ln_b, conv_w_out, gla_w_gate_fwd, gla_b_gate_fwd,
           gla_w_gate_bwd, gla_b_gate_bwd, gla_norm_g, gla_w_out, mla_q_norm_g, mla_w_uq, mla_kv_norm_g,
           mla_w_uk, mla_w_uv, mla_w_out, w_out, ln1_g, ln1_b, router_w, router_b, exp_w_gate, exp_b_gate,
           exp_w_up, exp_b_up, exp_w_down, exp_b_down, ln2_g, ln2_b):
    bp, tp, d = x_prompt.shape
    bs, ts, _ = x_sample.shape
    depth = w_in.shape[0]
    n_prompt, n_sample = bp * tp, bs * ts
    n_tok = n_prompt + n_sample
    conv_dim = conv_w_dw.shape[2]
    gla_heads, gla_dk, gla_dv = state_gla_fwd.shape[2:]
    gla_qk, gla_v = gla_heads * gla_dk, gla_heads * gla_dv
    gate_rank = gla_w_gate_fwd.shape[1]
    q_rank = mla_w_uq.shape[1]
    kv_rank = mla_w_uk.shape[1]
    rope = cache_kpe.shape[3]
    dh = mla_w_uk.shape[2] // MLA_HEADS
    n_exp = router_w.shape[2]
    past = cache_ckv.shape[2]
    alpha = (2 * depth) ** 0.25
    assert 2 * dh == 2 * LANES and rope == LANES // 2 and 2 * gate_rank <= LANES

    sizes = (conv_dim, conv_dim, gla_qk, gla_qk, gla_v, gla_v, gate_rank, gate_rank, q_rank, kv_rank, rope, d, d, d)
    offs = [0]
    for s in sizes:
        offs.append(offs[-1] + s)
    o_zf, o_cq, o_kpe, o_ma = offs[6], offs[8], offs[10], offs[11]
    c_ca, c_cg, c_gq, c_gk, c_gv, c_gr = offs[0], offs[1], offs[2], offs[3], offs[4], offs[5]
    c_cq = o_zf
    c_ckv = c_cq + q_rank
    c_ma = c_ckv + kv_rank
    c_pe = c_ma + 3 * d
    c_z = c_pe + LANES
    n_in = c_z + LANES

    cc = jnp.concatenate([c_ctx[None, :], c], axis=0)
    mp = (cc.shape[0] + 7) // 8 * 8
    cc = jnp.pad(cc, ((0, mp - cc.shape[0]), (0, 0)))
    mod_all = _modulation(cc, w_mod, b_mod).reshape(depth, mp, 6, d)

    tq_tab, tk_tab = _rope_tables(n_prompt, bs, ts, rope, dh)
    x = jnp.concatenate([x_prompt.reshape(n_prompt, d), x_sample.reshape(n_sample, d)], axis=0)
    scale = (dh + rope) ** -0.5

    ckv_l, kpe_l, sf_l, sb_l = [], [], [], []
    for l in range(depth):
        mod = mod_all[l]
        wl = w_in[l]
        kpe_cols = wl[:, o_kpe:o_kpe + rope]
        w_in_r = jnp.concatenate(
            [wl[:, :o_zf], wl[:, o_cq:o_kpe], wl[:, o_ma:], kpe_cols, _rot_half_cols(kpe_cols),
             wl[:, o_zf:o_cq], jnp.zeros((d, LANES - 2 * gate_rank), F32)], axis=1).astype(BF16)
        assert w_in_r.shape[1] == n_in
        zpad = jnp.zeros((LANES - 2 * gate_rank, gla_qk), F32)
        wgf = jnp.concatenate([gla_w_gate_fwd[l], jnp.zeros((gate_rank, gla_qk), F32), zpad], axis=0)
        wgb = jnp.concatenate([jnp.zeros((gate_rank, gla_qk), F32), gla_w_gate_bwd[l], zpad], axis=0)
        wq3 = mla_w_uq[l].reshape(q_rank, MLA_HEADS, dh + rope)
        zq = jnp.zeros((q_rank, MLA_HEADS, dh - rope), F32)
        wq = jnp.concatenate([wq3, zq], axis=-1).reshape(q_rank, -1).astype(BF16)
        wqr = jnp.concatenate([jnp.zeros((q_rank, MLA_HEADS, dh), F32), _rot_half_cols(wq3[..., dh:]), zq],
                              axis=-1).reshape(q_rank, -1).astype(BF16)
        wkv = jnp.concatenate([mla_w_uk[l], mla_w_uv[l]], axis=1).astype(BF16)
        rw = jnp.pad(router_w[l], ((0, 0), (0, LANES - n_exp)))
        rb = jnp.pad(router_b[l], (0, LANES - n_exp)).reshape(1, LANES)

        hmat = _gemm_in(x, mod, w_in_r, n_prompt, ts)

        conv_args = (hmat, c_ca, c_cg, conv_w_dw[l], conv_b_dw[l], conv_ln_g[l], conv_ln_b[l])
        ya = (_conv_branch(*conv_args, row_off=0, n_seq=bp, seq=tp),
              _conv_branch(*conv_args, row_off=n_prompt, n_seq=bs, seq=ts))

        gla_args = (hmat, (c_gq, c_gk, c_gv, c_gr, c_z), wgf, gla_b_gate_fwd[l], wgb, gla_b_gate_bwd[l],
                    gla_norm_g[l])
        gla_kw = dict(heads=gla_heads, dk=gla_dk, dv=gla_dv)
        og_p, s_f, s_b = _gla_branch(*gla_args, row_off=0, n_seq=bp, seq=tp, **gla_kw)
        og_s = _gla_branch(*gla_args, row_off=n_prompt, n_seq=bs, seq=ts, **gla_kw,
                           state=(state_gla_fwd[:, l], state_gla_bwd[:, l]))

        q_all, ckv, kpe = _mla_proj(hmat, c_cq, c_ckv, c_pe, mla_q_norm_g[l], mla_kv_norm_g[l], wq, wqr,
                                    tq_tab, tk_tab)
        kv_p = _gemm(ckv, wkv, rows=n_prompt, name="mla_kv").reshape(bp, tp, -1)
        keys_s = jnp.concatenate([cache_ckv[:, l], ckv[n_prompt:].reshape(bs, ts, kv_rank)], axis=1)
        kv_s = _gemm(keys_s.reshape(bs * (past + ts), kv_rank), wkv, name="mla_kv")
        kv_s = kv_s.reshape(bs, past + ts, -1)
        kpe_p = kpe[:n_prompt].reshape(bp, tp, LANES)
        kpe_s = jnp.concatenate([jnp.pad(cache_kpe[:, l], ((0, 0), (0, 0), (0, LANES - rope))),
                                 kpe[n_prompt:].reshape(bs, ts, LANES)], axis=1)
        att_kw = dict(heads=MLA_HEADS, dh=dh, scale=scale)
        att = (_attention(q_all, kv_p, kpe_p, row_off=0, n_seq=bp, seq=tp, **att_kw),
               _attention(q_all, kv_s, kpe_s, row_off=n_prompt, n_seq=bs, seq=ts, **att_kw))
        merged = _merge([ya, (og_p, og_s), att],
                        [conv_w_out[l].astype(BF16), gla_w_out[l].astype(BF16), mla_w_out[l].astype(BF16)],
                        hmat, c_ma)

        x1, h2, route_i, route_g, counts = _post_attn(merged, x, w_out[l].astype(BF16), mod, ln1_g[l], ln1_b[l],
                                                      rw, rb, n_prompt, ts, alpha, n_exp)

        dest, buf_tok, blk_e, blk_valid = _moe_plan(route_i, counts, n_exp)
        yb = _moe_experts(h2, buf_tok, blk_e, blk_valid, l, exp_w_gate, exp_b_gate, exp_w_up, exp_b_up, exp_w_down,
                          exp_b_down)
        x = _moe_combine(yb, dest, route_g, x1, mod, ln2_g[l], ln2_b[l], n_prompt, ts, alpha)

        ckv_l.append(ckv[:n_prompt].reshape(bp, tp, kv_rank))
        kpe_l.append(kpe_p[..., :rope])
        sf_l.append(s_f)
        sb_l.append(s_b)

    y_prompt = x[:n_prompt].reshape(bp, tp, d)
    y_sample = x[n_prompt:].reshape(bs, ts, d)
    return (y_prompt, y_sample, jnp.stack(ckv_l, axis=1), jnp.stack(kpe_l, axis=1),
            jnp.stack(sf_l, axis=1), jnp.stack(sb_l, axis=1))
```

```python
import functools

import jax
import jax.numpy as jnp
from jax import lax
from jax.experimental import pallas as pl
from jax.experimental.pallas import tpu as pltpu

F32 = jnp.float32
BF16 = jnp.bfloat16

LANES = 128
VMEM_LIMIT_BYTES = 56 * 1024 * 1024
GEMM_IN_VMEM_LIMIT_BYTES = 60 * 1024 * 1024

GRID_W = 64
GLA_TAU = 16.0
GLA_CHUNK = 64
MLA_HEADS = 8
ROPE_BASE = 10000.0
TOP_K = 4
SWIGLU_LIMIT = 7.0
SWIGLU_ALPHA = 1.702
LN_EPS = 1e-5
RMS_EPS = 1e-6

CONV_PAD = 16
MOE_ROWS = 1024
MOE_SUB = 512


def _params(*sem, vmem=VMEM_LIMIT_BYTES):
    return pltpu.CompilerParams(dimension_semantics=sem, vmem_limit_bytes=vmem)


def _tile(n, pref):
    if n <= pref:
        return n
    t = (pref // LANES) * LANES
    while t > LANES and n % t:
        t -= LANES
    assert n % t == 0, (n, pref)
    return t


def _ln(x):
    mu = jnp.mean(x, axis=-1, keepdims=True)
    xc = x - mu
    return xc * lax.rsqrt(jnp.mean(xc * xc, axis=-1, keepdims=True) + LN_EPS)


def _rms(x):
    return x * lax.rsqrt(jnp.mean(x * x, axis=-1, keepdims=True) + RMS_EPS)


def _silu(x):
    return x * jax.nn.sigmoid(x)


def _dot(a, b):
    return jnp.dot(a, b, preferred_element_type=F32)


def _dot_nt(a, b):
    return lax.dot_general(a, b, (((1,), (1,)), ((), ())), preferred_element_type=F32)


def _dot_tn(a, b):
    return lax.dot_general(a, b, (((0,), (0,)), ((), ())), preferred_element_type=F32)


def _mod_kernel(c_ref, w_ref, b_ref, o_ref):
    s = _silu(c_ref[...])
    o_ref[0] = _dot(s.astype(BF16), w_ref[0].astype(BF16)) + b_ref[0]


def _modulation(cc, w_mod, b_mod):
    n_layers, d, d6 = w_mod.shape
    mp = cc.shape[0]
    tn = _tile(d6, 1024)
    return pl.pallas_call(
        _mod_kernel,
        grid=(n_layers, d6 // tn),
        in_specs=[
            pl.BlockSpec((mp, d), lambda l, j: (0, 0)),
            pl.BlockSpec((1, d, tn), lambda l, j: (l, 0, j)),
            pl.BlockSpec((1, 1, tn), lambda l, j: (l, 0, j)),
        ],
        out_specs=pl.BlockSpec((1, mp, tn), lambda l, j: (l, 0, j)),
        out_shape=jax.ShapeDtypeStruct((n_layers, mp, d6), F32),
        compiler_params=_params("parallel", "parallel"),
        name="modulation",
    )(cc, w_mod, b_mod.reshape(n_layers, 1, d6))


def _mod_row(i, tm, n_prompt, t_sample):
    r = i * tm
    return jnp.where(r < n_prompt, 0, 1 + (r - n_prompt) // t_sample)


def _gemm_in_kernel(x_ref, mod_ref, w_ref, o_ref, h_ref):
    @pl.when(pl.program_id(1) == 0)
    def _():
        y = _ln(x_ref[...])
        h_ref[...] = (y * (1.0 + mod_ref[0, 1:2, :]) + mod_ref[0, 0:1, :]).astype(BF16)

    o_ref[...] = _dot(h_ref[...], w_ref[...])


def _gemm_in(x, mod, w, n_prompt, t_sample, tm=1024, tn=1792):
    m, d = x.shape
    n = w.shape[1]
    tm = min(tm, t_sample, n_prompt)
    tn = _tile(n, tn)
    assert m % tm == 0 and n_prompt % tm == 0 and t_sample % tm == 0
    return pl.pallas_call(
        _gemm_in_kernel,
        grid=(m // tm, n // tn),
        in_specs=[
            pl.BlockSpec((tm, d), lambda i, j: (i, 0)),
            pl.BlockSpec((1, 6, d), lambda i, j: (_mod_row(i, tm, n_prompt, t_sample), 0, 0)),
            pl.BlockSpec((d, tn), lambda i, j: (0, j)),
        ],
        out_specs=pl.BlockSpec((tm, tn), lambda i, j: (i, j)),
        out_shape=jax.ShapeDtypeStruct((m, n), F32),
        scratch_shapes=[pltpu.VMEM((tm, d), BF16)],
        compiler_params=_params("parallel", "arbitrary", vmem=GEMM_IN_VMEM_LIMIT_BYTES),
        name="gemm_in",
    )(x, mod, w)


def _gemm_kernel(x_ref, w_ref, o_ref):
    o_ref[...] = _dot(x_ref[...].astype(BF16), w_ref[...]).astype(o_ref.dtype)


def _gemm(x, w, *, rows=None, tm=512, tn=1024, name="gemm"):
    k, n = w.shape
    m = x.shape[0] if rows is None else rows
    tm = min(tm, m)
    tn = _tile(n, tn)
    assert m % tm == 0 and x.shape[1] == k
    return pl.pallas_call(
        _gemm_kernel,
        grid=(m // tm, n // tn),
        in_specs=[pl.BlockSpec((tm, k), lambda i, j: (i, 0)), pl.BlockSpec((k, tn), lambda i, j: (0, j))],
        out_specs=pl.BlockSpec((tm, tn), lambda i, j: (i, j)),
        out_shape=jax.ShapeDtypeStruct((m, n), BF16),
        compiler_params=_params("parallel", "parallel"),
        name=name,
    )(x, w)


def _merge_kernel(*refs, n_br, nb_first):
    x_refs = refs[:2 * n_br]
    w_refs = refs[2 * n_br:3 * n_br]
    g_refs = refs[3 * n_br:4 * n_br]
    o_ref = refs[4 * n_br]

    def compute(group):
        acc = None
        for b in range(n_br):
            term = jax.nn.sigmoid(g_refs[b][...]) * _dot(x_refs[2 * b + group][...], w_refs[b][...])
            acc = term if acc is None else acc + term
        o_ref[...] = acc.astype(o_ref.dtype)

    i = pl.program_id(0)
    pl.when(i < nb_first)(lambda: compute(0))
    pl.when(i >= nb_first)(lambda: compute(1))


def _merge(ys, ws, gate, gate_col, tm=512, tn=1024):
    n_br = len(ys)
    n = ws[0].shape[1]
    m = ys[0][0].shape[0] + ys[0][1].shape[0]
    tm = min(tm, ys[0][0].shape[0], ys[0][1].shape[0])
    tn = _tile(n, tn)
    nb_first = ys[0][0].shape[0] // tm
    assert gate_col % tn == 0 and n % tn == 0 and all(y[0].shape[0] % tm == 0 and y[1].shape[0] % tm == 0 for y in ys)
    in_specs, args = [], []
    for y, w in zip(ys, ws):
        assert w.shape == (y[0].shape[1], n) and y[1].shape[1] == w.shape[0]
        in_specs += [pl.BlockSpec((tm, w.shape[0]), lambda i, j: (jnp.minimum(i, nb_first - 1), 0)),
                     pl.BlockSpec((tm, w.shape[0]), lambda i, j: (jnp.maximum(i - nb_first, 0), 0))]
        args += [y[0], y[1]]
    in_specs += [pl.BlockSpec((w.shape[0], tn), lambda i, j: (0, j)) for w in ws]
    args += list(ws)
    for b in range(n_br):
        goff = (gate_col + b * n) // tn
        in_specs.append(pl.BlockSpec((tm, tn), lambda i, j, goff=goff: (i, goff + j)))
        args.append(gate)
    return pl.pallas_call(
        functools.partial(_merge_kernel, n_br=n_br, nb_first=nb_first),
        grid=(m // tm, n // tn),
        in_specs=in_specs,
        out_specs=pl.BlockSpec((tm, tn), lambda i, j: (i, j)),
        out_shape=jax.ShapeDtypeStruct((m, n), BF16),
        compiler_params=_params("parallel", "parallel"),
        name="merge_out",
    )(*args)


def _conv_kernel(a_ref, g_ref, wdw_ref, bdw_ref, lng_ref, lnb_ref, o_ref, upad_ref, yc_ref, *, seq, rb, rc, width):
    c = a_ref.shape[1]
    r = pl.program_id(1)
    fill_rows = min(seq, 64)

    @pl.when(r == 0)
    def _():
        zeros = jnp.zeros((CONV_PAD, c), F32)
        upad_ref[0:CONV_PAD, :] = zeros
        upad_ref[CONV_PAD + seq:CONV_PAD + seq + CONV_PAD, :] = zeros

        def fill(i, carry):
            src = pl.ds(pl.multiple_of(i * fill_rows, 8), fill_rows)
            dst = pl.ds(pl.multiple_of(CONV_PAD + i * fill_rows, 8), fill_rows)
            upad_ref[dst, :] = a_ref[src, :] * jax.nn.sigmoid(g_ref[src, :])
            return carry

        lax.fori_loop(0, seq // fill_rows, fill, 0)

    lead = CONV_PAD - width // 2
    n_shift = 8
    n_al = (lead + width - 1) // n_shift + 1
    win = rc + n_shift * n_al

    def row_chunk(ci, carry):
        base = pl.multiple_of(r * rb + ci * rc, 8)
        for cj in range(c // LANES):
            lanes = slice(cj * LANES, (cj + 1) * LANES)
            w_all = upad_ref[pl.ds(base, win), lanes]
            acc = jnp.zeros((rc, LANES), F32)
            for b in range(n_shift):
                taps = [(a8, n_shift * a8 + b - lead) for a8 in range(n_al)]
                taps = [(a8, k) for a8, k in taps if 0 <= k < width]
                if not taps:
                    continue
                w_b = w_all[b:b + rc + n_shift * (n_al - 1)]
                part = None
                for a8, k in taps:
                    term = wdw_ref[k:k + 1, lanes] * w_b[n_shift * a8:n_shift * a8 + rc]
                    part = term if part is None else part + term
                acc = acc + part
            yc_ref[pl.ds(pl.multiple_of(ci * rc, 8), rc), lanes] = acc + bdw_ref[:, lanes]
        return carry

    lax.fori_loop(0, rb // rc, row_chunk, 0)
    y = _ln(yc_ref[...]) * lng_ref[...] + lnb_ref[...]
    o_ref[...] = _silu(y).astype(o_ref.dtype)


def _conv_branch(hmat, col_a, col_g, w_dw, b_dw, ln_g, ln_b, *, row_off, n_seq, seq):
    width, c = w_dw.shape
    rb = min(seq, 256)
    rc = min(rb, 64)
    assert width // 2 <= CONV_PAD and row_off % seq == 0 and col_a % c == 0 and col_g % c == 0
    s_off = row_off // seq
    in_specs = [
        pl.BlockSpec((seq, c), lambda s, r: (s_off + s, col_a // c)),
        pl.BlockSpec((seq, c), lambda s, r: (s_off + s, col_g // c)),
        pl.BlockSpec((width, c), lambda s, r: (0, 0)),
        pl.BlockSpec((1, c), lambda s, r: (0, 0)),
        pl.BlockSpec((1, c), lambda s, r: (0, 0)),
        pl.BlockSpec((1, c), lambda s, r: (0, 0)),
    ]
    args = [hmat, hmat, w_dw, b_dw.reshape(1, c), ln_g.reshape(1, c), ln_b.reshape(1, c)]
    return pl.pallas_call(
        functools.partial(_conv_kernel, seq=seq, rb=rb, rc=rc, width=width),
        grid=(n_seq, seq // rb),
        in_specs=in_specs,
        out_specs=pl.BlockSpec((rb, c), lambda s, r: (s * (seq // rb) + r, 0)),
        out_shape=jax.ShapeDtypeStruct((n_seq * seq, c), BF16),
        scratch_shapes=[pltpu.VMEM((seq + 2 * CONV_PAD, c), F32), pltpu.VMEM((rb, c), F32)],
        compiler_params=_params("parallel", "arbitrary"),
        name="conv_branch",
    )(*args)


def _cumsum3(tri, la):
    hi = la.astype(BF16)
    r1 = la - hi.astype(F32)
    mid = r1.astype(BF16)
    lo = (r1 - mid.astype(F32)).astype(BF16)
    return _dot(tri, hi) + _dot(tri, mid) + _dot(tri, lo)


def _log_sigmoid(x):
    return jnp.minimum(x, 0.0) - jnp.log(1.0 + jnp.exp(-jnp.abs(x)))


def _gla_kernel(*refs, seq, chunk, has_state):
    it = iter(refs)
    q_ref, k_ref, v_ref, r_ref, z_ref = (next(it) for _ in range(5))
    wgf_ref, bgf_ref, wgb_ref, bgb_ref, ng_ref = (next(it) for _ in range(5))
    if has_state:
        s0f_ref, s0b_ref = next(it), next(it)
        o_ref = next(it)
        sf_ref = sb_ref = None
    else:
        o_ref, sf_ref, sb_ref = next(it), next(it), next(it)
    laf_ref, lab_ref, oacc_ref, kvf_ref, kvb_ref, qdf_ref, qdb_ref, decf_ref, decb_ref = (next(it) for _ in range(9))

    dk = q_ref.shape[1]
    dv = v_ref.shape[1]
    n_chunks = seq // chunk
    z = z_ref[...].astype(BF16)
    laf_ref[...] = _log_sigmoid(_dot(z, wgf_ref[...].astype(BF16)) + bgf_ref[...]) / GLA_TAU
    lab_ref[...] = _log_sigmoid(_dot(z, wgb_ref[...].astype(BF16)) + bgb_ref[...]) / GLA_TAU

    row = lax.broadcasted_iota(jnp.int32, (chunk, chunk), 0)
    col = lax.broadcasted_iota(jnp.int32, (chunk, chunk), 1)
    lower = row >= col
    upper = row <= col
    tri_f = jnp.where(lower, 1.0, 0.0).astype(BF16)
    tri_b = jnp.where(upper, 1.0, 0.0).astype(BF16)
    q_scale = dk ** -0.5

    def intra(n, rows, la_ref, tri, mask, edge, qd_ref, kv_ref, dec_ref):
        b = _cumsum3(tri, la_ref[rows, :])
        b_edge = b[edge:edge + 1, :]
        q = q_ref[rows, :] * q_scale
        k = k_ref[rows, :]
        v = v_ref[rows, :].astype(BF16)
        qd = (q * jnp.exp(b)).astype(BF16)
        kd = (k * jnp.exp(-b)).astype(BF16)
        ke = (k * jnp.exp(b_edge - b)).astype(BF16)
        att = jnp.where(mask, _dot_nt(qd, kd), 0.0).astype(BF16)
        qd_ref[rows, :] = qd
        kv_ref[n] = _dot_tn(v, ke)
        dec_ref[n] = jnp.broadcast_to(jnp.exp(b_edge), dec_ref.shape[1:])
        return _dot(att, v)

    def phase1(n, carry):
        rows = pl.ds(pl.multiple_of(n * chunk, chunk), chunk)
        oacc_ref[rows, :] = (intra(n, rows, laf_ref, tri_f, lower, chunk - 1, qdf_ref, kvf_ref, decf_ref)
                             + intra(n, rows, lab_ref, tri_b, upper, 0, qdb_ref, kvb_ref, decb_ref))
        return carry

    lax.fori_loop(0, n_chunks, phase1, 0, unroll=2)

    def scan(order, kv_ref, dec_ref, s0):
        def body(j, s):
            n = order(j)
            kv = kv_ref[n]
            kv_ref[n] = s
            return s * dec_ref[n, 0:1, :] + kv

        return lax.fori_loop(0, n_chunks, body, s0)

    zero = jnp.zeros((dv, dk), F32)
    s_f = scan(lambda j: j, kvf_ref, decf_ref, s0f_ref[0, 0].T if has_state else zero)
    s_b = scan(lambda j: n_chunks - 1 - j, kvb_ref, decb_ref, s0b_ref[0, 0].T if has_state else zero)
    if not has_state:
        sf_ref[0, 0] = s_f.T
        sb_ref[0, 0] = s_b.T

    def phase3(n, carry):
        rows = pl.ds(pl.multiple_of(n * chunk, chunk), chunk)
        o = (oacc_ref[rows, :] + _dot_nt(qdf_ref[rows, :], kvf_ref[n].astype(BF16))
             + _dot_nt(qdb_ref[rows, :], kvb_ref[n].astype(BF16)))
        o = _rms(o) * ng_ref[...]
        o_ref[rows, :] = (o * _silu(r_ref[rows, :])).astype(o_ref.dtype)
        return carry

    lax.fori_loop(0, n_chunks, phase3, 0, unroll=2)


def _gla_branch(hmat, cols, wgf, bgf, wgb, bgb, norm_g, *, heads, dk, dv, row_off, n_seq, seq, state=None):
    col_q, col_k, col_v, col_r, col_z = cols
    s_off = row_off // seq
    has_state = state is not None
    in_specs = [
        pl.BlockSpec((seq, dk), lambda s, h: (s_off + s, col_q // dk + h)),
        pl.BlockSpec((seq, dk), lambda s, h: (s_off + s, col_k // dk + h)),
        pl.BlockSpec((seq, dv), lambda s, h: (s_off + s, col_v // dv + h)),
        pl.BlockSpec((seq, dv), lambda s, h: (s_off + s, col_r // dv + h)),
        pl.BlockSpec((seq, LANES), lambda s, h: (s_off + s, col_z // LANES)),
        pl.BlockSpec((LANES, dk), lambda s, h: (0, h)),
        pl.BlockSpec((1, dk), lambda s, h: (0, h)),
        pl.BlockSpec((LANES, dk), lambda s, h: (0, h)),
        pl.BlockSpec((1, dk), lambda s, h: (0, h)),
        pl.BlockSpec((1, dv), lambda s, h: (0, h)),
    ]
    args = [hmat] * 5 + [wgf, bgf.reshape(1, -1), wgb, bgb.reshape(1, -1), norm_g.reshape(1, -1)]
    o_spec = pl.BlockSpec((seq, dv), lambda s, h: (s, h))
    o_shape = jax.ShapeDtypeStruct((n_seq * seq, heads * dv), BF16)
    if has_state:
        st_spec = pl.BlockSpec((1, 1, dk, dv), lambda s, h: (s, h, 0, 0))
        in_specs += [st_spec, st_spec]
        args += [state[0], state[1]]
        out_specs, out_shape = o_spec, o_shape
    else:
        st_spec = pl.BlockSpec((1, 1, dk, dv), lambda s, h: (s, h, 0, 0))
        st_shape = jax.ShapeDtypeStruct((n_seq, heads, dk, dv), F32)
        out_specs, out_shape = [o_spec, st_spec, st_spec], [o_shape, st_shape, st_shape]
    chunk = min(GLA_CHUNK, seq)
    n_chunks = seq // chunk
    return pl.pallas_call(
        functools.partial(_gla_kernel, seq=seq, chunk=chunk, has_state=has_state),
        grid=(n_seq, heads),
        in_specs=in_specs,
        out_specs=out_specs,
        out_shape=out_shape,
        scratch_shapes=[pltpu.VMEM((seq, dk), F32), pltpu.VMEM((seq, dk), F32), pltpu.VMEM((seq, dv), F32),
                        pltpu.VMEM((n_chunks, dv, dk), F32), pltpu.VMEM((n_chunks, dv, dk), F32),
                        pltpu.VMEM((seq, dk), BF16), pltpu.VMEM((seq, dk), BF16),
                        pltpu.VMEM((n_chunks, 8, dk), F32), pltpu.VMEM((n_chunks, 8, dk), F32)],
        compiler_params=_params("parallel", "parallel"),
        name="gla_branch",
    )(*args)


def _mla_proj_kernel(cq_ref, ckv_ref, pe_ref, qg_ref, kvg_ref, wq_ref, wqr_ref, tq_ref, tk_ref,
                     q_ref, ckvn_ref, kpe_ref, *, hw):
    cqn = (_rms(cq_ref[...]) * qg_ref[...]).astype(BF16)
    a = _dot(cqn, wq_ref[...])
    ar = _dot(cqn, wqr_ref[...])
    cos = tq_ref[:, 0:hw]
    sin = tq_ref[:, hw:2 * hw]
    for h in range(a.shape[1] // hw):
        sl = slice(h * hw, (h + 1) * hw)
        q_ref[:, sl] = (a[:, sl] * cos + ar[:, sl] * sin).astype(q_ref.dtype)
    ckvn_ref[...] = _rms(ckv_ref[...]) * kvg_ref[...]
    v = pe_ref[...] * tk_ref[...]
    v = v + pltpu.roll(v, LANES // 2, axis=1)
    lane = lax.broadcasted_iota(jnp.int32, v.shape, 1)
    kpe_ref[...] = jnp.where(lane < LANES // 2, v, 0.0)


def _mla_proj(hmat, col_cq, col_ckv, col_pe, q_g, kv_g, wq, wqr, tq, tk, tm=512):
    m = hmat.shape[0]
    rq = q_g.shape[0]
    rkv = kv_g.shape[0]
    nq = wq.shape[1]
    hw = tq.shape[1] // 2
    tm = min(tm, m)
    return pl.pallas_call(
        functools.partial(_mla_proj_kernel, hw=hw),
        grid=(m // tm,),
        in_specs=[
            pl.BlockSpec((tm, rq), lambda i: (i, col_cq // rq)),
            pl.BlockSpec((tm, rkv), lambda i: (i, col_ckv // rkv)),
            pl.BlockSpec((tm, LANES), lambda i: (i, col_pe // LANES)),
            pl.BlockSpec((1, rq), lambda i: (0, 0)),
            pl.BlockSpec((1, rkv), lambda i: (0, 0)),
            pl.BlockSpec((rq, nq), lambda i: (0, 0)),
            pl.BlockSpec((rq, nq), lambda i: (0, 0)),
            pl.BlockSpec((tm, 2 * hw), lambda i: (i, 0)),
            pl.BlockSpec((tm, LANES), lambda i: (i, 0)),
        ],
        out_specs=[
            pl.BlockSpec((tm, nq), lambda i: (i, 0)),
            pl.BlockSpec((tm, rkv), lambda i: (i, 0)),
            pl.BlockSpec((tm, LANES), lambda i: (i, 0)),
        ],
        out_shape=[
            jax.ShapeDtypeStruct((m, nq), BF16),
            jax.ShapeDtypeStruct((m, rkv), F32),
            jax.ShapeDtypeStruct((m, LANES), F32),
        ],
        compiler_params=_params("parallel"),
        name="mla_proj",
    )(hmat, hmat, hmat, q_g.reshape(1, rq), kv_g.reshape(1, rkv), wq, wqr, tq, tk)


def _attn_kernel(*refs, heads, hw, dh, scale):
    q_ref, kv_ref, kpe_ref = refs[:3]
    o_ref, kcat_ref = refs[-2:]

    @pl.when(pl.program_id(1) == 0)
    def _():
        kp = kpe_ref[0].astype(BF16)
        for h in range(heads):
            kcat_ref[:, h * hw:h * hw + dh] = kv_ref[0, :, h * dh:(h + 1) * dh]
            kcat_ref[:, h * hw + dh:(h + 1) * hw] = kp

    v_off = heads * dh
    for h in range(heads):
        s = _dot_nt(q_ref[:, h * hw:(h + 1) * hw], kcat_ref[:, h * hw:(h + 1) * hw]) * scale
        p = jnp.exp(s - jnp.max(s, axis=-1, keepdims=True))
        l = jnp.sum(p, axis=-1, keepdims=True)
        o = _dot(p.astype(BF16), kv_ref[0, :, v_off + h * dh:v_off + (h + 1) * dh]) / l
        o_ref[:, h * dh:(h + 1) * dh] = o.astype(o_ref.dtype)


def _attention(q_all, kv, kpe, *, heads, dh, scale, row_off, n_seq, seq, tq=256):
    s_len = kv.shape[1]
    hw = q_all.shape[1] // heads
    tq = min(tq, seq)
    q_off = row_off // tq
    in_specs = [
        pl.BlockSpec((tq, heads * hw), lambda b, i: (q_off + b * (seq // tq) + i, 0)),
        pl.BlockSpec((1, s_len, 2 * heads * dh), lambda b, i: (b, 0, 0)),
        pl.BlockSpec((1, s_len, LANES), lambda b, i: (b, 0, 0)),
    ]
    args = [q_all, kv, kpe]
    return pl.pallas_call(
        functools.partial(_attn_kernel, heads=heads, hw=hw, dh=dh, scale=scale),
        grid=(n_seq, seq // tq),
        in_specs=in_specs,
        out_specs=pl.BlockSpec((tq, heads * dh), lambda b, i: (b * (seq // tq) + i, 0)),
        out_shape=jax.ShapeDtypeStruct((n_seq * seq, heads * dh), BF16),
        scratch_shapes=[pltpu.VMEM((s_len, heads * hw), BF16)],
        compiler_params=_params("parallel", "arbitrary"),
        name="mla_attention",
    )(*args)


def _split3(x):
    hi = x.astype(BF16)
    lo = (x - hi.astype(F32)).astype(BF16)
    return hi, lo


def _post_attn_kernel(m_ref, x_ref, w_ref, mod_ref, g1_ref, b1_ref, rw_ref, rb_ref,
                      x1_ref, h2_ref, ri_ref, rg_ref, cnt_ref, run_ref, tri_ref, *, alpha, n_exp):
    tm = x_ref.shape[0]
    step = pl.program_id(0)

    @pl.when(step == 0)
    def _():
        run_ref[...] = jnp.zeros_like(run_ref)
        row = lax.broadcasted_iota(jnp.int32, (tm, tm), 0)
        col = lax.broadcasted_iota(jnp.int32, (tm, tm), 1)
        tri_ref[...] = jnp.where(row > col, 1.0, 0.0).astype(BF16)

    y = _dot(m_ref[...], w_ref[...])
    x1 = _ln(alpha * x_ref[...] + mod_ref[0, 2:3, :] * y) * g1_ref[...] + b1_ref[...]
    x1_ref[...] = x1
    h2 = _ln(x1) * (1.0 + mod_ref[0, 4:5, :]) + mod_ref[0, 3:4, :]
    half = h2_ref.shape[1]
    hi = pltpu.bitcast(h2[:, :half].astype(BF16).astype(F32), jnp.uint32)
    lo = pltpu.bitcast(h2[:, half:].astype(BF16).astype(F32), jnp.uint32)
    h2_ref[...] = hi | (lo >> 16)

    h_hi, h_lo = _split3(h2)
    w_hi, w_lo = _split3(rw_ref[...])
    logits = _dot(h_hi, w_hi) + _dot(h_lo, w_hi) + _dot(h_hi, w_lo) + rb_ref[...]
    lane = lax.broadcasted_iota(jnp.int32, logits.shape, 1).astype(F32)
    neg = jnp.float32(-jnp.inf)
    logits = jnp.where(lane < n_exp, logits, neg)

    counts = jnp.zeros(logits.shape, F32)
    vals, idxs = [], []
    for _ in range(TOP_K):
        mx = jnp.max(logits, axis=-1, keepdims=True)
        idx = jnp.min(jnp.where(logits == mx, lane, float(LANES)), axis=-1, keepdims=True)
        hit = lane == idx
        counts = counts + jnp.where(hit, 1.0, 0.0)
        logits = jnp.where(hit, neg, logits)
        vals.append(mx)
        idxs.append(idx)

    es = [jnp.exp(v - vals[0]) for v in vals]
    denom = es[0]
    for e in es[1:]:
        denom = denom + e
    before = _dot(tri_ref[...], counts.astype(BF16)) + run_ref[...]
    out_i = jnp.zeros(logits.shape, jnp.int32)
    out_g = jnp.zeros(logits.shape, F32)
    for k in range(TOP_K):
        rank = jnp.sum(jnp.where(lane == idxs[k], before, 0.0), axis=-1, keepdims=True)
        out_i = jnp.where(lane == k, idxs[k].astype(jnp.int32), out_i)
        out_i = jnp.where(lane == TOP_K + k, rank.astype(jnp.int32), out_i)
        out_g = jnp.where(lane == k, es[k] / denom, out_g)
    ri_ref[...] = out_i[:, 0:2 * TOP_K]
    rg_ref[...] = out_g[:, 0:TOP_K]
    run_ref[...] = run_ref[...] + jnp.sum(counts, axis=0, keepdims=True)
    cnt_ref[...] = run_ref[...]


def _post_attn(merged, x, w_out, mod, ln_g, ln_b, router_w, router_b, n_prompt, t_sample, alpha, n_exp, tm=512):
    m, d = x.shape
    tm = min(tm, t_sample, n_prompt)
    row_spec = pl.BlockSpec((tm, d), lambda i: (i, 0))
    vec_spec = pl.BlockSpec((1, d), lambda i: (0, 0))
    return pl.pallas_call(
        functools.partial(_post_attn_kernel, alpha=alpha, n_exp=n_exp),
        grid=(m // tm,),
        in_specs=[
            row_spec, row_spec,
            pl.BlockSpec((d, d), lambda i: (0, 0)),
            pl.BlockSpec((1, 6, d), lambda i: (_mod_row(i, tm, n_prompt, t_sample), 0, 0)),
            vec_spec, vec_spec,
            pl.BlockSpec((d, LANES), lambda i: (0, 0)),
            pl.BlockSpec((1, LANES), lambda i: (0, 0)),
        ],
        out_specs=[
            row_spec,
            pl.BlockSpec((tm, d // 2), lambda i: (i, 0)),
            pl.BlockSpec((tm, 2 * TOP_K), lambda i: (i, 0)),
            pl.BlockSpec((tm, TOP_K), lambda i: (i, 0)),
            pl.BlockSpec((1, LANES), lambda i: (0, 0)),
        ],
        out_shape=[
            jax.ShapeDtypeStruct((m, d), F32),
            jax.ShapeDtypeStruct((m, d // 2), jnp.uint32),
            jax.ShapeDtypeStruct((m, 2 * TOP_K), jnp.int32),
            jax.ShapeDtypeStruct((m, TOP_K), F32),
            jax.ShapeDtypeStruct((1, LANES), F32),
        ],
        scratch_shapes=[pltpu.VMEM((1, LANES), F32), pltpu.VMEM((tm, tm), BF16)],
        compiler_params=_params("arbitrary"),
        name="post_attn_router",
    )(merged, x, w_out, mod, ln_g.reshape(1, d), ln_b.reshape(1, d), router_w, router_b)


def _wait_slot(buf_ref, sem, slot):
    pltpu.make_async_copy(buf_ref.at[slot], buf_ref.at[slot], sem.at[slot]).wait()


def _moe_expert_kernel(be_ref, bv_ref, idx_ref, nxt_ref, h_hbm, wg_ref, bg_ref, wu_ref, bu_ref, wd_ref, bd_ref,
                       o_ref, xbuf_ref, xb_ref, sem, *, n_f):
    i = pl.program_id(0)
    f = pl.program_id(1)
    n_rows, half = xbuf_ref.shape[1], xbuf_ref.shape[2]
    share = n_rows // n_f
    valid = bv_ref[i]
    slot = i % 2
    del be_ref

    def row_copy(ids_ref, s, r):
        return pltpu.make_async_copy(h_hbm.at[pl.ds(ids_ref[0, 0, r], 1), :], xbuf_ref.at[s, pl.ds(r, 1), :],
                                     sem.at[s])

    @pl.when(jnp.logical_and(jnp.logical_and(i == 0, f == 0), valid > 0))
    def _():
        def body(r, carry):
            row_copy(idx_ref, 0, r).start()
            return carry

        lax.fori_loop(0, n_rows, body, 0, unroll=8)

    @pl.when(f == 0)
    def _():
        o_ref[...] = jnp.broadcast_to(bd_ref[0, 0], o_ref.shape)

    rows_in_flight = jnp.where(i == 0, valid > 0, bv_ref[jnp.maximum(i - 1, 0)] > 0)

    @pl.when(jnp.logical_and(f == 0, rows_in_flight))
    def _():
        _wait_slot(xbuf_ref, sem, slot)
        u = xbuf_ref[slot]
        xb_ref[:, :half] = pltpu.bitcast(u & jnp.uint32(0xFFFF0000), F32).astype(BF16)
        xb_ref[:, half:] = pltpu.bitcast(u << 16, F32).astype(BF16)

    for sb in range(n_rows // MOE_SUB):
        rows = slice(sb * MOE_SUB, (sb + 1) * MOE_SUB)

        @pl.when(sb * MOE_SUB < valid)
        def _():
            if sb == 0:
                for j in range(share):
                    row_copy(nxt_ref, 1 - slot, f * share + j).start()
            x = xb_ref[rows, :]
            gt = jnp.minimum(_dot(x, wg_ref[0, 0].astype(BF16)) + bg_ref[0, 0], SWIGLU_LIMIT)
            up = jnp.clip(_dot(x, wu_ref[0, 0].astype(BF16)) + bu_ref[0, 0], -SWIGLU_LIMIT, SWIGLU_LIMIT)
            act = gt * jax.nn.sigmoid(SWIGLU_ALPHA * gt) * (up + 1.0)
            o_ref[rows, :] = o_ref[rows, :] + _dot(act.astype(BF16), wd_ref[0, 0].astype(BF16))


def _moe_experts(h2p, buf_tok, blk_e, blk_valid, layer, w_gate, b_gate, w_up, b_up, w_down, b_down, tf=256):
    cap = buf_tok.shape[0]
    half = h2p.shape[1]
    n_layers, n_exp, d, ff = w_gate.shape
    assert d == 2 * half
    tf = _tile(ff, tf)
    nf = ff // tf
    nblk = cap // MOE_ROWS
    assert MOE_ROWS % nf == 0
    ids = buf_tok.reshape(nblk, 1, MOE_ROWS)

    def f_idx(i, f, bv):
        return jnp.where(bv[i] > 0, f, nf - 1)

    return pl.pallas_call(
        functools.partial(_moe_expert_kernel, n_f=nf),
        grid_spec=pltpu.PrefetchScalarGridSpec(
            num_scalar_prefetch=2,
            grid=(nblk, nf),
            in_specs=[
                pl.BlockSpec((1, 1, MOE_ROWS), lambda i, f, be, bv: (i, 0, 0), memory_space=pltpu.SMEM),
                pl.BlockSpec((1, 1, MOE_ROWS), lambda i, f, be, bv: (jnp.minimum(i + 1, nblk - 1), 0, 0),
                             memory_space=pltpu.SMEM),
                pl.BlockSpec(memory_space=pl.ANY),
                pl.BlockSpec((1, 1, d, tf), lambda i, f, be, bv: (layer, be[i], 0, f_idx(i, f, bv))),
                pl.BlockSpec((1, 1, 1, tf), lambda i, f, be, bv: (layer, be[i], 0, f_idx(i, f, bv))),
                pl.BlockSpec((1, 1, d, tf), lambda i, f, be, bv: (layer, be[i], 0, f_idx(i, f, bv))),
                pl.BlockSpec((1, 1, 1, tf), lambda i, f, be, bv: (layer, be[i], 0, f_idx(i, f, bv))),
                pl.BlockSpec((1, 1, tf, d), lambda i, f, be, bv: (layer, be[i], f_idx(i, f, bv), 0)),
                pl.BlockSpec((1, 1, 1, d), lambda i, f, be, bv: (layer, be[i], 0, 0)),
            ],
            out_specs=pl.BlockSpec((MOE_ROWS, d), lambda i, f, be, bv: (i, 0)),
            scratch_shapes=[pltpu.VMEM((2, MOE_ROWS, half), jnp.uint32), pltpu.VMEM((MOE_ROWS, d), BF16),
                            pltpu.SemaphoreType.DMA((2,))],
        ),
        out_shape=jax.ShapeDtypeStruct((cap, d), F32),
        compiler_params=_params("arbitrary", "arbitrary"),
        name="moe_experts",
    )(blk_e, blk_valid, ids, ids, h2p, w_gate, b_gate.reshape(n_layers, n_exp, 1, ff), w_up, b_up.reshape(n_layers, n_exp, 1, ff),
      w_down, b_down.reshape(n_layers, n_exp, 1, d))


def _moe_combine_kernel(idx_ref, nxt_ref, yb_hbm, g_ref, x_ref, mod_ref, lg_ref, lb_ref, o_ref, buf_ref, sem, *,
                        alpha):
    tc = x_ref.shape[0]
    blk = pl.program_id(0)
    slot = blk % 2

    def issue(ids_ref, s):
        def body(r, carry):
            for k in range(TOP_K):
                pltpu.make_async_copy(yb_hbm.at[pl.ds(ids_ref[0, 0, r * TOP_K + k], 1), :],
                                      buf_ref.at[s, k, pl.ds(r, 1), :], sem.at[s]).start()
            return carry

        lax.fori_loop(0, tc, body, 0, unroll=2)

    @pl.when(blk == 0)
    def _():
        issue(idx_ref, 0)

    @pl.when(blk + 1 < pl.num_programs(0))
    def _():
        issue(nxt_ref, 1 - slot)

    _wait_slot(buf_ref, sem, slot)
    g = g_ref[...]
    y = g[:, 0:1] * buf_ref[slot, 0]
    for k in range(1, TOP_K):
        y = y + g[:, k:k + 1] * buf_ref[slot, k]
    x2 = _ln(alpha * x_ref[...] + mod_ref[0, 5:6, :] * y) * lg_ref[...] + lb_ref[...]
    o_ref[...] = x2


def _moe_combine(yb, dest, gates, x1, mod, ln_g, ln_b, n_prompt, t_sample, alpha, tc=128):
    m, d = x1.shape
    tc = min(tc, t_sample, n_prompt)
    nb = m // tc
    ids = dest.reshape(nb, 1, tc * TOP_K)
    return pl.pallas_call(
        functools.partial(_moe_combine_kernel, alpha=alpha),
        grid=(nb,),
        in_specs=[
            pl.BlockSpec((1, 1, tc * TOP_K), lambda i: (i, 0, 0), memory_space=pltpu.SMEM),
            pl.BlockSpec((1, 1, tc * TOP_K), lambda i: (jnp.minimum(i + 1, nb - 1), 0, 0), memory_space=pltpu.SMEM),
            pl.BlockSpec(memory_space=pl.ANY),
            pl.BlockSpec((tc, TOP_K), lambda i: (i, 0)),
            pl.BlockSpec((tc, d), lambda i: (i, 0)),
            pl.BlockSpec((1, 6, d), lambda i: (_mod_row(i, tc, n_prompt, t_sample), 0, 0)),
            pl.BlockSpec((1, d), lambda i: (0, 0)),
            pl.BlockSpec((1, d), lambda i: (0, 0)),
        ],
        out_specs=pl.BlockSpec((tc, d), lambda i: (i, 0)),
        out_shape=jax.ShapeDtypeStruct((m, d), F32),
        scratch_shapes=[pltpu.VMEM((2, TOP_K, tc, d), F32), pltpu.SemaphoreType.DMA((2,))],
        compiler_params=_params("arbitrary"),
        name="moe_combine",
    )(ids, ids, yb, gates, x1, mod, ln_g.reshape(1, d), ln_b.reshape(1, d))


def _moe_plan(route_i, counts, n_exp):
    m = route_i.shape[0]
    top_e = route_i[:, :TOP_K]
    rank = route_i[:, TOP_K:]
    counts = counts[0, :n_exp].astype(jnp.int32)
    padded = (counts + MOE_ROWS - 1) // MOE_ROWS * MOE_ROWS
    pend = jnp.cumsum(padded)
    pstart = pend - padded
    dest = pstart[top_e] + rank
    nblk = (m * TOP_K + n_exp * (MOE_ROWS - 1)) // MOE_ROWS + 1
    blk_row = jnp.arange(nblk, dtype=jnp.int32) * MOE_ROWS
    n_used = pend[-1] // MOE_ROWS
    blk_e = jnp.minimum(jnp.searchsorted(pend, blk_row, side="right"), n_exp - 1).astype(jnp.int32)
    blk_valid = jnp.clip(counts[blk_e] - (blk_row - pstart[blk_e]), 0, MOE_ROWS).astype(jnp.int32)
    used = jnp.arange(nblk, dtype=jnp.int32) < n_used
    blk_valid = jnp.where(used, blk_valid, 0)
    blk_e = jnp.where(used, blk_e, blk_e[jnp.maximum(n_used - 1, 0)])
    tok = jnp.broadcast_to(jnp.arange(m, dtype=jnp.int32)[:, None], (m, TOP_K))
    buf_tok = jnp.zeros((nblk * MOE_ROWS,), jnp.int32).at[dest.reshape(-1)].set(tok.reshape(-1))
    return dest.astype(jnp.int32), buf_tok, blk_e, blk_valid


def _rot_half_cols(w):
    w1, w2, w3, w4 = jnp.split(w, 4, axis=-1)
    return jnp.concatenate([-w2, w1, -w4, w3], axis=-1)


def _rope_tables(n_prompt, n_batch, t_sample, rope, dh):
    rows = t_sample // GRID_W
    row = jnp.repeat(jnp.arange(rows), GRID_W).astype(F32)
    col = jnp.tile(jnp.arange(GRID_W), rows).astype(F32)
    n_freq = rope // 4
    inv = ROPE_BASE ** (-jnp.arange(n_freq, dtype=F32) / n_freq)
    ar = row[:, None] * inv
    ac = col[:, None] * inv
    ang = jnp.concatenate([ar, ar, ac, ac], axis=-1)
    cos = jnp.concatenate([jnp.ones((n_prompt, rope), F32), jnp.tile(jnp.cos(ang), (n_batch, 1))], axis=0)
    sin = jnp.concatenate([jnp.zeros((n_prompt, rope), F32), jnp.tile(jnp.sin(ang), (n_batch, 1))], axis=0)
    n = cos.shape[0]
    hw = 2 * dh
    pad = jnp.zeros((n, hw - dh - rope), F32)
    tq = jnp.concatenate([jnp.ones((n, dh), F32), cos, pad, jnp.zeros((n, dh), F32), sin, pad], axis=-1)
    tk = jnp.concatenate([cos, sin], axis=-1)
    return tq, tk


def kernel(x_prompt, x_sample, c, cache_ckv, cache_kpe, state_gla_fwd, state_gla_bwd, c_ctx, w_mod, b_mod, w_in,
           conv_w_dw, conv_b_dw, conv_ln_g, conv_ln_b, conv_w_out, gla_w_gate_fwd, gla_b_gate_fwd,
           gla_w_gate_bwd, gla_b_gate_bwd, gla_norm_g, gla_w_out, mla_q_norm_g, mla_w_uq, mla_kv_norm_g,
           mla_w_uk, mla_w_uv, mla_w_out, w_out, ln1_g, ln1_b, router_w, router_b, exp_w_gate, exp_b_gate,
           exp_w_up, exp_b_up, exp_w_down, exp_b_down, ln2_g, ln2_b):
    bp, tp, d = x_prompt.shape
    bs, ts, _ = x_sample.shape
    depth = w_in.shape[0]
    n_prompt, n_sample = bp * tp, bs * ts
    n_tok = n_prompt + n_sample
    conv_dim = conv_w_dw.shape[2]
    gla_heads, gla_dk, gla_dv = state_gla_fwd.shape[2:]
    gla_qk, gla_v = gla_heads * gla_dk, gla_heads * gla_dv
    gate_rank = gla_w_gate_fwd.shape[1]
    q_rank = mla_w_uq.shape[1]
    kv_rank = mla_w_uk.shape[1]
    rope = cache_kpe.shape[3]
    dh = mla_w_uk.shape[2] // MLA_HEADS
    n_exp = router_w.shape[2]
    past = cache_ckv.shape[2]
    alpha = (2 * depth) ** 0.25
    assert 2 * dh == 2 * LANES and rope == LANES // 2 and 2 * gate_rank <= LANES

    sizes = (conv_dim, conv_dim, gla_qk, gla_qk, gla_v, gla_v, gate_rank, gate_rank, q_rank, kv_rank, rope, d, d, d)
    offs = [0]
    for s in sizes:
        offs.append(offs[-1] + s)
    o_zf, o_cq, o_kpe, o_ma = offs[6], offs[8], offs[10], offs[11]
    c_ca, c_cg, c_gq, c_gk, c_gv, c_gr = offs[0], offs[1], offs[2], offs[3], offs[4], offs[5]
    c_cq = o_zf
    c_ckv = c_cq + q_rank
    c_ma = c_ckv + kv_rank
    c_pe = c_ma + 3 * d
    c_z = c_pe + LANES
    n_in = c_z + LANES

    cc = jnp.concatenate([c_ctx[None, :], c], axis=0)
    mp = (cc.shape[0] + 7) // 8 * 8
    cc = jnp.pad(cc, ((0, mp - cc.shape[0]), (0, 0)))
    mod_all = _modulation(cc, w_mod, b_mod).reshape(depth, mp, 6, d)

    tq_tab, tk_tab = _rope_tables(n_prompt, bs, ts, rope, dh)
    x = jnp.concatenate([x_prompt.reshape(n_prompt, d), x_sample.reshape(n_sample, d)], axis=0)
    scale = (dh + rope) ** -0.5

    ckv_l, kpe_l, sf_l, sb_l = [], [], [], []
    for l in range(depth):
        mod = mod_all[l]
        wl = w_in[l]
        kpe_cols = wl[:, o_kpe:o_kpe + rope]
        w_in_r = jnp.concatenate(
            [wl[:, :o_zf], wl[:, o_cq:o_kpe], wl[:, o_ma:], kpe_cols, _rot_half_cols(kpe_cols),
             wl[:, o_zf:o_cq], jnp.zeros((d, LANES - 2 * gate_rank), F32)], axis=1).astype(BF16)
        assert w_in_r.shape[1] == n_in
        zpad = jnp.zeros((LANES - 2 * gate_rank, gla_qk), F32)
        wgf = jnp.concatenate([gla_w_gate_fwd[l], jnp.zeros((gate_rank, gla_qk), F32), zpad], axis=0)
        wgb = jnp.concatenate([jnp.zeros((gate_rank, gla_qk), F32), gla_w_gate_bwd[l], zpad], axis=0)
        wq3 = mla_w_uq[l].reshape(q_rank, MLA_HEADS, dh + rope)
        zq = jnp.zeros((q_rank, MLA_HEADS, dh - rope), F32)
        wq = jnp.concatenate([wq3, zq], axis=-1).reshape(q_rank, -1).astype(BF16)
        wqr = jnp.concatenate([jnp.zeros((q_rank, MLA_HEADS, dh), F32), _rot_half_cols(wq3[..., dh:]), zq],
                              axis=-1).reshape(q_rank, -1).astype(BF16)
        wkv = jnp.concatenate([mla_w_uk[l], mla_w_uv[l]], axis=1).astype(BF16)
        rw = jnp.pad(router_w[l], ((0, 0), (0, LANES - n_exp)))
        rb = jnp.pad(router_b[l], (0, LANES - n_exp)).reshape(1, LANES)

        hmat = _gemm_in(x, mod, w_in_r, n_prompt, ts)

        conv_args = (hmat, c_ca, c_cg, conv_w_dw[l], conv_b_dw[l], conv_ln_g[l], conv_ln_b[l])
        ya = (_conv_branch(*conv_args, row_off=0, n_seq=bp, seq=tp),
              _conv_branch(*conv_args, row_off=n_prompt, n_seq=bs, seq=ts))

        gla_args = (hmat, (c_gq, c_gk, c_gv, c_gr, c_z), wgf, gla_b_gate_fwd[l], wgb, gla_b_gate_bwd[l],
                    gla_norm_g[l])
        gla_kw = dict(heads=gla_heads, dk=gla_dk, dv=gla_dv)
        og_p, s_f, s_b = _gla_branch(*gla_args, row_off=0, n_seq=bp, seq=tp, **gla_kw)
        og_s = _gla_branch(*gla_args, row_off=n_prompt, n_seq=bs, seq=ts, **gla_kw,
                           state=(state_gla_fwd[:, l], state_gla_bwd[:, l]))

        q_all, ckv, kpe = _mla_proj(hmat, c_cq, c_ckv, c_pe, mla_q_norm_g[l], mla_kv_norm_g[l], wq, wqr,
                                    tq_tab, tk_tab)
        kv_p = _gemm(ckv, wkv, rows=n_prompt, name="mla_kv").reshape(bp, tp, -1)
        keys_s = jnp.concatenate([cache_ckv[:, l], ckv[n_prompt:].reshape(bs, ts, kv_rank)], axis=1)
        kv_s = _gemm(keys_s.reshape(bs * (past + ts), kv_rank), wkv, name="mla_kv")
        kv_s = kv_s.reshape(bs, past + ts, -1)
        kpe_p = kpe[:n_prompt].reshape(bp, tp, LANES)
        kpe_s = jnp.concatenate([jnp.pad(cache_kpe[:, l], ((0, 0), (0, 0), (0, LANES - rope))),
                                 kpe[n_prompt:].reshape(bs, ts, LANES)], axis=1)
        att_kw = dict(heads=MLA_HEADS, dh=dh, scale=scale)
        att = (_attention(q_all, kv_p, kpe_p, row_off=0, n_seq=bp, seq=tp, **att_kw),
               _attention(q_all, kv_s, kpe_s, row_off=n_prompt, n_seq=bs, seq=ts, **att_kw))
        merged = _merge([ya, (og_p, og_s), att],
                        [conv_w_out[l].astype(BF16), gla_w_out[l].astype(BF16), mla_w_out[l].astype(BF16)],
                        hmat, c_ma)

        x1, h2, route_i, route_g, counts = _post_attn(merged, x, w_out[l].astype(BF16), mod, ln1_g[l], ln1_b[l],
                                                      rw, rb, n_prompt, ts, alpha, n_exp)

        dest, buf_tok, blk_e, blk_valid = _moe_plan(route_i, counts, n_exp)
        yb = _moe_experts(h2, buf_tok, blk_e, blk_valid, l, exp_w_gate, exp_b_gate, exp_w_up, exp_b_up, exp_w_down,
                          exp_b_down)
        x = _moe_combine(yb, dest, route_g, x1, mod, ln2_g[l], ln2_b[l], n_prompt, ts, alpha)

        ckv_l.append(ckv[:n_prompt].reshape(bp, tp, kv_rank))
        kpe_l.append(kpe_p[..., :rope])
        sf_l.append(s_f)
        sb_l.append(s_b)

    y_prompt = x[:n_prompt].reshape(bp, tp, d)
    y_sample = x[n_prompt:].reshape(bs, ts, d)
    return (y_prompt, y_sample, jnp.stack(ckv_l, axis=1), jnp.stack(kpe_l, axis=1),
            jnp.stack(sf_l, axis=1), jnp.stack(sb_l, axis=1))
```

```python
import functools

import jax
import jax.numpy as jnp
from jax import lax
from jax.experimental import pallas as pl
from jax.experimental.pallas import tpu as pltpu

F32 = jnp.float32
BF16 = jnp.bfloat16

LANES = 128
VMEM_LIMIT_BYTES = 56 * 1024 * 1024
GEMM_IN_VMEM_LIMIT_BYTES = 60 * 1024 * 1024

GRID_W = 64
GLA_TAU = 16.0
GLA_CHUNK = 64
MLA_HEADS = 8
ROPE_BASE = 10000.0
TOP_K = 4
SWIGLU_LIMIT = 7.0
SWIGLU_ALPHA = 1.702
LN_EPS = 1e-5
RMS_EPS = 1e-6

CONV_PAD = 16
MOE_ROWS = 1024
MOE_SUB = 512


def _params(*sem, vmem=VMEM_LIMIT_BYTES):
    return pltpu.CompilerParams(dimension_semantics=sem, vmem_limit_bytes=vmem)


def _tile(n, pref):
    if n <= pref:
        return n
    t = (pref // LANES) * LANES
    while t > LANES and n % t:
        t -= LANES
    assert n % t == 0, (n, pref)
    return t


def _ln(x):
    mu = jnp.mean(x, axis=-1, keepdims=True)
    xc = x - mu
    return xc * lax.rsqrt(jnp.mean(xc * xc, axis=-1, keepdims=True) + LN_EPS)


def _rms(x):
    return x * lax.rsqrt(jnp.mean(x * x, axis=-1, keepdims=True) + RMS_EPS)


def _silu(x):
    return x * jax.nn.sigmoid(x)


def _dot(a, b):
    return jnp.dot(a, b, preferred_element_type=F32)


def _dot_nt(a, b):
    return lax.dot_general(a, b, (((1,), (1,)), ((), ())), preferred_element_type=F32)


def _dot_tn(a, b):
    return lax.dot_general(a, b, (((0,), (0,)), ((), ())), preferred_element_type=F32)


def _mod_kernel(c_ref, w_ref, b_ref, o_ref):
    s = _silu(c_ref[...])
    o_ref[0] = _dot(s.astype(BF16), w_ref[0].astype(BF16)) + b_ref[0]


def _modulation(cc, w_mod, b_mod):
    n_layers, d, d6 = w_mod.shape
    mp = cc.shape[0]
    tn = _tile(d6, 1024)
    return pl.pallas_call(
        _mod_kernel,
        grid=(n_layers, d6 // tn),
        in_specs=[
            pl.BlockSpec((mp, d), lambda l, j: (0, 0)),
            pl.BlockSpec((1, d, tn), lambda l, j: (l, 0, j)),
            pl.BlockSpec((1, 1, tn), lambda l, j: (l, 0, j)),
        ],
        out_specs=pl.BlockSpec((1, mp, tn), lambda l, j: (l, 0, j)),
        out_shape=jax.ShapeDtypeStruct((n_layers, mp, d6), F32),
        compiler_params=_params("parallel", "parallel"),
        name="modulation",
    )(cc, w_mod, b_mod.reshape(n_layers, 1, d6))


def _mod_row(i, tm, n_prompt, t_sample):
    r = i * tm
    return jnp.where(r < n_prompt, 0, 1 + (r - n_prompt) // t_sample)


def _gemm_in_kernel(x_ref, mod_ref, w_ref, o_ref, h_ref):
    @pl.when(pl.program_id(1) == 0)
    def _():
        y = _ln(x_ref[...])
        h_ref[...] = (y * (1.0 + mod_ref[0, 1:2, :]) + mod_ref[0, 0:1, :]).astype(BF16)

    o_ref[...] = _dot(h_ref[...], w_ref[...])


def _gemm_in(x, mod, w, n_prompt, t_sample, tm=1024, tn=1792):
    m, d = x.shape
    n = w.shape[1]
    tm = min(tm, t_sample, n_prompt)
    tn = _tile(n, tn)
    assert m % tm == 0 and n_prompt % tm == 0 and t_sample % tm == 0
    return pl.pallas_call(
        _gemm_in_kernel,
        grid=(m // tm, n // tn),
        in_specs=[
            pl.BlockSpec((tm, d), lambda i, j: (i, 0)),
            pl.BlockSpec((1, 6, d), lambda i, j: (_mod_row(i, tm, n_prompt, t_sample), 0, 0)),
            pl.BlockSpec((d, tn), lambda i, j: (0, j)),
        ],
        out_specs=pl.BlockSpec((tm, tn), lambda i, j: (i, j)),
        out_shape=jax.ShapeDtypeStruct((m, n), F32),
        scratch_shapes=[pltpu.VMEM((tm, d), BF16)],
        compiler_params=_params("parallel", "arbitrary", vmem=GEMM_IN_VMEM_LIMIT_BYTES),
        name="gemm_in",
    )(x, mod, w)


def _gemm_kernel(x_ref, w_ref, o_ref):
    o_ref[...] = _dot(x_ref[...].astype(BF16), w_ref[...]).astype(o_ref.dtype)


def _gemm(x, w, *, rows=None, tm=512, tn=1024, name="gemm"):
    k, n = w.shape
    m = x.shape[0] if rows is None else rows
    tm = min(tm, m)
    tn = _tile(n, tn)
    assert m % tm == 0 and x.shape[1] == k
    return pl.pallas_call(
        _gemm_kernel,
        grid=(m // tm, n // tn),
        in_specs=[pl.BlockSpec((tm, k), lambda i, j: (i, 0)), pl.BlockSpec((k, tn), lambda i, j: (0, j))],
        out_specs=pl.BlockSpec((tm, tn), lambda i, j: (i, j)),
        out_shape=jax.ShapeDtypeStruct((m, n), BF16),
        compiler_params=_params("parallel", "parallel"),
        name=name,
    )(x, w)


def _merge_kernel(*refs, n_br, nb_first):
    x_refs = refs[:2 * n_br]
    w_refs = refs[2 * n_br:3 * n_br]
    g_refs = refs[3 * n_br:4 * n_br]
    o_ref = refs[4 * n_br]

    def compute(group):
        acc = None
        for b in range(n_br):
            term = jax.nn.sigmoid(g_refs[b][...]) * _dot(x_refs[2 * b + group][...], w_refs[b][...])
            acc = term if acc is None else acc + term
        o_ref[...] = acc.astype(o_ref.dtype)

    i = pl.program_id(0)
    pl.when(i < nb_first)(lambda: compute(0))
    pl.when(i >= nb_first)(lambda: compute(1))


def _merge(ys, ws, gate, gate_col, tm=512, tn=1024):
    n_br = len(ys)
    n = ws[0].shape[1]
    m = ys[0][0].shape[0] + ys[0][1].shape[0]
    tm = min(tm, ys[0][0].shape[0], ys[0][1].shape[0])
    tn = _tile(n, tn)
    nb_first = ys[0][0].shape[0] // tm
    assert gate_col % tn == 0 and n % tn == 0 and all(y[0].shape[0] % tm == 0 and y[1].shape[0] % tm == 0 for y in ys)
    in_specs, args = [], []
    for y, w in zip(ys, ws):
        assert w.shape == (y[0].shape[1], n) and y[1].shape[1] == w.shape[0]
        in_specs += [pl.BlockSpec((tm, w.shape[0]), lambda i, j: (jnp.minimum(i, nb_first - 1), 0)),
                     pl.BlockSpec((tm, w.shape[0]), lambda i, j: (jnp.maximum(i - nb_first, 0), 0))]
        args += [y[0], y[1]]
    in_specs += [pl.BlockSpec((w.shape[0], tn), lambda i, j: (0, j)) for w in ws]
    args += list(ws)
    for b in range(n_br):
        goff = (gate_col + b * n) // tn
        in_specs.append(pl.BlockSpec((tm, tn), lambda i, j, goff=goff: (i, goff + j)))
        args.append(gate)
    return pl.pallas_call(
        functools.partial(_merge_kernel, n_br=n_br, nb_first=nb_first),
        grid=(m // tm, n // tn),
        in_specs=in_specs,
        out_specs=pl.BlockSpec((tm, tn), lambda i, j: (i, j)),
        out_shape=jax.ShapeDtypeStruct((m, n), BF16),
        compiler_params=_params("parallel", "parallel"),
        name="merge_out",
    )(*args)


def _conv_kernel(a_ref, g_ref, wdw_ref, bdw_ref, lng_ref, lnb_ref, o_ref, upad_ref, yc_ref, *, seq, rb, rc, width):
    c = a_ref.shape[1]
    r = pl.program_id(1)
    fill_rows = min(seq, 64)

    @pl.when(r == 0)
    def _():
        zeros = jnp.zeros((CONV_PAD, c), F32)
        upad_ref[0:CONV_PAD, :] = zeros
        upad_ref[CONV_PAD + seq:CONV_PAD + seq + CONV_PAD, :] = zeros

        def fill(i, carry):
            src = pl.ds(pl.multiple_of(i * fill_rows, 8), fill_rows)
            dst = pl.ds(pl.multiple_of(CONV_PAD + i * fill_rows, 8), fill_rows)
            upad_ref[dst, :] = a_ref[src, :] * jax.nn.sigmoid(g_ref[src, :])
            return carry

        lax.fori_loop(0, seq // fill_rows, fill, 0)

    lead = CONV_PAD - width // 2
    n_shift = 8
    n_al = (lead + width - 1) // n_shift + 1
    win = rc + n_shift * n_al

    def row_chunk(ci, carry):
        base = pl.multiple_of(r * rb + ci * rc, 8)
        for cj in range(c // LANES):
            lanes = slice(cj * LANES, (cj + 1) * LANES)
            w_all = upad_ref[pl.ds(base, win), lanes]
            acc = jnp.zeros((rc, LANES), F32)
            for b in range(n_shift):
                taps = [(a8, n_shift * a8 + b - lead) for a8 in range(n_al)]
                taps = [(a8, k) for a8, k in taps if 0 <= k < width]
                if not taps:
                    continue
                w_b = w_all[b:b + rc + n_shift * (n_al - 1)]
                part = None
                for a8, k in taps:
                    term = wdw_ref[k:k + 1, lanes] * w_b[n_shift * a8:n_shift * a8 + rc]
                    part = term if part is None else part + term
                acc = acc + part
            yc_ref[pl.ds(pl.multiple_of(ci * rc, 8), rc), lanes] = acc + bdw_ref[:, lanes]
        return carry

    lax.fori_loop(0, rb // rc, row_chunk, 0)
    y = _ln(yc_ref[...]) * lng_ref[...] + lnb_ref[...]
    o_ref[...] = _silu(y).astype(o_ref.dtype)


def _conv_branch(hmat, col_a, col_g, w_dw, b_dw, ln_g, ln_b, *, row_off, n_seq, seq):
    width, c = w_dw.shape
    rb = min(seq, 256)
    rc = min(rb, 64)
    assert width // 2 <= CONV_PAD and row_off % seq == 0 and col_a % c == 0 and col_g % c == 0
    s_off = row_off // seq
    in_specs = [
        pl.BlockSpec((seq, c), lambda s, r: (s_off + s, col_a // c)),
        pl.BlockSpec((seq, c), lambda s, r: (s_off + s, col_g // c)),
        pl.BlockSpec((width, c), lambda s, r: (0, 0)),
        pl.BlockSpec((1, c), lambda s, r: (0, 0)),
        pl.BlockSpec((1, c), lambda s, r: (0, 0)),
        pl.BlockSpec((1, c), lambda s, r: (0, 0)),
    ]
    args = [hmat, hmat, w_dw, b_dw.reshape(1, c), ln_g.reshape(1, c), ln_b.reshape(1, c)]
    return pl.pallas_call(
        functools.partial(_conv_kernel, seq=seq, rb=rb, rc=rc, width=width),
        grid=(n_seq, seq // rb),
        in_specs=in_specs,
        out_specs=pl.BlockSpec((rb, c), lambda s, r: (s * (seq // rb) + r, 0)),
        out_shape=jax.ShapeDtypeStruct((n_seq * seq, c), BF16),
        scratch_shapes=[pltpu.VMEM((seq + 2 * CONV_PAD, c), F32), pltpu.VMEM((rb, c), F32)],
        compiler_params=_params("parallel", "arbitrary"),
        name="conv_branch",
    )(*args)


def _cumsum3(tri, la):
    hi = la.astype(BF16)
    r1 = la - hi.astype(F32)
    mid = r1.astype(BF16)
    lo = (r1 - mid.astype(F32)).astype(BF16)
    return _dot(tri, hi) + _dot(tri, mid) + _dot(tri, lo)


def _log_sigmoid(x):
    return jnp.minimum(x, 0.0) - jnp.log(1.0 + jnp.exp(-jnp.abs(x)))


def _gla_kernel(*refs, seq, chunk, has_state):
    it = iter(refs)
    q_ref, k_ref, v_ref, r_ref, z_ref = (next(it) for _ in range(5))
    wgf_ref, bgf_ref, wgb_ref, bgb_ref, ng_ref = (next(it) for _ in range(5))
    if has_state:
        s0f_ref, s0b_ref = next(it), next(it)
        o_ref = next(it)
        sf_ref = sb_ref = None
    else:
        o_ref, sf_ref, sb_ref = next(it), next(it), next(it)
    laf_ref, lab_ref, oacc_ref, kvf_ref, kvb_ref, qdf_ref, qdb_ref, decf_ref, decb_ref = (next(it) for _ in range(9))

    dk = q_ref.shape[1]
    dv = v_ref.shape[1]
    n_chunks = seq // chunk
    z = z_ref[...].astype(BF16)
    laf_ref[...] = _log_sigmoid(_dot(z, wgf_ref[...].astype(BF16)) + bgf_ref[...]) / GLA_TAU
    lab_ref[...] = _log_sigmoid(_dot(z, wgb_ref[...].astype(BF16)) + bgb_ref[...]) / GLA_TAU

    row = lax.broadcasted_iota(jnp.int32, (chunk, chunk), 0)
    col = lax.broadcasted_iota(jnp.int32, (chunk, chunk), 1)
    lower = row >= col
    upper = row <= col
    tri_f = jnp.where(lower, 1.0, 0.0).astype(BF16)
    tri_b = jnp.where(upper, 1.0, 0.0).astype(BF16)
    q_scale = dk ** -0.5

    def intra(n, rows, la_ref, tri, mask, edge, qd_ref, kv_ref, dec_ref):
        b = _cumsum3(tri, la_ref[rows, :])
        b_edge = b[edge:edge + 1, :]
        q = q_ref[rows, :] * q_scale
        k = k_ref[rows, :]
        v = v_ref[rows, :].astype(BF16)
        qd = (q * jnp.exp(b)).astype(BF16)
        kd = (k * jnp.exp(-b)).astype(BF16)
        ke = (k * jnp.exp(b_edge - b)).astype(BF16)
        att = jnp.where(mask, _dot_nt(qd, kd), 0.0).astype(BF16)
        qd_ref[rows, :] = qd
        kv_ref[n] = _dot_tn(v, ke)
        dec_ref[n] = jnp.broadcast_to(jnp.exp(b_edge), dec_ref.shape[1:])
        return _dot(att, v)

    def phase1(n, carry):
        rows = pl.ds(pl.multiple_of(n * chunk, chunk), chunk)
        oacc_ref[rows, :] = (intra(n, rows, laf_ref, tri_f, lower, chunk - 1, qdf_ref, kvf_ref, decf_ref)
                             + intra(n, rows, lab_ref, tri_b, upper, 0, qdb_ref, kvb_ref, decb_ref))
        return carry

    lax.fori_loop(0, n_chunks, phase1, 0, unroll=2)

    def scan(order, kv_ref, dec_ref, s0):
        def body(j, s):
            n = order(j)
            kv = kv_ref[n]
            kv_ref[n] = s
            return s * dec_ref[n, 0:1, :] + kv

        return lax.fori_loop(0, n_chunks, body, s0)

    zero = jnp.zeros((dv, dk), F32)
    s_f = scan(lambda j: j, kvf_ref, decf_ref, s0f_ref[0, 0].T if has_state else zero)
    s_b = scan(lambda j: n_chunks - 1 - j, kvb_ref, decb_ref, s0b_ref[0, 0].T if has_state else zero)
    if not has_state:
        sf_ref[0, 0] = s_f.T
        sb_ref[0, 0] = s_b.T

    def phase3(n, carry):
        rows = pl.ds(pl.multiple_of(n * chunk, chunk), chunk)
        o = (oacc_ref[rows, :] + _dot_nt(qdf_ref[rows, :], kvf_ref[n].astype(BF16))
             + _dot_nt(qdb_ref[rows, :], kvb_ref[n].astype(BF16)))
        o = _rms(o) * ng_ref[...]
        o_ref[rows, :] = (o * _silu(r_ref[rows, :])).astype(o_ref.dtype)
        return carry

    lax.fori_loop(0, n_chunks, phase3, 0, unroll=2)


def _gla_branch(hmat, cols, wgf, bgf, wgb, bgb, norm_g, *, heads, dk, dv, row_off, n_seq, seq, state=None):
    col_q, col_k, col_v, col_r, col_z = cols
    s_off = row_off // seq
    has_state = state is not None
    in_specs = [
        pl.BlockSpec((seq, dk), lambda s, h: (s_off + s, col_q // dk + h)),
        pl.BlockSpec((seq, dk), lambda s, h: (s_off + s, col_k // dk + h)),
        pl.BlockSpec((seq, dv), lambda s, h: (s_off + s, col_v // dv + h)),
        pl.BlockSpec((seq, dv), lambda s, h: (s_off + s, col_r // dv + h)),
        pl.BlockSpec((seq, LANES), lambda s, h: (s_off + s, col_z // LANES)),
        pl.BlockSpec((LANES, dk), lambda s, h: (0, h)),
        pl.BlockSpec((1, dk), lambda s, h: (0, h)),
        pl.BlockSpec((LANES, dk), lambda s, h: (0, h)),
        pl.BlockSpec((1, dk), lambda s, h: (0, h)),
        pl.BlockSpec((1, dv), lambda s, h: (0, h)),
    ]
    args = [hmat] * 5 + [wgf, bgf.reshape(1, -1), wgb, bgb.reshape(1, -1), norm_g.reshape(1, -1)]
    o_spec = pl.BlockSpec((seq, dv), lambda s, h: (s, h))
    o_shape = jax.ShapeDtypeStruct((n_seq * seq, heads * dv), BF16)
    if has_state:
        st_spec = pl.BlockSpec((1, 1, dk, dv), lambda s, h: (s, h, 0, 0))
        in_specs += [st_spec, st_spec]
        args += [state[0], state[1]]
        out_specs, out_shape = o_spec, o_shape
    else:
        st_spec = pl.BlockSpec((1, 1, dk, dv), lambda s, h: (s, h, 0, 0))
        st_shape = jax.ShapeDtypeStruct((n_seq, heads, dk, dv), F32)
        out_specs, out_shape = [o_spec, st_spec, st_spec], [o_shape, st_shape, st_shape]
    chunk = min(GLA_CHUNK, seq)
    n_chunks = seq // chunk
    return pl.pallas_call(
        functools.partial(_gla_kernel, seq=seq, chunk=chunk, has_state=has_state),
        grid=(n_seq, heads),
        in_specs=in_specs,
        out_specs=out_specs,
        out_shape=out_shape,
        scratch_shapes=[pltpu.VMEM((seq, dk), F32), pltpu.VMEM((seq, dk), F32), pltpu.VMEM((seq, dv), F32),
                        pltpu.VMEM((n_chunks, dv, dk), F32), pltpu.VMEM((n_chunks, dv, dk), F32),
                        pltpu.VMEM((seq, dk), BF16), pltpu.VMEM((seq, dk), BF16),
                        pltpu.VMEM((n_chunks, 8, dk), F32), pltpu.VMEM((n_chunks, 8, dk), F32)],
        compiler_params=_params("parallel", "parallel"),
        name="gla_branch",
    )(*args)


def _mla_proj_kernel(cq_ref, ckv_ref, pe_ref, qg_ref, kvg_ref, wq_ref, wqr_ref, tq_ref, tk_ref,
                     q_ref, ckvn_ref, kpe_ref, *, hw):
    cqn = (_rms(cq_ref[...]) * qg_ref[...]).astype(BF16)
    a = _dot(cqn, wq_ref[...])
    ar = _dot(cqn, wqr_ref[...])
    cos = tq_ref[:, 0:hw]
    sin = tq_ref[:, hw:2 * hw]
    for h in range(a.shape[1] // hw):
        sl = slice(h * hw, (h + 1) * hw)
        q_ref[:, sl] = (a[:, sl] * cos + ar[:, sl] * sin).astype(q_ref.dtype)
    ckvn_ref[...] = _rms(ckv_ref[...]) * kvg_ref[...]
    v = pe_ref[...] * tk_ref[...]
    v = v + pltpu.roll(v, LANES // 2, axis=1)
    lane = lax.broadcasted_iota(jnp.int32, v.shape, 1)
    kpe_ref[...] = jnp.where(lane < LANES // 2, v, 0.0)


def _mla_proj(hmat, col_cq, col_ckv, col_pe, q_g, kv_g, wq, wqr, tq, tk, tm=512):
    m = hmat.shape[0]
    rq = q_g.shape[0]
    rkv = kv_g.shape[0]
    nq = wq.shape[1]
    hw = tq.shape[1] // 2
    tm = min(tm, m)
    return pl.pallas_call(
        functools.partial(_mla_proj_kernel, hw=hw),
        grid=(m // tm,),
        in_specs=[
            pl.BlockSpec((tm, rq), lambda i: (i, col_cq // rq)),
            pl.BlockSpec((tm, rkv), lambda i: (i, col_ckv // rkv)),
            pl.BlockSpec((tm, LANES), lambda i: (i, col_pe // LANES)),
            pl.BlockSpec((1, rq), lambda i: (0, 0)),
            pl.BlockSpec((1, rkv), lambda i: (0, 0)),
            pl.BlockSpec((rq, nq), lambda i: (0, 0)),
            pl.BlockSpec((rq, nq), lambda i: (0, 0)),
            pl.BlockSpec((tm, 2 * hw), lambda i: (i, 0)),
            pl.BlockSpec((tm, LANES), lambda i: (i, 0)),
        ],
        out_specs=[
            pl.BlockSpec((tm, nq), lambda i: (i, 0)),
            pl.BlockSpec((tm, rkv), lambda i: (i, 0)),
            pl.BlockSpec((tm, LANES), lambda i: (i, 0)),
        ],
        out_shape=[
            jax.ShapeDtypeStruct((m, nq), BF16),
            jax.ShapeDtypeStruct((m, rkv), F32),
            jax.ShapeDtypeStruct((m, LANES), F32),
        ],
        compiler_params=_params("parallel"),
        name="mla_proj",
    )(hmat, hmat, hmat, q_g.reshape(1, rq), kv_g.reshape(1, rkv), wq, wqr, tq, tk)


def _attn_kernel(*refs, heads, hw, dh, scale):
    q_ref, kv_ref, kpe_ref = refs[:3]
    o_ref, kcat_ref = refs[-2:]

    @pl.when(pl.program_id(1) == 0)
    def _():
        kp = kpe_ref[0].astype(BF16)
        for h in range(heads):
            kcat_ref[:, h * hw:h * hw + dh] = kv_ref[0, :, h * dh:(h + 1) * dh]
            kcat_ref[:, h * hw + dh:(h + 1) * hw] = kp

    v_off = heads * dh
    for h in range(heads):
        s = _dot_nt(q_ref[:, h * hw:(h + 1) * hw], kcat_ref[:, h * hw:(h + 1) * hw]) * scale
        p = jnp.exp(s - jnp.max(s, axis=-1, keepdims=True))
        l = jnp.sum(p, axis=-1, keepdims=True)
        o = _dot(p.astype(BF16), kv_ref[0, :, v_off + h * dh:v_off + (h + 1) * dh]) / l
        o_ref[:, h * dh:(h + 1) * dh] = o.astype(o_ref.dtype)


def _attention(q_all, kv, kpe, *, heads, dh, scale, row_off, n_seq, seq, tq=256):
    s_len = kv.shape[1]
    hw = q_all.shape[1] // heads
    tq = min(tq, seq)
    q_off = row_off // tq
    in_specs = [
        pl.BlockSpec((tq, heads * hw), lambda b, i: (q_off + b * (seq // tq) + i, 0)),
        pl.BlockSpec((1, s_len, 2 * heads * dh), lambda b, i: (b, 0, 0)),
        pl.BlockSpec((1, s_len, LANES), lambda b, i: (b, 0, 0)),
    ]
    args = [q_all, kv, kpe]
    return pl.pallas_call(
        functools.partial(_attn_kernel, heads=heads, hw=hw, dh=dh, scale=scale),
        grid=(n_seq, seq // tq),
        in_specs=in_specs,
        out_specs=pl.BlockSpec((tq, heads * dh), lambda b, i: (b * (seq // tq) + i, 0)),
        out_shape=jax.ShapeDtypeStruct((n_seq * seq, heads * dh), BF16),
        scratch_shapes=[pltpu.VMEM((s_len, heads * hw), BF16)],
        compiler_params=_params("parallel", "arbitrary"),
        name="mla_attention",
    )(*args)


def _split3(x):
    hi = x.astype(BF16)
    lo = (x - hi.astype(F32)).astype(BF16)
    return hi, lo


def _post_attn_kernel(m_ref, x_ref, w_ref, mod_ref, g1_ref, b1_ref, rw_ref, rb_ref,
                      x1_ref, h2_ref, ri_ref, rg_ref, cnt_ref, run_ref, tri_ref, *, alpha, n_exp):
    tm = x_ref.shape[0]
    step = pl.program_id(0)

    @pl.when(step == 0)
    def _():
        run_ref[...] = jnp.zeros_like(run_ref)
        row = lax.broadcasted_iota(jnp.int32, (tm, tm), 0)
        col = lax.broadcasted_iota(jnp.int32, (tm, tm), 1)
        tri_ref[...] = jnp.where(row > col, 1.0, 0.0).astype(BF16)

    y = _dot(m_ref[...], w_ref[...])
    x1 = _ln(alpha * x_ref[...] + mod_ref[0, 2:3, :] * y) * g1_ref[...] + b1_ref[...]
    x1_ref[...] = x1
    h2 = _ln(x1) * (1.0 + mod_ref[0, 4:5, :]) + mod_ref[0, 3:4, :]
    half = h2_ref.shape[1]
    hi = pltpu.bitcast(h2[:, :half].astype(BF16).astype(F32), jnp.uint32)
    lo = pltpu.bitcast(h2[:, half:].astype(BF16).astype(F32), jnp.uint32)
    h2_ref[...] = hi | (lo >> 16)

    h_hi, h_lo = _split3(h2)
    w_hi, w_lo = _split3(rw_ref[...])
    logits = _dot(h_hi, w_hi) + _dot(h_lo, w_hi) + _dot(h_hi, w_lo) + rb_ref[...]
    lane = lax.broadcasted_iota(jnp.int32, logits.shape, 1).astype(F32)
    neg = jnp.float32(-jnp.inf)
    logits = jnp.where(lane < n_exp, logits, neg)

    counts = jnp.zeros(logits.shape, F32)
    vals, idxs = [], []
    for _ in range(TOP_K):
        mx = jnp.max(logits, axis=-1, keepdims=True)
        idx = jnp.min(jnp.where(logits == mx, lane, float(LANES)), axis=-1, keepdims=True)
        hit = lane == idx
        counts = counts + jnp.where(hit, 1.0, 0.0)
        logits = jnp.where(hit, neg, logits)
        vals.append(mx)
        idxs.append(idx)

    es = [jnp.exp(v - vals[0]) for v in vals]
    denom = es[0]
    for e in es[1:]:
        denom = denom + e
    before = _dot(tri_ref[...], counts.astype(BF16)) + run_ref[...]
    out_i = jnp.zeros(logits.shape, jnp.int32)
    out_g = jnp.zeros(logits.shape, F32)
    for k in range(TOP_K):
        rank = jnp.sum(jnp.where(lane == idxs[k], before, 0.0), axis=-1, keepdims=True)
        out_i = jnp.where(lane == k, idxs[k].astype(jnp.int32), out_i)
        out_i = jnp.where(lane == TOP_K + k, rank.astype(jnp.int32), out_i)
        out_g = jnp.where(lane == k, es[k] / denom, out_g)
    ri_ref[...] = out_i[:, 0:2 * TOP_K]
    rg_ref[...] = out_g[:, 0:TOP_K]
    run_ref[...] = run_ref[...] + jnp.sum(counts, axis=0, keepdims=True)
    cnt_ref[...] = run_ref[...]


def _post_attn(merged, x, w_out, mod, ln_g, ln_b, router_w, router_b, n_prompt, t_sample, alpha, n_exp, tm=512):
    m, d = x.shape
    tm = min(tm, t_sample, n_prompt)
    row_spec = pl.BlockSpec((tm, d), lambda i: (i, 0))
    vec_spec = pl.BlockSpec((1, d), lambda i: (0, 0))
    return pl.pallas_call(
        functools.partial(_post_attn_kernel, alpha=alpha, n_exp=n_exp),
        grid=(m // tm,),
        in_specs=[
            row_spec, row_spec,
            pl.BlockSpec((d, d), lambda i: (0, 0)),
            pl.BlockSpec((1, 6, d), lambda i: (_mod_row(i, tm, n_prompt, t_sample), 0, 0)),
            vec_spec, vec_spec,
            pl.BlockSpec((d, LANES), lambda i: (0, 0)),
            pl.BlockSpec((1, LANES), lambda i: (0, 0)),
        ],
        out_specs=[
            row_spec,
            pl.BlockSpec((tm, d // 2), lambda i: (i, 0)),
            pl.BlockSpec((tm, 2 * TOP_K), lambda i: (i, 0)),
            pl.BlockSpec((tm, TOP_K), lambda i: (i, 0)),
            pl.BlockSpec((1, LANES), lambda i: (0, 0)),
        ],
        out_shape=[
            jax.ShapeDtypeStruct((m, d), F32),
            jax.ShapeDtypeStruct((m, d // 2), jnp.uint32),
            jax.ShapeDtypeStruct((m, 2 * TOP_K), jnp.int32),
            jax.ShapeDtypeStruct((m, TOP_K), F32),
            jax.ShapeDtypeStruct((1, LANES), F32),
        ],
        scratch_shapes=[pltpu.VMEM((1, LANES), F32), pltpu.VMEM((tm, tm), BF16)],
        compiler_params=_params("arbitrary"),
        name="post_attn_router",
    )(merged, x, w_out, mod, ln_g.reshape(1, d), ln_b.reshape(1, d), router_w, router_b)


def _wait_slot(buf_ref, sem, slot):
    pltpu.make_async_copy(buf_ref.at[slot], buf_ref.at[slot], sem.at[slot]).wait()


def _moe_expert_kernel(be_ref, bv_ref, idx_ref, nxt_ref, h_hbm, wg_ref, bg_ref, wu_ref, bu_ref, wd_ref, bd_ref,
                       o_ref, xbuf_ref, xb_ref, sem, *, n_f):
    i = pl.program_id(0)
    f = pl.program_id(1)
    n_rows, half = xbuf_ref.shape[1], xbuf_ref.shape[2]
    share = n_rows // n_f
    valid = bv_ref[i]
    slot = i % 2
    del be_ref

    def row_copy(ids_ref, s, r):
        return pltpu.make_async_copy(h_hbm.at[pl.ds(ids_ref[0, 0, r], 1), :], xbuf_ref.at[s, pl.ds(r, 1), :],
                                     sem.at[s])

    @pl.when(jnp.logical_and(jnp.logical_and(i == 0, f == 0), valid > 0))
    def _():
        def body(r, carry):
            row_copy(idx_ref, 0, r).start()
            return carry

        lax.fori_loop(0, n_rows, body, 0, unroll=8)

    @pl.when(f == 0)
    def _():
        o_ref[...] = jnp.broadcast_to(bd_ref[0, 0], o_ref.shape)

    rows_in_flight = jnp.where(i == 0, valid > 0, bv_ref[jnp.maximum(i - 1, 0)] > 0)

    @pl.when(jnp.logical_and(f == 0, rows_in_flight))
    def _():
        _wait_slot(xbuf_ref, sem, slot)
        u = xbuf_ref[slot]
        xb_ref[:, :half] = pltpu.bitcast(u & jnp.uint32(0xFFFF0000), F32).astype(BF16)
        xb_ref[:, half:] = pltpu.bitcast(u << 16, F32).astype(BF16)

    for sb in range(n_rows // MOE_SUB):
        rows = slice(sb * MOE_SUB, (sb + 1) * MOE_SUB)

        @pl.when(sb * MOE_SUB < valid)
        def _():
            if sb == 0:
                for j in range(share):
                    row_copy(nxt_ref, 1 - slot, f * share + j).start()
            x = xb_ref[rows, :]
            gt = jnp.minimum(_dot(x, wg_ref[0, 0].astype(BF16)) + bg_ref[0, 0], SWIGLU_LIMIT)
            up = jnp.clip(_dot(x, wu_ref[0, 0].astype(BF16)) + bu_ref[0, 0], -SWIGLU_LIMIT, SWIGLU_LIMIT)
            act = gt * jax.nn.sigmoid(SWIGLU_ALPHA * gt) * (up + 1.0)
            o_ref[rows, :] = o_ref[rows, :] + _dot(act.astype(BF16), wd_ref[0, 0].astype(BF16))


def _moe_experts(h2p, buf_tok, blk_e, blk_valid, layer, w_gate, b_gate, w_up, b_up, w_down, b_down, tf=256):
    cap = buf_tok.shape[0]
    half = h2p.shape[1]
    n_layers, n_exp, d, ff = w_gate.shape
    assert d == 2 * half
    tf = _tile(ff, tf)
    nf = ff // tf
    nblk = cap // MOE_ROWS
    assert MOE_ROWS % nf == 0
    ids = buf_tok.reshape(nblk, 1, MOE_ROWS)

    def f_idx(i, f, bv):
        return jnp.where(bv[i] > 0, f, nf - 1)

    return pl.pallas_call(
        functools.partial(_moe_expert_kernel, n_f=nf),
        grid_spec=pltpu.PrefetchScalarGridSpec(
            num_scalar_prefetch=2,
            grid=(nblk, nf),
            in_specs=[
                pl.BlockSpec((1, 1, MOE_ROWS), lambda i, f, be, bv: (i, 0, 0), memory_space=pltpu.SMEM),
                pl.BlockSpec((1, 1, MOE_ROWS), lambda i, f, be, bv: (jnp.minimum(i + 1, nblk - 1), 0, 0),
                             memory_space=pltpu.SMEM),
                pl.BlockSpec(memory_space=pl.ANY),
                pl.BlockSpec((1, 1, d, tf), lambda i, f, be, bv: (layer, be[i], 0, f_idx(i, f, bv))),
                pl.BlockSpec((1, 1, 1, tf), lambda i, f, be, bv: (layer, be[i], 0, f_idx(i, f, bv))),
                pl.BlockSpec((1, 1, d, tf), lambda i, f, be, bv: (layer, be[i], 0, f_idx(i, f, bv))),
                pl.BlockSpec((1, 1, 1, tf), lambda i, f, be, bv: (layer, be[i], 0, f_idx(i, f, bv))),
                pl.BlockSpec((1, 1, tf, d), lambda i, f, be, bv: (layer, be[i], f_idx(i, f, bv), 0)),
                pl.BlockSpec((1, 1, 1, d), lambda i, f, be, bv: (layer, be[i], 0, 0)),
            ],
            out_specs=pl.BlockSpec((MOE_ROWS, d), lambda i, f, be, bv: (i, 0)),
            scratch_shapes=[pltpu.VMEM((2, MOE_ROWS, half), jnp.uint32), pltpu.VMEM((MOE_ROWS, d), BF16),
                            pltpu.SemaphoreType.DMA((2,))],
        ),
        out_shape=jax.ShapeDtypeStruct((cap, d), F32),
        compiler_params=_params("arbitrary", "arbitrary"),
        name="moe_experts",
    )(blk_e, blk_valid, ids, ids, h2p, w_gate, b_gate.reshape(n_layers, n_exp, 1, ff), w_up, b_up.reshape(n_layers, n_exp, 1, ff),
      w_down, b_down.reshape(n_layers, n_exp, 1, d))


def _moe_combine_kernel(idx_ref, nxt_ref, yb_hbm, g_ref, x_ref, mod_ref, lg_ref, lb_ref, o_ref, buf_ref, sem, *,
                        alpha):
    tc = x_ref.shape[0]
    blk = pl.program_id(0)
    slot = blk % 2

    def issue(ids_ref, s):
        def body(r, carry):
            for k in range(TOP_K):
                pltpu.make_async_copy(yb_hbm.at[pl.ds(ids_ref[0, 0, r * TOP_K + k], 1), :],
                                      buf_ref.at[s, k, pl.ds(r, 1), :], sem.at[s]).start(priority=k % 2)
            return carry

        lax.fori_loop(0, tc, body, 0, unroll=2)

    @pl.when(blk == 0)
    def _():
        issue(idx_ref, 0)

    @pl.when(blk + 1 < pl.num_programs(0))
    def _():
        issue(nxt_ref, 1 - slot)

    _wait_slot(buf_ref, sem, slot)
    g = g_ref[...]
    y = g[:, 0:1] * buf_ref[slot, 0]
    for k in range(1, TOP_K):
        y = y + g[:, k:k + 1] * buf_ref[slot, k]
    x2 = _ln(alpha * x_ref[...] + mod_ref[0, 5:6, :] * y) * lg_ref[...] + lb_ref[...]
    o_ref[...] = x2


def _moe_combine(yb, dest, gates, x1, mod, ln_g, ln_b, n_prompt, t_sample, alpha, tc=256):
    m, d = x1.shape
    tc = min(tc, t_sample, n_prompt)
    nb = m // tc
    ids = dest.reshape(nb, 1, tc * TOP_K)
    return pl.pallas_call(
        functools.partial(_moe_combine_kernel, alpha=alpha),
        grid=(nb,),
        in_specs=[
            pl.BlockSpec((1, 1, tc * TOP_K), lambda i: (i, 0, 0), memory_space=pltpu.SMEM),
            pl.BlockSpec((1, 1, tc * TOP_K), lambda i: (jnp.minimum(i + 1, nb - 1), 0, 0), memory_space=pltpu.SMEM),
            pl.BlockSpec(memory_space=pl.ANY),
            pl.BlockSpec((tc, TOP_K), lambda i: (i, 0)),
            pl.BlockSpec((tc, d), lambda i: (i, 0)),
            pl.BlockSpec((1, 6, d), lambda i: (_mod_row(i, tc, n_prompt, t_sample), 0, 0)),
            pl.BlockSpec((1, d), lambda i: (0, 0)),
            pl.BlockSpec((1, d), lambda i: (0, 0)),
        ],
        out_specs=pl.BlockSpec((tc, d), lambda i: (i, 0)),
        out_shape=jax.ShapeDtypeStruct((m, d), F32),
        scratch_shapes=[pltpu.VMEM((2, TOP_K, tc, d), F32), pltpu.SemaphoreType.DMA((2,))],
        compiler_params=_params("arbitrary"),
        name="moe_combine",
    )(ids, ids, yb, gates, x1, mod, ln_g.reshape(1, d), ln_b.reshape(1, d))


def _moe_plan(route_i, counts, n_exp):
    m = route_i.shape[0]
    top_e = route_i[:, :TOP_K]
    rank = route_i[:, TOP_K:]
    counts = counts[0, :n_exp].astype(jnp.int32)
    padded = (counts + MOE_ROWS - 1) // MOE_ROWS * MOE_ROWS
    pend = jnp.cumsum(padded)
    pstart = pend - padded
    dest = pstart[top_e] + rank
    nblk = (m * TOP_K + n_exp * (MOE_ROWS - 1)) // MOE_ROWS + 1
    blk_row = jnp.arange(nblk, dtype=jnp.int32) * MOE_ROWS
    n_used = pend[-1] // MOE_ROWS
    blk_e = jnp.minimum(jnp.searchsorted(pend, blk_row, side="right"), n_exp - 1).astype(jnp.int32)
    blk_valid = jnp.clip(counts[blk_e] - (blk_row - pstart[blk_e]), 0, MOE_ROWS).astype(jnp.int32)
    used = jnp.arange(nblk, dtype=jnp.int32) < n_used
    blk_valid = jnp.where(used, blk_valid, 0)
    blk_e = jnp.where(used, blk_e, blk_e[jnp.maximum(n_used - 1, 0)])
    tok = jnp.broadcast_to(jnp.arange(m, dtype=jnp.int32)[:, None], (m, TOP_K))
    buf_tok = jnp.zeros((nblk * MOE_ROWS,), jnp.int32).at[dest.reshape(-1)].set(tok.reshape(-1))
    return dest.astype(jnp.int32), buf_tok, blk_e, blk_valid


def _rot_half_cols(w):
    w1, w2, w3, w4 = jnp.split(w, 4, axis=-1)
    return jnp.concatenate([-w2, w1, -w4, w3], axis=-1)


def _rope_tables(n_prompt, n_batch, t_sample, rope, dh):
    rows = t_sample // GRID_W
    row = jnp.repeat(jnp.arange(rows), GRID_W).astype(F32)
    col = jnp.tile(jnp.arange(GRID_W), rows).astype(F32)
    n_freq = rope // 4
    inv = ROPE_BASE ** (-jnp.arange(n_freq, dtype=F32) / n_freq)
    ar = row[:, None] * inv
    ac = col[:, None] * inv
    ang = jnp.concatenate([ar, ar, ac, ac], axis=-1)
    cos = jnp.concatenate([jnp.ones((n_prompt, rope), F32), jnp.tile(jnp.cos(ang), (n_batch, 1))], axis=0)
    sin = jnp.concatenate([jnp.zeros((n_prompt, rope), F32), jnp.tile(jnp.sin(ang), (n_batch, 1))], axis=0)
    n = cos.shape[0]
    hw = 2 * dh
    pad = jnp.zeros((n, hw - dh - rope), F32)
    tq = jnp.concatenate([jnp.ones((n, dh), F32), cos, pad, jnp.zeros((n, dh), F32), sin, pad], axis=-1)
    tk = jnp.concatenate([cos, sin], axis=-1)
    return tq, tk


def kernel(x_prompt, x_sample, c, cache_ckv, cache_kpe, state_gla_fwd, state_gla_bwd, c_ctx, w_mod, b_mod, w_in,
           conv_w_dw, conv_b_dw, conv_ln_g, conv_ln_b, conv_w_out, gla_w_gate_fwd, gla_b_gate_fwd,
           gla_w_gate_bwd, gla_b_gate_bwd, gla_norm_g, gla_w_out, mla_q_norm_g, mla_w_uq, mla_kv_norm_g,
           mla_w_uk, mla_w_uv, mla_w_out, w_out, ln1_g, ln1_b, router_w, router_b, exp_w_gate, exp_b_gate,
           exp_w_up, exp_b_up, exp_w_down, exp_b_down, ln2_g, ln2_b):
    bp, tp, d = x_prompt.shape
    bs, ts, _ = x_sample.shape
    depth = w_in.shape[0]
    n_prompt, n_sample = bp * tp, bs * ts
    n_tok = n_prompt + n_sample
    conv_dim = conv_w_dw.shape[2]
    gla_heads, gla_dk, gla_dv = state_gla_fwd.shape[2:]
    gla_qk, gla_v = gla_heads * gla_dk, gla_heads * gla_dv
    gate_rank = gla_w_gate_fwd.shape[1]
    q_rank = mla_w_uq.shape[1]
    kv_rank = mla_w_uk.shape[1]
    rope = cache_kpe.shape[3]
    dh = mla_w_uk.shape[2] // MLA_HEADS
    n_exp = router_w.shape[2]
    past = cache_ckv.shape[2]
    alpha = (2 * depth) ** 0.25
    assert 2 * dh == 2 * LANES and rope == LANES // 2 and 2 * gate_rank <= LANES

    sizes = (conv_dim, conv_dim, gla_qk, gla_qk, gla_v, gla_v, gate_rank, gate_rank, q_rank, kv_rank, rope, d, d, d)
    offs = [0]
    for s in sizes:
        offs.append(offs[-1] + s)
    o_zf, o_cq, o_kpe, o_ma = offs[6], offs[8], offs[10], offs[11]
    c_ca, c_cg, c_gq, c_gk, c_gv, c_gr = offs[0], offs[1], offs[2], offs[3], offs[4], offs[5]
    c_cq = o_zf
    c_ckv = c_cq + q_rank
    c_ma = c_ckv + kv_rank
    c_pe = c_ma + 3 * d
    c_z = c_pe + LANES
    n_in = c_z + LANES

    cc = jnp.concatenate([c_ctx[None, :], c], axis=0)
    mp = (cc.shape[0] + 7) // 8 * 8
    cc = jnp.pad(cc, ((0, mp - cc.shape[0]), (0, 0)))
    mod_all = _modulation(cc, w_mod, b_mod).reshape(depth, mp, 6, d)

    tq_tab, tk_tab = _rope_tables(n_prompt, bs, ts, rope, dh)
    x = jnp.concatenate([x_prompt.reshape(n_prompt, d), x_sample.reshape(n_sample, d)], axis=0)
    scale = (dh + rope) ** -0.5

    ckv_l, kpe_l, sf_l, sb_l = [], [], [], []
    for l in range(depth):
        mod = mod_all[l]
        wl = w_in[l]
        kpe_cols = wl[:, o_kpe:o_kpe + rope]
        w_in_r = jnp.concatenate(
            [wl[:, :o_zf], wl[:, o_cq:o_kpe], wl[:, o_ma:], kpe_cols, _rot_half_cols(kpe_cols),
             wl[:, o_zf:o_cq], jnp.zeros((d, LANES - 2 * gate_rank), F32)], axis=1).astype(BF16)
        assert w_in_r.shape[1] == n_in
        zpad = jnp.zeros((LANES - 2 * gate_rank, gla_qk), F32)
        wgf = jnp.concatenate([gla_w_gate_fwd[l], jnp.zeros((gate_rank, gla_qk), F32), zpad], axis=0)
        wgb = jnp.concatenate([jnp.zeros((gate_rank, gla_qk), F32), gla_w_gate_bwd[l], zpad], axis=0)
        wq3 = mla_w_uq[l].reshape(q_rank, MLA_HEADS, dh + rope)
        zq = jnp.zeros((q_rank, MLA_HEADS, dh - rope), F32)
        wq = jnp.concatenate([wq3, zq], axis=-1).reshape(q_rank, -1).astype(BF16)
        wqr = jnp.concatenate([jnp.zeros((q_rank, MLA_HEADS, dh), F32), _rot_half_cols(wq3[..., dh:]), zq],
                              axis=-1).reshape(q_rank, -1).astype(BF16)
        wkv = jnp.concatenate([mla_w_uk[l], mla_w_uv[l]], axis=1).astype(BF16)
        rw = jnp.pad(router_w[l], ((0, 0), (0, LANES - n_exp)))
        rb = jnp.pad(router_b[l], (0, LANES - n_exp)).reshape(1, LANES)

        hmat = _gemm_in(x, mod, w_in_r, n_prompt, ts)

        conv_args = (hmat, c_ca, c_cg, conv_w_dw[l], conv_b_dw[l], conv_ln_g[l], conv_ln_b[l])
        ya = (_conv_branch(*conv_args, row_off=0, n_seq=bp, seq=tp),
              _conv_branch(*conv_args, row_off=n_prompt, n_seq=bs, seq=ts))

        gla_args = (hmat, (c_gq, c_gk, c_gv, c_gr, c_z), wgf, gla_b_gate_fwd[l], wgb, gla_b_gate_bwd[l],
                    gla_norm_g[l])
        gla_kw = dict(heads=gla_heads, dk=gla_dk, dv=gla_dv)
        og_p, s_f, s_b = _gla_branch(*gla_args, row_off=0, n_seq=bp, seq=tp, **gla_kw)
        og_s = _gla_branch(*gla_args, row_off=n_prompt, n_seq=bs, seq=ts, **gla_kw,
                           state=(state_gla_fwd[:, l], state_gla_bwd[:, l]))

        q_all, ckv, kpe = _mla_proj(hmat, c_cq, c_ckv, c_pe, mla_q_norm_g[l], mla_kv_norm_g[l], wq, wqr,
                                    tq_tab, tk_tab)
        kv_p = _gemm(ckv, wkv, rows=n_prompt, name="mla_kv").reshape(bp, tp, -1)
        keys_s = jnp.concatenate([cache_ckv[:, l], ckv[n_prompt:].reshape(bs, ts, kv_rank)], axis=1)
        kv_s = _gemm(keys_s.reshape(bs * (past + ts), kv_rank), wkv, name="mla_kv")
        kv_s = kv_s.reshape(bs, past + ts, -1)
        kpe_p = kpe[:n_prompt].reshape(bp, tp, LANES)
        kpe_s = jnp.concatenate([jnp.pad(cache_kpe[:, l], ((0, 0), (0, 0), (0, LANES - rope))),
                                 kpe[n_prompt:].reshape(bs, ts, LANES)], axis=1)
        att_kw = dict(heads=MLA_HEADS, dh=dh, scale=scale)
        att = (_attention(q_all, kv_p, kpe_p, row_off=0, n_seq=bp, seq=tp, **att_kw),
               _attention(q_all, kv_s, kpe_s, row_off=n_prompt, n_seq=bs, seq=ts, **att_kw))
        merged = _merge([ya, (og_p, og_s), att],
                        [conv_w_out[l].astype(BF16), gla_w_out[l].astype(BF16), mla_w_out[l].astype(BF16)],
                        hmat, c_ma)

        x1, h2, route_i, route_g, counts = _post_attn(merged, x, w_out[l].astype(BF16), mod, ln1_g[l], ln1_b[l],
                                                      rw, rb, n_prompt, ts, alpha, n_exp)

        dest, buf_tok, blk_e, blk_valid = _moe_plan(route_i, counts, n_exp)
        yb = _moe_experts(h2, buf_tok, blk_e, blk_valid, l, exp_w_gate, exp_b_gate, exp_w_up, exp_b_up, exp_w_down,
                          exp_b_down)
        x = _moe_combine(yb, dest, route_g, x1, mod, ln2_g[l], ln2_b[l], n_prompt, ts, alpha)

        ckv_l.append(ckv[:n_prompt].reshape(bp, tp, kv_rank))
        kpe_l.append(kpe_p[..., :rope])
        sf_l.append(s_f)
        sb_l.append(s_b)

    y_prompt = x[:n_prompt].reshape(bp, tp, d)
    y_sample = x[n_prompt:].reshape(bs, ts, d)
    return (y_prompt, y_sample, jnp.stack(ckv_l, axis=1), jnp.stack(kpe_l, axis=1),
            jnp.stack(sf_l, axis=1), jnp.stack(sb_l, axis=1))
```

```python
import functools

import jax
import jax.numpy as jnp
from jax import lax
from jax.experimental import pallas as pl
from jax.experimental.pallas import tpu as pltpu

F32 = jnp.float32
BF16 = jnp.bfloat16

LANES = 128
VMEM_LIMIT_BYTES = 56 * 1024 * 1024
GEMM_IN_VMEM_LIMIT_BYTES = 60 * 1024 * 1024

GRID_W = 64
GLA_TAU = 16.0
GLA_CHUNK = 64
MLA_HEADS = 8
ROPE_BASE = 10000.0
TOP_K = 4
SWIGLU_LIMIT = 7.0
SWIGLU_ALPHA = 1.702
LN_EPS = 1e-5
RMS_EPS = 1e-6

CONV_PAD = 16
MOE_ROWS = 1024
MOE_SUB = 512


def _params(*sem, vmem=VMEM_LIMIT_BYTES):
    return pltpu.CompilerParams(dimension_semantics=sem, vmem_limit_bytes=vmem)


def _tile(n, pref):
    if n <= pref:
        return n
    t = (pref // LANES) * LANES
    while t > LANES and n % t:
        t -= LANES
    assert n % t == 0, (n, pref)
    return t


def _ln(x):
    mu = jnp.mean(x, axis=-1, keepdims=True)
    xc = x - mu
    return xc * lax.rsqrt(jnp.mean(xc * xc, axis=-1, keepdims=True) + LN_EPS)


def _rms(x):
    return x * lax.rsqrt(jnp.mean(x * x, axis=-1, keepdims=True) + RMS_EPS)


def _silu(x):
    return x * jax.nn.sigmoid(x)


def _dot(a, b):
    return jnp.dot(a, b, preferred_element_type=F32)


def _dot_nt(a, b):
    return lax.dot_general(a, b, (((1,), (1,)), ((), ())), preferred_element_type=F32)


def _dot_tn(a, b):
    return lax.dot_general(a, b, (((0,), (0,)), ((), ())), preferred_element_type=F32)


def _mod_kernel(c_ref, w_ref, b_ref, o_ref):
    s = _silu(c_ref[...])
    o_ref[0] = _dot(s.astype(BF16), w_ref[0].astype(BF16)) + b_ref[0]


def _modulation(cc, w_mod, b_mod):
    n_layers, d, d6 = w_mod.shape
    mp = cc.shape[0]
    tn = _tile(d6, 1024)
    return pl.pallas_call(
        _mod_kernel,
        grid=(n_layers, d6 // tn),
        in_specs=[
            pl.BlockSpec((mp, d), lambda l, j: (0, 0)),
            pl.BlockSpec((1, d, tn), lambda l, j: (l, 0, j)),
            pl.BlockSpec((1, 1, tn), lambda l, j: (l, 0, j)),
        ],
        out_specs=pl.BlockSpec((1, mp, tn), lambda l, j: (l, 0, j)),
        out_shape=jax.ShapeDtypeStruct((n_layers, mp, d6), F32),
        compiler_params=_params("parallel", "parallel"),
        name="modulation",
    )(cc, w_mod, b_mod.reshape(n_layers, 1, d6))


def _mod_row(i, tm, n_prompt, t_sample):
    r = i * tm
    return jnp.where(r < n_prompt, 0, 1 + (r - n_prompt) // t_sample)


def _gemm_in_kernel(x_ref, mod_ref, w_ref, o_ref, h_ref):
    @pl.when(pl.program_id(1) == 0)
    def _():
        y = _ln(x_ref[...])
        h_ref[...] = (y * (1.0 + mod_ref[0, 1:2, :]) + mod_ref[0, 0:1, :]).astype(BF16)

    o_ref[...] = _dot(h_ref[...], w_ref[...])


def _gemm_in(x, mod, w, n_prompt, t_sample, tm=1024, tn=1792):
    m, d = x.shape
    n = w.shape[1]
    tm = min(tm, t_sample, n_prompt)
    tn = _tile(n, tn)
    assert m % tm == 0 and n_prompt % tm == 0 and t_sample % tm == 0
    return pl.pallas_call(
        _gemm_in_kernel,
        grid=(m // tm, n // tn),
        in_specs=[
            pl.BlockSpec((tm, d), lambda i, j: (i, 0)),
            pl.BlockSpec((1, 6, d), lambda i, j: (_mod_row(i, tm, n_prompt, t_sample), 0, 0)),
            pl.BlockSpec((d, tn), lambda i, j: (0, j)),
        ],
        out_specs=pl.BlockSpec((tm, tn), lambda i, j: (i, j)),
        out_shape=jax.ShapeDtypeStruct((m, n), F32),
        scratch_shapes=[pltpu.VMEM((tm, d), BF16)],
        compiler_params=_params("parallel", "arbitrary", vmem=GEMM_IN_VMEM_LIMIT_BYTES),
        name="gemm_in",
    )(x, mod, w)


def _gemm_kernel(x_ref, w_ref, o_ref):
    o_ref[...] = _dot(x_ref[...].astype(BF16), w_ref[...]).astype(o_ref.dtype)


def _gemm(x, w, *, rows=None, tm=512, tn=1024, name="gemm"):
    k, n = w.shape
    m = x.shape[0] if rows is None else rows
    tm = min(tm, m)
    tn = _tile(n, tn)
    assert m % tm == 0 and x.shape[1] == k
    return pl.pallas_call(
        _gemm_kernel,
        grid=(m // tm, n // tn),
        in_specs=[pl.BlockSpec((tm, k), lambda i, j: (i, 0)), pl.BlockSpec((k, tn), lambda i, j: (0, j))],
        out_specs=pl.BlockSpec((tm, tn), lambda i, j: (i, j)),
        out_shape=jax.ShapeDtypeStruct((m, n), BF16),
        compiler_params=_params("parallel", "parallel"),
        name=name,
    )(x, w)


def _merge_kernel(*refs, n_br, nb_first):
    x_refs = refs[:2 * n_br]
    w_refs = refs[2 * n_br:3 * n_br]
    g_refs = refs[3 * n_br:4 * n_br]
    o_ref = refs[4 * n_br]

    def compute(group):
        acc = None
        for b in range(n_br):
            term = jax.nn.sigmoid(g_refs[b][...]) * _dot(x_refs[2 * b + group][...], w_refs[b][...])
            acc = term if acc is None else acc + term
        o_ref[...] = acc.astype(o_ref.dtype)

    i = pl.program_id(0)
    pl.when(i < nb_first)(lambda: compute(0))
    pl.when(i >= nb_first)(lambda: compute(1))


def _merge(ys, ws, gate, gate_col, tm=512, tn=1024):
    n_br = len(ys)
    n = ws[0].shape[1]
    m = ys[0][0].shape[0] + ys[0][1].shape[0]
    tm = min(tm, ys[0][0].shape[0], ys[0][1].shape[0])
    tn = _tile(n, tn)
    nb_first = ys[0][0].shape[0] // tm
    assert gate_col % tn == 0 and n % tn == 0 and all(y[0].shape[0] % tm == 0 and y[1].shape[0] % tm == 0 for y in ys)
    in_specs, args = [], []
    for y, w in zip(ys, ws):
        assert w.shape == (y[0].shape[1], n) and y[1].shape[1] == w.shape[0]
        in_specs += [pl.BlockSpec((tm, w.shape[0]), lambda i, j: (jnp.minimum(i, nb_first - 1), 0)),
                     pl.BlockSpec((tm, w.shape[0]), lambda i, j: (jnp.maximum(i - nb_first, 0), 0))]
        args += [y[0], y[1]]
    in_specs += [pl.BlockSpec((w.shape[0], tn), lambda i, j: (0, j)) for w in ws]
    args += list(ws)
    for b in range(n_br):
        goff = (gate_col + b * n) // tn
        in_specs.append(pl.BlockSpec((tm, tn), lambda i, j, goff=goff: (i, goff + j)))
        args.append(gate)
    return pl.pallas_call(
        functools.partial(_merge_kernel, n_br=n_br, nb_first=nb_first),
        grid=(m // tm, n // tn),
        in_specs=in_specs,
        out_specs=pl.BlockSpec((tm, tn), lambda i, j: (i, j)),
        out_shape=jax.ShapeDtypeStruct((m, n), BF16),
        compiler_params=_params("parallel", "parallel"),
        name="merge_out",
    )(*args)


def _conv_kernel(a_ref, g_ref, wdw_ref, bdw_ref, lng_ref, lnb_ref, o_ref, upad_ref, yc_ref, *, seq, rb, rc, width):
    c = a_ref.shape[1]
    r = pl.program_id(1)
    fill_rows = min(seq, 64)

    @pl.when(r == 0)
    def _():
        zeros = jnp.zeros((CONV_PAD, c), F32)
        upad_ref[0:CONV_PAD, :] = zeros
        upad_ref[CONV_PAD + seq:CONV_PAD + seq + CONV_PAD, :] = zeros

        def fill(i, carry):
            src = pl.ds(pl.multiple_of(i * fill_rows, 8), fill_rows)
            dst = pl.ds(pl.multiple_of(CONV_PAD + i * fill_rows, 8), fill_rows)
            upad_ref[dst, :] = a_ref[src, :] * jax.nn.sigmoid(g_ref[src, :])
            return carry

        lax.fori_loop(0, seq // fill_rows, fill, 0)

    lead = CONV_PAD - width // 2
    n_shift = 8
    n_al = (lead + width - 1) // n_shift + 1
    win = rc + n_shift * n_al

    def row_chunk(ci, carry):
        base = pl.multiple_of(r * rb + ci * rc, 8)
        for cj in range(c // LANES):
            lanes = slice(cj * LANES, (cj + 1) * LANES)
            w_all = upad_ref[pl.ds(base, win), lanes]
            acc = jnp.zeros((rc, LANES), F32)
            for b in range(n_shift):
                taps = [(a8, n_shift * a8 + b - lead) for a8 in range(n_al)]
                taps = [(a8, k) for a8, k in taps if 0 <= k < width]
                if not taps:
                    continue
                w_b = w_all[b:b + rc + n_shift * (n_al - 1)]
                part = None
                for a8, k in taps:
                    term = wdw_ref[k:k + 1, lanes] * w_b[n_shift * a8:n_shift * a8 + rc]
                    part = term if part is None else part + term
                acc = acc + part
            yc_ref[pl.ds(pl.multiple_of(ci * rc, 8), rc), lanes] = acc + bdw_ref[:, lanes]
        return carry

    lax.fori_loop(0, rb // rc, row_chunk, 0)
    y = _ln(yc_ref[...]) * lng_ref[...] + lnb_ref[...]
    o_ref[...] = _silu(y).astype(o_ref.dtype)


def _conv_branch(hmat, col_a, col_g, w_dw, b_dw, ln_g, ln_b, *, row_off, n_seq, seq):
    width, c = w_dw.shape
    rb = min(seq, 256)
    rc = min(rb, 64)
    assert width // 2 <= CONV_PAD and row_off % seq == 0 and col_a % c == 0 and col_g % c == 0
    s_off = row_off // seq
    in_specs = [
        pl.BlockSpec((seq, c), lambda s, r: (s_off + s, col_a // c)),
        pl.BlockSpec((seq, c), lambda s, r: (s_off + s, col_g // c)),
        pl.BlockSpec((width, c), lambda s, r: (0, 0)),
        pl.BlockSpec((1, c), lambda s, r: (0, 0)),
        pl.BlockSpec((1, c), lambda s, r: (0, 0)),
        pl.BlockSpec((1, c), lambda s, r: (0, 0)),
    ]
    args = [hmat, hmat, w_dw, b_dw.reshape(1, c), ln_g.reshape(1, c), ln_b.reshape(1, c)]
    return pl.pallas_call(
        functools.partial(_conv_kernel, seq=seq, rb=rb, rc=rc, width=width),
        grid=(n_seq, seq // rb),
        in_specs=in_specs,
        out_specs=pl.BlockSpec((rb, c), lambda s, r: (s * (seq // rb) + r, 0)),
        out_shape=jax.ShapeDtypeStruct((n_seq * seq, c), BF16),
        scratch_shapes=[pltpu.VMEM((seq + 2 * CONV_PAD, c), F32), pltpu.VMEM((rb, c), F32)],
        compiler_params=_params("parallel", "arbitrary"),
        name="conv_branch",
    )(*args)


def _cumsum3(tri, la):
    hi = la.astype(BF16)
    r1 = la - hi.astype(F32)
    mid = r1.astype(BF16)
    lo = (r1 - mid.astype(F32)).astype(BF16)
    return _dot(tri, hi) + _dot(tri, mid) + _dot(tri, lo)


def _log_sigmoid(x):
    return jnp.minimum(x, 0.0) - jnp.log(1.0 + jnp.exp(-jnp.abs(x)))


def _gla_kernel(*refs, seq, chunk, has_state):
    it = iter(refs)
    q_ref, k_ref, v_ref, r_ref, z_ref = (next(it) for _ in range(5))
    wgf_ref, bgf_ref, wgb_ref, bgb_ref, ng_ref = (next(it) for _ in range(5))
    if has_state:
        s0f_ref, s0b_ref = next(it), next(it)
        o_ref = next(it)
        sf_ref = sb_ref = None
    else:
        o_ref, sf_ref, sb_ref = next(it), next(it), next(it)
    laf_ref, lab_ref, oacc_ref, kvf_ref, kvb_ref, qdf_ref, qdb_ref, decf_ref, decb_ref = (next(it) for _ in range(9))

    dk = q_ref.shape[1]
    dv = v_ref.shape[1]
    n_chunks = seq // chunk
    z = z_ref[...].astype(BF16)
    laf_ref[...] = _log_sigmoid(_dot(z, wgf_ref[...].astype(BF16)) + bgf_ref[...]) / GLA_TAU
    lab_ref[...] = _log_sigmoid(_dot(z, wgb_ref[...].astype(BF16)) + bgb_ref[...]) / GLA_TAU

    row = lax.broadcasted_iota(jnp.int32, (chunk, chunk), 0)
    col = lax.broadcasted_iota(jnp.int32, (chunk, chunk), 1)
    lower = row >= col
    upper = row <= col
    tri_f = jnp.where(lower, 1.0, 0.0).astype(BF16)
    tri_b = jnp.where(upper, 1.0, 0.0).astype(BF16)
    q_scale = dk ** -0.5

    def intra(n, rows, la_ref, tri, mask, edge, qd_ref, kv_ref, dec_ref):
        b = _cumsum3(tri, la_ref[rows, :])
        b_edge = b[edge:edge + 1, :]
        q = q_ref[rows, :] * q_scale
        k = k_ref[rows, :]
        v = v_ref[rows, :].astype(BF16)
        qd = (q * jnp.exp(b)).astype(BF16)
        kd = (k * jnp.exp(-b)).astype(BF16)
        ke = (k * jnp.exp(b_edge - b)).astype(BF16)
        att = jnp.where(mask, _dot_nt(qd, kd), 0.0).astype(BF16)
        qd_ref[rows, :] = qd
        kv_ref[n] = _dot_tn(v, ke)
        dec_ref[n] = jnp.broadcast_to(jnp.exp(b_edge), dec_ref.shape[1:])
        return _dot(att, v)

    def phase1(n, carry):
        rows = pl.ds(pl.multiple_of(n * chunk, chunk), chunk)
        oacc_ref[rows, :] = (intra(n, rows, laf_ref, tri_f, lower, chunk - 1, qdf_ref, kvf_ref, decf_ref)
                             + intra(n, rows, lab_ref, tri_b, upper, 0, qdb_ref, kvb_ref, decb_ref))
        return carry

    lax.fori_loop(0, n_chunks, phase1, 0, unroll=4)

    def scan(order, kv_ref, dec_ref, s0):
        def body(j, s):
            n = order(j)
            kv = kv_ref[n]
            kv_ref[n] = s
            return s * dec_ref[n, 0:1, :] + kv

        return lax.fori_loop(0, n_chunks, body, s0)

    zero = jnp.zeros((dv, dk), F32)
    s_f = scan(lambda j: j, kvf_ref, decf_ref, s0f_ref[0, 0].T if has_state else zero)
    s_b = scan(lambda j: n_chunks - 1 - j, kvb_ref, decb_ref, s0b_ref[0, 0].T if has_state else zero)
    if not has_state:
        sf_ref[0, 0] = s_f.T
        sb_ref[0, 0] = s_b.T

    def phase3(n, carry):
        rows = pl.ds(pl.multiple_of(n * chunk, chunk), chunk)
        o = (oacc_ref[rows, :] + _dot_nt(qdf_ref[rows, :], kvf_ref[n].astype(BF16))
             + _dot_nt(qdb_ref[rows, :], kvb_ref[n].astype(BF16)))
        o = _rms(o) * ng_ref[...]
        o_ref[rows, :] = (o * _silu(r_ref[rows, :])).astype(o_ref.dtype)
        return carry

    lax.fori_loop(0, n_chunks, phase3, 0, unroll=4)


def _gla_branch(hmat, cols, wgf, bgf, wgb, bgb, norm_g, *, heads, dk, dv, row_off, n_seq, seq, state=None):
    col_q, col_k, col_v, col_r, col_z = cols
    s_off = row_off // seq
    has_state = state is not None
    in_specs = [
        pl.BlockSpec((seq, dk), lambda s, h: (s_off + s, col_q // dk + h)),
        pl.BlockSpec((seq, dk), lambda s, h: (s_off + s, col_k // dk + h)),
        pl.BlockSpec((seq, dv), lambda s, h: (s_off + s, col_v // dv + h)),
        pl.BlockSpec((seq, dv), lambda s, h: (s_off + s, col_r // dv + h)),
        pl.BlockSpec((seq, LANES), lambda s, h: (s_off + s, col_z // LANES)),
        pl.BlockSpec((LANES, dk), lambda s, h: (0, h)),
        pl.BlockSpec((1, dk), lambda s, h: (0, h)),
        pl.BlockSpec((LANES, dk), lambda s, h: (0, h)),
        pl.BlockSpec((1, dk), lambda s, h: (0, h)),
        pl.BlockSpec((1, dv), lambda s, h: (0, h)),
    ]
    args = [hmat] * 5 + [wgf, bgf.reshape(1, -1), wgb, bgb.reshape(1, -1), norm_g.reshape(1, -1)]
    o_spec = pl.BlockSpec((seq, dv), lambda s, h: (s, h))
    o_shape = jax.ShapeDtypeStruct((n_seq * seq, heads * dv), BF16)
    if has_state:
        st_spec = pl.BlockSpec((1, 1, dk, dv), lambda s, h: (s, h, 0, 0))
        in_specs += [st_spec, st_spec]
        args += [state[0], state[1]]
        out_specs, out_shape = o_spec, o_shape
    else:
        st_spec = pl.BlockSpec((1, 1, dk, dv), lambda s, h: (s, h, 0, 0))
        st_shape = jax.ShapeDtypeStruct((n_seq, heads, dk, dv), F32)
        out_specs, out_shape = [o_spec, st_spec, st_spec], [o_shape, st_shape, st_shape]
    chunk = min(GLA_CHUNK, seq)
    n_chunks = seq // chunk
    return pl.pallas_call(
        functools.partial(_gla_kernel, seq=seq, chunk=chunk, has_state=has_state),
        grid=(n_seq, heads),
        in_specs=in_specs,
        out_specs=out_specs,
        out_shape=out_shape,
        scratch_shapes=[pltpu.VMEM((seq, dk), F32), pltpu.VMEM((seq, dk), F32), pltpu.VMEM((seq, dv), F32),
                        pltpu.VMEM((n_chunks, dv, dk), F32), pltpu.VMEM((n_chunks, dv, dk), F32),
                        pltpu.VMEM((seq, dk), BF16), pltpu.VMEM((seq, dk), BF16),
                        pltpu.VMEM((n_chunks, 8, dk), F32), pltpu.VMEM((n_chunks, 8, dk), F32)],
        compiler_params=_params("parallel", "parallel"),
        name="gla_branch",
    )(*args)


def _mla_proj_kernel(cq_ref, ckv_ref, pe_ref, qg_ref, kvg_ref, wq_ref, wqr_ref, tq_ref, tk_ref,
                     q_ref, ckvn_ref, kpe_ref, *, hw):
    cqn = (_rms(cq_ref[...]) * qg_ref[...]).astype(BF16)
    a = _dot(cqn, wq_ref[...])
    ar = _dot(cqn, wqr_ref[...])
    cos = tq_ref[:, 0:hw]
    sin = tq_ref[:, hw:2 * hw]
    for h in range(a.shape[1] // hw):
        sl = slice(h * hw, (h + 1) * hw)
        q_ref[:, sl] = (a[:, sl] * cos + ar[:, sl] * sin).astype(q_ref.dtype)
    ckvn_ref[...] = _rms(ckv_ref[...]) * kvg_ref[...]
    v = pe_ref[...] * tk_ref[...]
    v = v + pltpu.roll(v, LANES // 2, axis=1)
    lane = lax.broadcasted_iota(jnp.int32, v.shape, 1)
    kpe_ref[...] = jnp.where(lane < LANES // 2, v, 0.0)


def _mla_proj(hmat, col_cq, col_ckv, col_pe, q_g, kv_g, wq, wqr, tq, tk, tm=512):
    m = hmat.shape[0]
    rq = q_g.shape[0]
    rkv = kv_g.shape[0]
    nq = wq.shape[1]
    hw = tq.shape[1] // 2
    tm = min(tm, m)
    return pl.pallas_call(
        functools.partial(_mla_proj_kernel, hw=hw),
        grid=(m // tm,),
        in_specs=[
            pl.BlockSpec((tm, rq), lambda i: (i, col_cq // rq)),
            pl.BlockSpec((tm, rkv), lambda i: (i, col_ckv // rkv)),
            pl.BlockSpec((tm, LANES), lambda i: (i, col_pe // LANES)),
            pl.BlockSpec((1, rq), lambda i: (0, 0)),
            pl.BlockSpec((1, rkv), lambda i: (0, 0)),
            pl.BlockSpec((rq, nq), lambda i: (0, 0)),
            pl.BlockSpec((rq, nq), lambda i: (0, 0)),
            pl.BlockSpec((tm, 2 * hw), lambda i: (i, 0)),
            pl.BlockSpec((tm, LANES), lambda i: (i, 0)),
        ],
        out_specs=[
            pl.BlockSpec((tm, nq), lambda i: (i, 0)),
            pl.BlockSpec((tm, rkv), lambda i: (i, 0)),
            pl.BlockSpec((tm, LANES), lambda i: (i, 0)),
        ],
        out_shape=[
            jax.ShapeDtypeStruct((m, nq), BF16),
            jax.ShapeDtypeStruct((m, rkv), F32),
            jax.ShapeDtypeStruct((m, LANES), F32),
        ],
        compiler_params=_params("parallel"),
        name="mla_proj",
    )(hmat, hmat, hmat, q_g.reshape(1, rq), kv_g.reshape(1, rkv), wq, wqr, tq, tk)


def _attn_kernel(*refs, heads, hw, dh, scale):
    q_ref, kv_ref, kpe_ref = refs[:3]
    o_ref, kcat_ref = refs[-2:]

    @pl.when(pl.program_id(1) == 0)
    def _():
        kp = kpe_ref[0].astype(BF16)
        for h in range(heads):
            kcat_ref[:, h * hw:h * hw + dh] = kv_ref[0, :, h * dh:(h + 1) * dh]
            kcat_ref[:, h * hw + dh:(h + 1) * hw] = kp

    v_off = heads * dh
    for h in range(heads):
        s = _dot_nt(q_ref[:, h * hw:(h + 1) * hw], kcat_ref[:, h * hw:(h + 1) * hw]) * scale
        p = jnp.exp(s - jnp.max(s, axis=-1, keepdims=True))
        l = jnp.sum(p, axis=-1, keepdims=True)
        o = _dot(p.astype(BF16), kv_ref[0, :, v_off + h * dh:v_off + (h + 1) * dh]) / l
        o_ref[:, h * dh:(h + 1) * dh] = o.astype(o_ref.dtype)


def _attention(q_all, kv, kpe, *, heads, dh, scale, row_off, n_seq, seq, tq=256):
    s_len = kv.shape[1]
    hw = q_all.shape[1] // heads
    tq = min(tq, seq)
    q_off = row_off // tq
    in_specs = [
        pl.BlockSpec((tq, heads * hw), lambda b, i: (q_off + b * (seq // tq) + i, 0)),
        pl.BlockSpec((1, s_len, 2 * heads * dh), lambda b, i: (b, 0, 0)),
        pl.BlockSpec((1, s_len, LANES), lambda b, i: (b, 0, 0)),
    ]
    args = [q_all, kv, kpe]
    return pl.pallas_call(
        functools.partial(_attn_kernel, heads=heads, hw=hw, dh=dh, scale=scale),
        grid=(n_seq, seq // tq),
        in_specs=in_specs,
        out_specs=pl.BlockSpec((tq, heads * dh), lambda b, i: (b * (seq // tq) + i, 0)),
        out_shape=jax.ShapeDtypeStruct((n_seq * seq, heads * dh), BF16),
        scratch_shapes=[pltpu.VMEM((s_len, heads * hw), BF16)],
        compiler_params=_params("parallel", "arbitrary"),
        name="mla_attention",
    )(*args)


def _split3(x):
    hi = x.astype(BF16)
    lo = (x - hi.astype(F32)).astype(BF16)
    return hi, lo


def _post_attn_kernel(m_ref, x_ref, w_ref, mod_ref, g1_ref, b1_ref, rw_ref, rb_ref,
                      x1_ref, h2_ref, ri_ref, rg_ref, cnt_ref, run_ref, tri_ref, *, alpha, n_exp):
    tm = x_ref.shape[0]
    step = pl.program_id(0)

    @pl.when(step == 0)
    def _():
        run_ref[...] = jnp.zeros_like(run_ref)
        row = lax.broadcasted_iota(jnp.int32, (tm, tm), 0)
        col = lax.broadcasted_iota(jnp.int32, (tm, tm), 1)
        tri_ref[...] = jnp.where(row > col, 1.0, 0.0).astype(BF16)

    y = _dot(m_ref[...], w_ref[...])
    x1 = _ln(alpha * x_ref[...] + mod_ref[0, 2:3, :] * y) * g1_ref[...] + b1_ref[...]
    x1_ref[...] = x1
    h2 = _ln(x1) * (1.0 + mod_ref[0, 4:5, :]) + mod_ref[0, 3:4, :]
    half = h2_ref.shape[1]
    hi = pltpu.bitcast(h2[:, :half].astype(BF16).astype(F32), jnp.uint32)
    lo = pltpu.bitcast(h2[:, half:].astype(BF16).astype(F32), jnp.uint32)
    h2_ref[...] = hi | (lo >> 16)

    h_hi, h_lo = _split3(h2)
    w_hi, w_lo = _split3(rw_ref[...])
    logits = _dot(h_hi, w_hi) + _dot(h_lo, w_hi) + _dot(h_hi, w_lo) + rb_ref[...]
    lane = lax.broadcasted_iota(jnp.int32, logits.shape, 1).astype(F32)
    neg = jnp.float32(-jnp.inf)
    logits = jnp.where(lane < n_exp, logits, neg)

    counts = jnp.zeros(logits.shape, F32)
    vals, idxs = [], []
    for _ in range(TOP_K):
        mx = jnp.max(logits, axis=-1, keepdims=True)
        idx = jnp.min(jnp.where(logits == mx, lane, float(LANES)), axis=-1, keepdims=True)
        hit = lane == idx
        counts = counts + jnp.where(hit, 1.0, 0.0)
        logits = jnp.where(hit, neg, logits)
        vals.append(mx)
        idxs.append(idx)

    es = [jnp.exp(v - vals[0]) for v in vals]
    denom = es[0]
    for e in es[1:]:
        denom = denom + e
    before = _dot(tri_ref[...], counts.astype(BF16)) + run_ref[...]
    out_i = jnp.zeros(logits.shape, jnp.int32)
    out_g = jnp.zeros(logits.shape, F32)
    for k in range(TOP_K):
        rank = jnp.sum(jnp.where(lane == idxs[k], before, 0.0), axis=-1, keepdims=True)
        out_i = jnp.where(lane == k, idxs[k].astype(jnp.int32), out_i)
        out_i = jnp.where(lane == TOP_K + k, rank.astype(jnp.int32), out_i)
        out_g = jnp.where(lane == k, es[k] / denom, out_g)
    ri_ref[...] = out_i[:, 0:2 * TOP_K]
    rg_ref[...] = out_g[:, 0:TOP_K]
    run_ref[...] = run_ref[...] + jnp.sum(counts, axis=0, keepdims=True)
    cnt_ref[...] = run_ref[...]


def _post_attn(merged, x, w_out, mod, ln_g, ln_b, router_w, router_b, n_prompt, t_sample, alpha, n_exp, tm=512):
    m, d = x.shape
    tm = min(tm, t_sample, n_prompt)
    row_spec = pl.BlockSpec((tm, d), lambda i: (i, 0))
    vec_spec = pl.BlockSpec((1, d), lambda i: (0, 0))
    return pl.pallas_call(
        functools.partial(_post_attn_kernel, alpha=alpha, n_exp=n_exp),
        grid=(m // tm,),
        in_specs=[
            row_spec, row_spec,
            pl.BlockSpec((d, d), lambda i: (0, 0)),
            pl.BlockSpec((1, 6, d), lambda i: (_mod_row(i, tm, n_prompt, t_sample), 0, 0)),
            vec_spec, vec_spec,
            pl.BlockSpec((d, LANES), lambda i: (0, 0)),
            pl.BlockSpec((1, LANES), lambda i: (0, 0)),
        ],
        out_specs=[
            row_spec,
            pl.BlockSpec((tm, d // 2), lambda i: (i, 0)),
            pl.BlockSpec((tm, 2 * TOP_K), lambda i: (i, 0)),
            pl.BlockSpec((tm, TOP_K), lambda i: (i, 0)),
            pl.BlockSpec((1, LANES), lambda i: (0, 0)),
        ],
        out_shape=[
            jax.ShapeDtypeStruct((m, d), F32),
            jax.ShapeDtypeStruct((m, d // 2), jnp.uint32),
            jax.ShapeDtypeStruct((m, 2 * TOP_K), jnp.int32),
            jax.ShapeDtypeStruct((m, TOP_K), F32),
            jax.ShapeDtypeStruct((1, LANES), F32),
        ],
        scratch_shapes=[pltpu.VMEM((1, LANES), F32), pltpu.VMEM((tm, tm), BF16)],
        compiler_params=_params("arbitrary"),
        name="post_attn_router",
    )(merged, x, w_out, mod, ln_g.reshape(1, d), ln_b.reshape(1, d), router_w, router_b)


def _wait_slot(buf_ref, sem, slot):
    pltpu.make_async_copy(buf_ref.at[slot], buf_ref.at[slot], sem.at[slot]).wait()


def _moe_expert_kernel(be_ref, bv_ref, idx_ref, nxt_ref, h_hbm, wg_ref, bg_ref, wu_ref, bu_ref, wd_ref, bd_ref,
                       o_ref, xbuf_ref, xb_ref, sem, *, n_f):
    i = pl.program_id(0)
    f = pl.program_id(1)
    n_rows, half = xbuf_ref.shape[1], xbuf_ref.shape[2]
    share = n_rows // n_f
    valid = bv_ref[i]
    slot = i % 2
    del be_ref

    def row_copy(ids_ref, s, r):
        return pltpu.make_async_copy(h_hbm.at[pl.ds(ids_ref[0, 0, r], 1), :], xbuf_ref.at[s, pl.ds(r, 1), :],
                                     sem.at[s])

    @pl.when(jnp.logical_and(jnp.logical_and(i == 0, f == 0), valid > 0))
    def _():
        def body(r, carry):
            row_copy(idx_ref, 0, r).start()
            return carry

        lax.fori_loop(0, n_rows, body, 0, unroll=8)

    @pl.when(f == 0)
    def _():
        o_ref[...] = jnp.broadcast_to(bd_ref[0, 0], o_ref.shape)

    rows_in_flight = jnp.where(i == 0, valid > 0, bv_ref[jnp.maximum(i - 1, 0)] > 0)

    @pl.when(jnp.logical_and(f == 0, rows_in_flight))
    def _():
        _wait_slot(xbuf_ref, sem, slot)
        u = xbuf_ref[slot]
        xb_ref[:, :half] = pltpu.bitcast(u & jnp.uint32(0xFFFF0000), F32).astype(BF16)
        xb_ref[:, half:] = pltpu.bitcast(u << 16, F32).astype(BF16)

    for sb in range(n_rows // MOE_SUB):
        rows = slice(sb * MOE_SUB, (sb + 1) * MOE_SUB)

        @pl.when(sb * MOE_SUB < valid)
        def _():
            if sb == 0:
                for j in range(share):
                    row_copy(nxt_ref, 1 - slot, f * share + j).start()
            x = xb_ref[rows, :]
            gt = jnp.minimum(_dot(x, wg_ref[0, 0].astype(BF16)) + bg_ref[0, 0], SWIGLU_LIMIT)
            up = jnp.clip(_dot(x, wu_ref[0, 0].astype(BF16)) + bu_ref[0, 0], -SWIGLU_LIMIT, SWIGLU_LIMIT)
            act = gt * jax.nn.sigmoid(SWIGLU_ALPHA * gt) * (up + 1.0)
            o_ref[rows, :] = o_ref[rows, :] + _dot(act.astype(BF16), wd_ref[0, 0].astype(BF16))


def _moe_experts(h2p, buf_tok, blk_e, blk_valid, layer, w_gate, b_gate, w_up, b_up, w_down, b_down, tf=256):
    cap = buf_tok.shape[0]
    half = h2p.shape[1]
    n_layers, n_exp, d, ff = w_gate.shape
    assert d == 2 * half
    tf = _tile(ff, tf)
    nf = ff // tf
    nblk = cap // MOE_ROWS
    assert MOE_ROWS % nf == 0
    ids = buf_tok.reshape(nblk, 1, MOE_ROWS)

    def f_idx(i, f, bv):
        return jnp.where(bv[i] > 0, f, nf - 1)

    return pl.pallas_call(
        functools.partial(_moe_expert_kernel, n_f=nf),
        grid_spec=pltpu.PrefetchScalarGridSpec(
            num_scalar_prefetch=2,
            grid=(nblk, nf),
            in_specs=[
                pl.BlockSpec((1, 1, MOE_ROWS), lambda i, f, be, bv: (i, 0, 0), memory_space=pltpu.SMEM),
                pl.BlockSpec((1, 1, MOE_ROWS), lambda i, f, be, bv: (jnp.minimum(i + 1, nblk - 1), 0, 0),
                             memory_space=pltpu.SMEM),
                pl.BlockSpec(memory_space=pl.ANY),
                pl.BlockSpec((1, 1, d, tf), lambda i, f, be, bv: (layer, be[i], 0, f_idx(i, f, bv))),
                pl.BlockSpec((1, 1, 1, tf), lambda i, f, be, bv: (layer, be[i], 0, f_idx(i, f, bv))),
                pl.BlockSpec((1, 1, d, tf), lambda i, f, be, bv: (layer, be[i], 0, f_idx(i, f, bv))),
                pl.BlockSpec((1, 1, 1, tf), lambda i, f, be, bv: (layer, be[i], 0, f_idx(i, f, bv))),
                pl.BlockSpec((1, 1, tf, d), lambda i, f, be, bv: (layer, be[i], f_idx(i, f, bv), 0)),
                pl.BlockSpec((1, 1, 1, d), lambda i, f, be, bv: (layer, be[i], 0, 0)),
            ],
            out_specs=pl.BlockSpec((MOE_ROWS, d), lambda i, f, be, bv: (i, 0)),
            scratch_shapes=[pltpu.VMEM((2, MOE_ROWS, half), jnp.uint32), pltpu.VMEM((MOE_ROWS, d), BF16),
                            pltpu.SemaphoreType.DMA((2,))],
        ),
        out_shape=jax.ShapeDtypeStruct((cap, d), F32),
        compiler_params=_params("arbitrary", "arbitrary"),
        name="moe_experts",
    )(blk_e, blk_valid, ids, ids, h2p, w_gate, b_gate.reshape(n_layers, n_exp, 1, ff), w_up, b_up.reshape(n_layers, n_exp, 1, ff),
      w_down, b_down.reshape(n_layers, n_exp, 1, d))


def _moe_combine_kernel(idx_ref, nxt_ref, yb_hbm, g_ref, x_ref, mod_ref, lg_ref, lb_ref, o_ref, buf_ref, sem, *,
                        alpha):
    tc = x_ref.shape[0]
    blk = pl.program_id(0)
    slot = blk % 2

    def issue(ids_ref, s):
        def body(r, carry):
            for k in range(TOP_K):
                pltpu.make_async_copy(yb_hbm.at[pl.ds(ids_ref[0, 0, r * TOP_K + k], 1), :],
                                      buf_ref.at[s, k, pl.ds(r, 1), :], sem.at[s]).start()
            return carry

        lax.fori_loop(0, tc, body, 0, unroll=2)

    @pl.when(blk == 0)
    def _():
        issue(idx_ref, 0)

    @pl.when(blk + 1 < pl.num_programs(0))
    def _():
        issue(nxt_ref, 1 - slot)

    _wait_slot(buf_ref, sem, slot)
    g = g_ref[...]
    y = g[:, 0:1] * buf_ref[slot, 0]
    for k in range(1, TOP_K):
        y = y + g[:, k:k + 1] * buf_ref[slot, k]
    x2 = _ln(alpha * x_ref[...] + mod_ref[0, 5:6, :] * y) * lg_ref[...] + lb_ref[...]
    o_ref[...] = x2


def _moe_combine(yb, dest, gates, x1, mod, ln_g, ln_b, n_prompt, t_sample, alpha, tc=128):
    m, d = x1.shape
    tc = min(tc, t_sample, n_prompt)
    nb = m // tc
    ids = dest.reshape(nb, 1, tc * TOP_K)
    return pl.pallas_call(
        functools.partial(_moe_combine_kernel, alpha=alpha),
        grid=(nb,),
        in_specs=[
            pl.BlockSpec((1, 1, tc * TOP_K), lambda i: (i, 0, 0), memory_space=pltpu.SMEM),
            pl.BlockSpec((1, 1, tc * TOP_K), lambda i: (jnp.minimum(i + 1, nb - 1), 0, 0), memory_space=pltpu.SMEM),
            pl.BlockSpec(memory_space=pl.ANY),
            pl.BlockSpec((tc, TOP_K), lambda i: (i, 0)),
            pl.BlockSpec((tc, d), lambda i: (i, 0)),
            pl.BlockSpec((1, 6, d), lambda i: (_mod_row(i, tc, n_prompt, t_sample), 0, 0)),
            pl.BlockSpec((1, d), lambda i: (0, 0)),
            pl.BlockSpec((1, d), lambda i: (0, 0)),
        ],
        out_specs=pl.BlockSpec((tc, d), lambda i: (i, 0)),
        out_shape=jax.ShapeDtypeStruct((m, d), F32),
        scratch_shapes=[pltpu.VMEM((2, TOP_K, tc, d), F32), pltpu.SemaphoreType.DMA((2,))],
        compiler_params=_params("arbitrary"),
        name="moe_combine",
    )(ids, ids, yb, gates, x1, mod, ln_g.reshape(1, d), ln_b.reshape(1, d))


def _moe_plan(route_i, counts, n_exp):
    m = route_i.shape[0]
    top_e = route_i[:, :TOP_K]
    rank = route_i[:, TOP_K:]
    counts = counts[0, :n_exp].astype(jnp.int32)
    padded = (counts + MOE_ROWS - 1) // MOE_ROWS * MOE_ROWS
    pend = jnp.cumsum(padded)
    pstart = pend - padded
    dest = pstart[top_e] + rank
    nblk = (m * TOP_K + n_exp * (MOE_ROWS - 1)) // MOE_ROWS + 1
    blk_row = jnp.arange(nblk, dtype=jnp.int32) * MOE_ROWS
    n_used = pend[-1] // MOE_ROWS
    blk_e = jnp.minimum(jnp.searchsorted(pend, blk_row, side="right"), n_exp - 1).astype(jnp.int32)
    blk_valid = jnp.clip(counts[blk_e] - (blk_row - pstart[blk_e]), 0, MOE_ROWS).astype(jnp.int32)
    used = jnp.arange(nblk, dtype=jnp.int32) < n_used
    blk_valid = jnp.where(used, blk_valid, 0)
    blk_e = jnp.where(used, blk_e, blk_e[jnp.maximum(n_used - 1, 0)])
    tok = jnp.broadcast_to(jnp.arange(m, dtype=jnp.int32)[:, None], (m, TOP_K))
    buf_tok = jnp.zeros((nblk * MOE_ROWS,), jnp.int32).at[dest.reshape(-1)].set(tok.reshape(-1))
    return dest.astype(jnp.int32), buf_tok, blk_e, blk_valid


def _rot_half_cols(w):
    w1, w2, w3, w4 = jnp.split(w, 4, axis=-1)
    return jnp.concatenate([-w2, w1, -w4, w3], axis=-1)


def _rope_tables(n_prompt, n_batch, t_sample, rope, dh):
    rows = t_sample // GRID_W
    row = jnp.repeat(jnp.arange(rows), GRID_W).astype(F32)
    col = jnp.tile(jnp.arange(GRID_W), rows).astype(F32)
    n_freq = rope // 4
    inv = ROPE_BASE ** (-jnp.arange(n_freq, dtype=F32) / n_freq)
    ar = row[:, None] * inv
    ac = col[:, None] * inv
    ang = jnp.concatenate([ar, ar, ac, ac], axis=-1)
    cos = jnp.concatenate([jnp.ones((n_prompt, rope), F32), jnp.tile(jnp.cos(ang), (n_batch, 1))], axis=0)
    sin = jnp.concatenate([jnp.zeros((n_prompt, rope), F32), jnp.tile(jnp.sin(ang), (n_batch, 1))], axis=0)
    n = cos.shape[0]
    hw = 2 * dh
    pad = jnp.zeros((n, hw - dh - rope), F32)
    tq = jnp.concatenate([jnp.ones((n, dh), F32), cos, pad, jnp.zeros((n, dh), F32), sin, pad], axis=-1)
    tk = jnp.concatenate([cos, sin], axis=-1)
    return tq, tk


def kernel(x_prompt, x_sample, c, cache_ckv, cache_kpe, state_gla_fwd, state_gla_bwd, c_ctx, w_mod, b_mod, w_in,
           conv_w_dw, conv_b_dw, conv_ln_g, conv_ln_b, conv_w_out, gla_w_gate_fwd, gla_b_gate_fwd,
           gla_w_gate_bwd, gla_b_gate_bwd, gla_norm_g, gla_w_out, mla_q_norm_g, mla_w_uq, mla_kv_norm_g,
           mla_w_uk, mla_w_uv, mla_w_out, w_out, ln1_g, ln1_b, router_w, router_b, exp_w_gate, exp_b_gate,
           exp_w_up, exp_b_up, exp_w_down, exp_b_down, ln2_g, ln2_b):
    bp, tp, d = x_prompt.shape
    bs, ts, _ = x_sample.shape
    depth = w_in.shape[0]
    n_prompt, n_sample = bp * tp, bs * ts
    n_tok = n_prompt + n_sample
    conv_dim = conv_w_dw.shape[2]
    gla_heads, gla_dk, gla_dv = state_gla_fwd.shape[2:]
    gla_qk, gla_v = gla_heads * gla_dk, gla_heads * gla_dv
    gate_rank = gla_w_gate_fwd.shape[1]
    q_rank = mla_w_uq.shape[1]
    kv_rank = mla_w_uk.shape[1]
    rope = cache_kpe.shape[3]
    dh = mla_w_uk.shape[2] // MLA_HEADS
    n_exp = router_w.shape[2]
    past = cache_ckv.shape[2]
    alpha = (2 * depth) ** 0.25
    assert 2 * dh == 2 * LANES and rope == LANES // 2 and 2 * gate_rank <= LANES

    sizes = (conv_dim, conv_dim, gla_qk, gla_qk, gla_v, gla_v, gate_rank, gate_rank, q_rank, kv_rank, rope, d, d, d)
    offs = [0]
    for s in sizes:
        offs.append(offs[-1] + s)
    o_zf, o_cq, o_kpe, o_ma = offs[6], offs[8], offs[10], offs[11]
    c_ca, c_cg, c_gq, c_gk, c_gv, c_gr = offs[0], offs[1], offs[2], offs[3], offs[4], offs[5]
    c_cq = o_zf
    c_ckv = c_cq + q_rank
    c_ma = c_ckv + kv_rank
    c_pe = c_ma + 3 * d
    c_z = c_pe + LANES
    n_in = c_z + LANES

    cc = jnp.concatenate([c_ctx[None, :], c], axis=0)
    mp = (cc.shape[0] + 7) // 8 * 8
    cc = jnp.pad(cc, ((0, mp - cc.shape[0]), (0, 0)))
    mod_all = _modulation(cc, w_mod, b_mod).reshape(depth, mp, 6, d)

    tq_tab, tk_tab = _rope_tables(n_prompt, bs, ts, rope, dh)
    x = jnp.concatenate([x_prompt.reshape(n_prompt, d), x_sample.reshape(n_sample, d)], axis=0)
    scale = (dh + rope) ** -0.5

    ckv_l, kpe_l, sf_l, sb_l = [], [], [], []
    for l in range(depth):
        mod = mod_all[l]
        wl = w_in[l]
        kpe_cols = wl[:, o_kpe:o_kpe + rope]
        w_in_r = jnp.concatenate(
            [wl[:, :o_zf], wl[:, o_cq:o_kpe], wl[:, o_ma:], kpe_cols, _rot_half_cols(kpe_cols),
             wl[:, o_zf:o_cq], jnp.zeros((d, LANES - 2 * gate_rank), F32)], axis=1).astype(BF16)
        assert w_in_r.shape[1] == n_in
        zpad = jnp.zeros((LANES - 2 * gate_rank, gla_qk), F32)
        wgf = jnp.concatenate([gla_w_gate_fwd[l], jnp.zeros((gate_rank, gla_qk), F32), zpad], axis=0)
        wgb = jnp.concatenate([jnp.zeros((gate_rank, gla_qk), F32), gla_w_gate_bwd[l], zpad], axis=0)
        wq3 = mla_w_uq[l].reshape(q_rank, MLA_HEADS, dh + rope)
        zq = jnp.zeros((q_rank, MLA_HEADS, dh - rope), F32)
        wq = jnp.concatenate([wq3, zq], axis=-1).reshape(q_rank, -1).astype(BF16)
        wqr = jnp.concatenate([jnp.zeros((q_rank, MLA_HEADS, dh), F32), _rot_half_cols(wq3[..., dh:]), zq],
                              axis=-1).reshape(q_rank, -1).astype(BF16)
        wkv = jnp.concatenate([mla_w_uk[l], mla_w_uv[l]], axis=1).astype(BF16)
        rw = jnp.pad(router_w[l], ((0, 0), (0, LANES - n_exp)))
        rb = jnp.pad(router_b[l], (0, LANES - n_exp)).reshape(1, LANES)

        hmat = _gemm_in(x, mod, w_in_r, n_prompt, ts)

        conv_args = (hmat, c_ca, c_cg, conv_w_dw[l], conv_b_dw[l], conv_ln_g[l], conv_ln_b[l])
        ya = (_conv_branch(*conv_args, row_off=0, n_seq=bp, seq=tp),
              _conv_branch(*conv_args, row_off=n_prompt, n_seq=bs, seq=ts))

        gla_args = (hmat, (c_gq, c_gk, c_gv, c_gr, c_z), wgf, gla_b_gate_fwd[l], wgb, gla_b_gate_bwd[l],
                    gla_norm_g[l])
        gla_kw = dict(heads=gla_heads, dk=gla_dk, dv=gla_dv)
        og_p, s_f, s_b = _gla_branch(*gla_args, row_off=0, n_seq=bp, seq=tp, **gla_kw)
        og_s = _gla_branch(*gla_args, row_off=n_prompt, n_seq=bs, seq=ts, **gla_kw,
                           state=(state_gla_fwd[:, l], state_gla_bwd[:, l]))

        q_all, ckv, kpe = _mla_proj(hmat, c_cq, c_ckv, c_pe, mla_q_norm_g[l], mla_kv_norm_g[l], wq, wqr,
                                    tq_tab, tk_tab)
        kv_p = _gemm(ckv, wkv, rows=n_prompt, name="mla_kv").reshape(bp, tp, -1)
        keys_s = jnp.concatenate([cache_ckv[:, l], ckv[n_prompt:].reshape(bs, ts, kv_rank)], axis=1)
        kv_s = _gemm(keys_s.reshape(bs * (past + ts), kv_rank), wkv, name="mla_kv")
        kv_s = kv_s.reshape(bs, past + ts, -1)
        kpe_p = kpe[:n_prompt].reshape(bp, tp, LANES)
        kpe_s = jnp.concatenate([jnp.pad(cache_kpe[:, l], ((0, 0), (0, 0), (0, LANES - rope))),
                                 kpe[n_prompt:].reshape(bs, ts, LANES)], axis=1)
        att_kw = dict(heads=MLA_HEADS, dh=dh, scale=scale)
        att = (_attention(q_all, kv_p, kpe_p, row_off=0, n_seq=bp, seq=tp, **att_kw),
               _attention(q_all, kv_s, kpe_s, row_off=n_prompt, n_seq=bs, seq=ts, **att_kw))
        merged = _merge([ya, (og_p, og_s), att],
                        [conv_w_out[l].astype(BF16), gla_w_out[l].astype(BF16), mla_w_out[l].astype(BF16)],
                        hmat, c_ma)

        x1, h2, route_i, route_g, counts = _post_attn(merged, x, w_out[l].astype(BF16), mod, ln1_g[l], ln1_b[l],
                                                      rw, rb, n_prompt, ts, alpha, n_exp)

        dest, buf_tok, blk_e, blk_valid = _moe_plan(route_i, counts, n_exp)
        yb = _moe_experts(h2, buf_tok, blk_e, blk_valid, l, exp_w_gate, exp_b_gate, exp_w_up, exp_b_up, exp_w_down,
                          exp_b_down)
        x = _moe_combine(yb, dest, route_g, x1, mod, ln2_g[l], ln2_b[l], n_prompt, ts, alpha)

        ckv_l.append(ckv[:n_prompt].reshape(bp, tp, kv_rank))
        kpe_l.append(kpe_p[..., :rope])
        sf_l.append(s_f)
        sb_l.append(s_b)

    y_prompt = x[:n_prompt].reshape(bp, tp, d)
    y_sample = x[n_prompt:].reshape(bs, ts, d)
    return (y_prompt, y_sample, jnp.stack(ckv_l, axis=1), jnp.stack(kpe_l, axis=1),
            jnp.stack(sf_l, axis=1), jnp.stack(sb_l, axis=1))
```

```python
import functools

import jax
import jax.numpy as jnp
from jax import lax
from jax.experimental import pallas as pl
from jax.experimental.pallas import tpu as pltpu

F32 = jnp.float32
BF16 = jnp.bfloat16

LANES = 128
VMEM_LIMIT_BYTES = 56 * 1024 * 1024
GEMM_IN_VMEM_LIMIT_BYTES = 60 * 1024 * 1024

GRID_W = 64
GLA_TAU = 16.0
GLA_CHUNK = 64
MLA_HEADS = 8
ROPE_BASE = 10000.0
TOP_K = 4
SWIGLU_LIMIT = 7.0
SWIGLU_ALPHA = 1.702
LN_EPS = 1e-5
RMS_EPS = 1e-6

CONV_PAD = 16
MOE_ROWS = 1024
MOE_SUB = 512


def _params(*sem, vmem=VMEM_LIMIT_BYTES):
    return pltpu.CompilerParams(dimension_semantics=sem, vmem_limit_bytes=vmem)


def _tile(n, pref):
    if n <= pref:
        return n
    t = (pref // LANES) * LANES
    while t > LANES and n % t:
        t -= LANES
    assert n % t == 0, (n, pref)
    return t


def _ln(x):
    mu = jnp.mean(x, axis=-1, keepdims=True)
    xc = x - mu
    return xc * lax.rsqrt(jnp.mean(xc * xc, axis=-1, keepdims=True) + LN_EPS)


def _rms(x):
    return x * lax.rsqrt(jnp.mean(x * x, axis=-1, keepdims=True) + RMS_EPS)


def _silu(x):
    return x * jax.nn.sigmoid(x)


def _dot(a, b):
    return jnp.dot(a, b, preferred_element_type=F32)


def _dot_nt(a, b):
    return lax.dot_general(a, b, (((1,), (1,)), ((), ())), preferred_element_type=F32)


def _dot_tn(a, b):
    return lax.dot_general(a, b, (((0,), (0,)), ((), ())), preferred_element_type=F32)


def _mod_kernel(c_ref, w_ref, b_ref, o_ref):
    s = _silu(c_ref[...])
    o_ref[0] = _dot(s.astype(BF16), w_ref[0].astype(BF16)) + b_ref[0]


def _modulation(cc, w_mod, b_mod):
    n_layers, d, d6 = w_mod.shape
    mp = cc.shape[0]
    tn = _tile(d6, 1024)
    return pl.pallas_call(
        _mod_kernel,
        grid=(n_layers, d6 // tn),
        in_specs=[
            pl.BlockSpec((mp, d), lambda l, j: (0, 0)),
            pl.BlockSpec((1, d, tn), lambda l, j: (l, 0, j)),
            pl.BlockSpec((1, 1, tn), lambda l, j: (l, 0, j)),
        ],
        out_specs=pl.BlockSpec((1, mp, tn), lambda l, j: (l, 0, j)),
        out_shape=jax.ShapeDtypeStruct((n_layers, mp, d6), F32),
        compiler_params=_params("parallel", "parallel"),
        name="modulation",
    )(cc, w_mod, b_mod.reshape(n_layers, 1, d6))


def _mod_row(i, tm, n_prompt, t_sample):
    r = i * tm
    return jnp.where(r < n_prompt, 0, 1 + (r - n_prompt) // t_sample)


def _gemm_in_kernel(x_ref, mod_ref, w_ref, o_ref, h_ref):
    @pl.when(pl.program_id(1) == 0)
    def _():
        y = _ln(x_ref[...])
        h_ref[...] = (y * (1.0 + mod_ref[0, 1:2, :]) + mod_ref[0, 0:1, :]).astype(BF16)

    o_ref[...] = _dot(h_ref[...], w_ref[...])


def _gemm_in(x, mod, w, n_prompt, t_sample, tm=1024, tn=1792):
    m, d = x.shape
    n = w.shape[1]
    tm = min(tm, t_sample, n_prompt)
    tn = _tile(n, tn)
    assert m % tm == 0 and n_prompt % tm == 0 and t_sample % tm == 0
    return pl.pallas_call(
        _gemm_in_kernel,
        grid=(m // tm, n // tn),
        in_specs=[
            pl.BlockSpec((tm, d), lambda i, j: (i, 0)),
            pl.BlockSpec((1, 6, d), lambda i, j: (_mod_row(i, tm, n_prompt, t_sample), 0, 0)),
            pl.BlockSpec((d, tn), lambda i, j: (0, j)),
        ],
        out_specs=pl.BlockSpec((tm, tn), lambda i, j: (i, j)),
        out_shape=jax.ShapeDtypeStruct((m, n), F32),
        scratch_shapes=[pltpu.VMEM((tm, d), BF16)],
        compiler_params=_params("parallel", "arbitrary", vmem=GEMM_IN_VMEM_LIMIT_BYTES),
        name="gemm_in",
    )(x, mod, w)


def _gemm_kernel(x_ref, w_ref, o_ref):
    o_ref[...] = _dot(x_ref[...].astype(BF16), w_ref[...]).astype(o_ref.dtype)


def _gemm(x, w, *, rows=None, tm=512, tn=1024, name="gemm"):
    k, n = w.shape
    m = x.shape[0] if rows is None else rows
    tm = min(tm, m)
    tn = _tile(n, tn)
    assert m % tm == 0 and x.shape[1] == k
    return pl.pallas_call(
        _gemm_kernel,
        grid=(m // tm, n // tn),
        in_specs=[pl.BlockSpec((tm, k), lambda i, j: (i, 0)), pl.BlockSpec((k, tn), lambda i, j: (0, j))],
        out_specs=pl.BlockSpec((tm, tn), lambda i, j: (i, j)),
        out_shape=jax.ShapeDtypeStruct((m, n), BF16),
        compiler_params=_params("parallel", "parallel"),
        name=name,
    )(x, w)


def _merge_kernel(*refs, n_br, nb_first):
    x_refs = refs[:2 * n_br]
    w_refs = refs[2 * n_br:3 * n_br]
    g_refs = refs[3 * n_br:4 * n_br]
    o_ref = refs[4 * n_br]

    def compute(group):
        acc = None
        for b in range(n_br):
            term = jax.nn.sigmoid(g_refs[b][...]) * _dot(x_refs[2 * b + group][...], w_refs[b][...])
            acc = term if acc is None else acc + term
        o_ref[...] = acc.astype(o_ref.dtype)

    i = pl.program_id(0)
    pl.when(i < nb_first)(lambda: compute(0))
    pl.when(i >= nb_first)(lambda: compute(1))


def _merge(ys, ws, gate, gate_col, tm=512, tn=1024):
    n_br = len(ys)
    n = ws[0].shape[1]
    m = ys[0][0].shape[0] + ys[0][1].shape[0]
    tm = min(tm, ys[0][0].shape[0], ys[0][1].shape[0])
    tn = _tile(n, tn)
    nb_first = ys[0][0].shape[0] // tm
    assert gate_col % tn == 0 and n % tn == 0 and all(y[0].shape[0] % tm == 0 and y[1].shape[0] % tm == 0 for y in ys)
    in_specs, args = [], []
    for y, w in zip(ys, ws):
        assert w.shape == (y[0].shape[1], n) and y[1].shape[1] == w.shape[0]
        in_specs += [pl.BlockSpec((tm, w.shape[0]), lambda i, j: (jnp.minimum(i, nb_first - 1), 0)),
                     pl.BlockSpec((tm, w.shape[0]), lambda i, j: (jnp.maximum(i - nb_first, 0), 0))]
        args += [y[0], y[1]]
    in_specs += [pl.BlockSpec((w.shape[0], tn), lambda i, j: (0, j)) for w in ws]
    args += list(ws)
    for b in range(n_br):
        goff = (gate_col + b * n) // tn
        in_specs.append(pl.BlockSpec((tm, tn), lambda i, j, goff=goff: (i, goff + j)))
        args.append(gate)
    return pl.pallas_call(
        functools.partial(_merge_kernel, n_br=n_br, nb_first=nb_first),
        grid=(m // tm, n // tn),
        in_specs=in_specs,
        out_specs=pl.BlockSpec((tm, tn), lambda i, j: (i, j)),
        out_shape=jax.ShapeDtypeStruct((m, n), BF16),
        compiler_params=_params("parallel", "parallel"),
        name="merge_out",
    )(*args)


def _conv_kernel(a_ref, g_ref, wdw_ref, bdw_ref, lng_ref, lnb_ref, o_ref, upad_ref, yc_ref, *, seq, rb, rc, width):
    c = a_ref.shape[1]
    r = pl.program_id(1)
    fill_rows = min(seq, 64)

    @pl.when(r == 0)
    def _():
        zeros = jnp.zeros((CONV_PAD, c), F32)
        upad_ref[0:CONV_PAD, :] = zeros
        upad_ref[CONV_PAD + seq:CONV_PAD + seq + CONV_PAD, :] = zeros

        def fill(i, carry):
            src = pl.ds(pl.multiple_of(i * fill_rows, 8), fill_rows)
            dst = pl.ds(pl.multiple_of(CONV_PAD + i * fill_rows, 8), fill_rows)
            upad_ref[dst, :] = a_ref[src, :] * jax.nn.sigmoid(g_ref[src, :])
            return carry

        lax.fori_loop(0, seq // fill_rows, fill, 0)

    lead = CONV_PAD - width // 2
    n_shift = 8
    n_al = (lead + width - 1) // n_shift + 1
    win = rc + n_shift * n_al

    def row_chunk(ci, carry):
        base = pl.multiple_of(r * rb + ci * rc, 8)
        for cj in range(c // LANES):
            lanes = slice(cj * LANES, (cj + 1) * LANES)
            w_all = upad_ref[pl.ds(base, win), lanes]
            acc = jnp.zeros((rc, LANES), F32)
            for b in range(n_shift):
                taps = [(a8, n_shift * a8 + b - lead) for a8 in range(n_al)]
                taps = [(a8, k) for a8, k in taps if 0 <= k < width]
                if not taps:
                    continue
                w_b = w_all[b:b + rc + n_shift * (n_al - 1)]
                part = None
                for a8, k in taps:
                    term = wdw_ref[k:k + 1, lanes] * w_b[n_shift * a8:n_shift * a8 + rc]
                    part = term if part is None else part + term
                acc = acc + part
            yc_ref[pl.ds(pl.multiple_of(ci * rc, 8), rc), lanes] = acc + bdw_ref[:, lanes]
        return carry

    lax.fori_loop(0, rb // rc, row_chunk, 0)
    y = _ln(yc_ref[...]) * lng_ref[...] + lnb_ref[...]
    o_ref[...] = _silu(y).astype(o_ref.dtype)


def _conv_branch(hmat, col_a, col_g, w_dw, b_dw, ln_g, ln_b, *, row_off, n_seq, seq):
    width, c = w_dw.shape
    rb = min(seq, 256)
    rc = min(rb, 64)
    assert width // 2 <= CONV_PAD and row_off % seq == 0 and col_a % c == 0 and col_g % c == 0
    s_off = row_off // seq
    in_specs = [
        pl.BlockSpec((seq, c), lambda s, r: (s_off + s, col_a // c)),
        pl.BlockSpec((seq, c), lambda s, r: (s_off + s, col_g // c)),
        pl.BlockSpec((width, c), lambda s, r: (0, 0)),
        pl.BlockSpec((1, c), lambda s, r: (0, 0)),
        pl.BlockSpec((1, c), lambda s, r: (0, 0)),
        pl.BlockSpec((1, c), lambda s, r: (0, 0)),
    ]
    args = [hmat, hmat, w_dw, b_dw.reshape(1, c), ln_g.reshape(1, c), ln_b.reshape(1, c)]
    return pl.pallas_call(
        functools.partial(_conv_kernel, seq=seq, rb=rb, rc=rc, width=width),
        grid=(n_seq, seq // rb),
        in_specs=in_specs,
        out_specs=pl.BlockSpec((rb, c), lambda s, r: (s * (seq // rb) + r, 0)),
        out_shape=jax.ShapeDtypeStruct((n_seq * seq, c), BF16),
        scratch_shapes=[pltpu.VMEM((seq + 2 * CONV_PAD, c), F32), pltpu.VMEM((rb, c), F32)],
        compiler_params=_params("parallel", "arbitrary"),
        name="conv_branch",
    )(*args)


def _cumsum3(tri, la):
    hi = la.astype(BF16)
    r1 = la - hi.astype(F32)
    mid = r1.astype(BF16)
    lo = (r1 - mid.astype(F32)).astype(BF16)
    return _dot(tri, hi) + _dot(tri, mid) + _dot(tri, lo)


def _log_sigmoid(x):
    return jnp.minimum(x, 0.0) - jnp.log(1.0 + jnp.exp(-jnp.abs(x)))


def _gla_kernel(*refs, seq, chunk, has_state):
    it = iter(refs)
    q_ref, k_ref, v_ref, r_ref, z_ref = (next(it) for _ in range(5))
    wgf_ref, bgf_ref, wgb_ref, bgb_ref, ng_ref = (next(it) for _ in range(5))
    if has_state:
        s0f_ref, s0b_ref = next(it), next(it)
        o_ref = next(it)
        sf_ref = sb_ref = None
    else:
        o_ref, sf_ref, sb_ref = next(it), next(it), next(it)
    laf_ref, lab_ref, oacc_ref, kvf_ref, kvb_ref, qdf_ref, qdb_ref, decf_ref, decb_ref = (next(it) for _ in range(9))

    dk = q_ref.shape[1]
    dv = v_ref.shape[1]
    n_chunks = seq // chunk
    z = z_ref[...].astype(BF16)
    laf_ref[...] = _log_sigmoid(_dot(z, wgf_ref[...].astype(BF16)) + bgf_ref[...]) / GLA_TAU
    lab_ref[...] = _log_sigmoid(_dot(z, wgb_ref[...].astype(BF16)) + bgb_ref[...]) / GLA_TAU

    row = lax.broadcasted_iota(jnp.int32, (chunk, chunk), 0)
    col = lax.broadcasted_iota(jnp.int32, (chunk, chunk), 1)
    lower = row >= col
    upper = row <= col
    tri_f = jnp.where(lower, 1.0, 0.0).astype(BF16)
    tri_b = jnp.where(upper, 1.0, 0.0).astype(BF16)
    q_scale = dk ** -0.5

    def intra(n, rows, la_ref, tri, mask, edge, qd_ref, kv_ref, dec_ref):
        b = _cumsum3(tri, la_ref[rows, :])
        b_edge = b[edge:edge + 1, :]
        q = q_ref[rows, :] * q_scale
        k = k_ref[rows, :]
        v = v_ref[rows, :].astype(BF16)
        qd = (q * jnp.exp(b)).astype(BF16)
        kd = (k * jnp.exp(-b)).astype(BF16)
        ke = (k * jnp.exp(b_edge - b)).astype(BF16)
        att = jnp.where(mask, _dot_nt(qd, kd), 0.0).astype(BF16)
        qd_ref[rows, :] = qd
        kv_ref[n] = _dot_tn(v, ke)
        dec_ref[n] = jnp.broadcast_to(jnp.exp(b_edge), dec_ref.shape[1:])
        return _dot(att, v)

    def phase1(n, carry):
        rows = pl.ds(pl.multiple_of(n * chunk, chunk), chunk)
        oacc_ref[rows, :] = (intra(n, rows, laf_ref, tri_f, lower, chunk - 1, qdf_ref, kvf_ref, decf_ref)
                             + intra(n, rows, lab_ref, tri_b, upper, 0, qdb_ref, kvb_ref, decb_ref))
        return carry

    lax.fori_loop(0, n_chunks, phase1, 0, unroll=4)

    def scan(order, kv_ref, dec_ref, s0):
        def body(j, s):
            n = order(j)
            kv = kv_ref[n]
            kv_ref[n] = s
            return s * dec_ref[n, 0:1, :] + kv

        return lax.fori_loop(0, n_chunks, body, s0)

    zero = jnp.zeros((dv, dk), F32)
    s_f = scan(lambda j: j, kvf_ref, decf_ref, s0f_ref[0, 0].T if has_state else zero)
    s_b = scan(lambda j: n_chunks - 1 - j, kvb_ref, decb_ref, s0b_ref[0, 0].T if has_state else zero)
    if not has_state:
        sf_ref[0, 0] = s_f.T
        sb_ref[0, 0] = s_b.T

    def phase3(n, carry):
        rows = pl.ds(pl.multiple_of(n * chunk, chunk), chunk)
        o = (oacc_ref[rows, :] + _dot_nt(qdf_ref[rows, :], kvf_ref[n].astype(BF16))
             + _dot_nt(qdb_ref[rows, :], kvb_ref[n].astype(BF16)))
        o = _rms(o) * ng_ref[...]
        o_ref[rows, :] = (o * _silu(r_ref[rows, :])).astype(o_ref.dtype)
        return carry

    lax.fori_loop(0, n_chunks, phase3, 0, unroll=4)


def _gla_branch(hmat, cols, wgf, bgf, wgb, bgb, norm_g, *, heads, dk, dv, row_off, n_seq, seq, state=None):
    col_q, col_k, col_v, col_r, col_z = cols
    s_off = row_off // seq
    has_state = state is not None
    in_specs = [
        pl.BlockSpec((seq, dk), lambda s, h: (s_off + s, col_q // dk + h)),
        pl.BlockSpec((seq, dk), lambda s, h: (s_off + s, col_k // dk + h)),
        pl.BlockSpec((seq, dv), lambda s, h: (s_off + s, col_v // dv + h)),
        pl.BlockSpec((seq, dv), lambda s, h: (s_off + s, col_r // dv + h)),
        pl.BlockSpec((seq, LANES), lambda s, h: (s_off + s, col_z // LANES)),
        pl.BlockSpec((LANES, dk), lambda s, h: (0, h)),
        pl.BlockSpec((1, dk), lambda s, h: (0, h)),
        pl.BlockSpec((LANES, dk), lambda s, h: (0, h)),
        pl.BlockSpec((1, dk), lambda s, h: (0, h)),
        pl.BlockSpec((1, dv), lambda s, h: (0, h)),
    ]
    args = [hmat] * 5 + [wgf, bgf.reshape(1, -1), wgb, bgb.reshape(1, -1), norm_g.reshape(1, -1)]
    o_spec = pl.BlockSpec((seq, dv), lambda s, h: (s, h))
    o_shape = jax.ShapeDtypeStruct((n_seq * seq, heads * dv), BF16)
    if has_state:
        st_spec = pl.BlockSpec((1, 1, dk, dv), lambda s, h: (s, h, 0, 0))
        in_specs += [st_spec, st_spec]
        args += [state[0], state[1]]
        out_specs, out_shape = o_spec, o_shape
    else:
        st_spec = pl.BlockSpec((1, 1, dk, dv), lambda s, h: (s, h, 0, 0))
        st_shape = jax.ShapeDtypeStruct((n_seq, heads, dk, dv), F32)
        out_specs, out_shape = [o_spec, st_spec, st_spec], [o_shape, st_shape, st_shape]
    chunk = min(GLA_CHUNK, seq)
    n_chunks = seq // chunk
    return pl.pallas_call(
        functools.partial(_gla_kernel, seq=seq, chunk=chunk, has_state=has_state),
        grid=(n_seq, heads),
        in_specs=in_specs,
        out_specs=out_specs,
        out_shape=out_shape,
        scratch_shapes=[pltpu.VMEM((seq, dk), F32), pltpu.VMEM((seq, dk), F32), pltpu.VMEM((seq, dv), F32),
                        pltpu.VMEM((n_chunks, dv, dk), F32), pltpu.VMEM((n_chunks, dv, dk), F32),
                        pltpu.VMEM((seq, dk), BF16), pltpu.VMEM((seq, dk), BF16),
                        pltpu.VMEM((n_chunks, 8, dk), F32), pltpu.VMEM((n_chunks, 8, dk), F32)],
        compiler_params=_params("parallel", "parallel"),
        name="gla_branch",
    )(*args)


def _mla_proj_kernel(cq_ref, ckv_ref, pe_ref, qg_ref, kvg_ref, wq_ref, wqr_ref, tq_ref, tk_ref,
                     q_ref, ckvn_ref, kpe_ref, *, hw):
    cqn = (_rms(cq_ref[...]) * qg_ref[...]).astype(BF16)
    a = _dot(cqn, wq_ref[...])
    ar = _dot(cqn, wqr_ref[...])
    cos = tq_ref[:, 0:hw]
    sin = tq_ref[:, hw:2 * hw]
    for h in range(a.shape[1] // hw):
        sl = slice(h * hw, (h + 1) * hw)
        q_ref[:, sl] = (a[:, sl] * cos + ar[:, sl] * sin).astype(q_ref.dtype)
    ckvn_ref[...] = _rms(ckv_ref[...]) * kvg_ref[...]
    v = pe_ref[...] * tk_ref[...]
    v = v + pltpu.roll(v, LANES // 2, axis=1)
    lane = lax.broadcasted_iota(jnp.int32, v.shape, 1)
    kpe_ref[...] = jnp.where(lane < LANES // 2, v, 0.0)


def _mla_proj(hmat, col_cq, col_ckv, col_pe, q_g, kv_g, wq, wqr, tq, tk, tm=512):
    m = hmat.shape[0]
    rq = q_g.shape[0]
    rkv = kv_g.shape[0]
    nq = wq.shape[1]
    hw = tq.shape[1] // 2
    tm = min(tm, m)
    return pl.pallas_call(
        functools.partial(_mla_proj_kernel, hw=hw),
        grid=(m // tm,),
        in_specs=[
            pl.BlockSpec((tm, rq), lambda i: (i, col_cq // rq)),
            pl.BlockSpec((tm, rkv), lambda i: (i, col_ckv // rkv)),
            pl.BlockSpec((tm, LANES), lambda i: (i, col_pe // LANES)),
            pl.BlockSpec((1, rq), lambda i: (0, 0)),
            pl.BlockSpec((1, rkv), lambda i: (0, 0)),
            pl.BlockSpec((rq, nq), lambda i: (0, 0)),
            pl.BlockSpec((rq, nq), lambda i: (0, 0)),
            pl.BlockSpec((tm, 2 * hw), lambda i: (i, 0)),
            pl.BlockSpec((tm, LANES), lambda i: (i, 0)),
        ],
        out_specs=[
            pl.BlockSpec((tm, nq), lambda i: (i, 0)),
            pl.BlockSpec((tm, rkv), lambda i: (i, 0)),
            pl.BlockSpec((tm, LANES), lambda i: (i, 0)),
        ],
        out_shape=[
            jax.ShapeDtypeStruct((m, nq), BF16),
            jax.ShapeDtypeStruct((m, rkv), F32),
            jax.ShapeDtypeStruct((m, LANES), F32),
        ],
        compiler_params=_params("parallel"),
        name="mla_proj",
    )(hmat, hmat, hmat, q_g.reshape(1, rq), kv_g.reshape(1, rkv), wq, wqr, tq, tk)


def _attn_kernel(*refs, heads, hw, dh, scale):
    q_ref, kv_ref, kpe_ref = refs[:3]
    o_ref, kcat_ref = refs[-2:]

    @pl.when(pl.program_id(1) == 0)
    def _():
        kp = kpe_ref[0].astype(BF16)
        for h in range(heads):
            kcat_ref[:, h * hw:h * hw + dh] = kv_ref[0, :, h * dh:(h + 1) * dh]
            kcat_ref[:, h * hw + dh:(h + 1) * hw] = kp

    v_off = heads * dh
    for h in range(heads):
        s = _dot_nt(q_ref[:, h * hw:(h + 1) * hw], kcat_ref[:, h * hw:(h + 1) * hw]) * scale
        p = jnp.exp(s - jnp.max(s, axis=-1, keepdims=True))
        l = jnp.sum(p, axis=-1, keepdims=True)
        o = _dot(p.astype(BF16), kv_ref[0, :, v_off + h * dh:v_off + (h + 1) * dh]) / l
        o_ref[:, h * dh:(h + 1) * dh] = o.astype(o_ref.dtype)


def _attention(q_all, kv, kpe, *, heads, dh, scale, row_off, n_seq, seq, tq=256):
    s_len = kv.shape[1]
    hw = q_all.shape[1] // heads
    tq = min(tq, seq)
    q_off = row_off // tq
    in_specs = [
        pl.BlockSpec((tq, heads * hw), lambda b, i: (q_off + b * (seq // tq) + i, 0)),
        pl.BlockSpec((1, s_len, 2 * heads * dh), lambda b, i: (b, 0, 0)),
        pl.BlockSpec((1, s_len, LANES), lambda b, i: (b, 0, 0)),
    ]
    args = [q_all, kv, kpe]
    return pl.pallas_call(
        functools.partial(_attn_kernel, heads=heads, hw=hw, dh=dh, scale=scale),
        grid=(n_seq, seq // tq),
        in_specs=in_specs,
        out_specs=pl.BlockSpec((tq, heads * dh), lambda b, i: (b * (seq // tq) + i, 0)),
        out_shape=jax.ShapeDtypeStruct((n_seq * seq, heads * dh), BF16),
        scratch_shapes=[pltpu.VMEM((s_len, heads * hw), BF16)],
        compiler_params=_params("parallel", "arbitrary"),
        name="mla_attention",
    )(*args)


def _split3(x):
    hi = x.astype(BF16)
    lo = (x - hi.astype(F32)).astype(BF16)
    return hi, lo


def _post_attn_kernel(m_ref, x_ref, w_ref, mod_ref, g1_ref, b1_ref, rw_ref, rb_ref,
                      x1_ref, h2_ref, ri_ref, rg_ref, cnt_ref, run_ref, tri_ref, *, alpha, n_exp):
    tm = x_ref.shape[0]
    step = pl.program_id(0)

    @pl.when(step == 0)
    def _():
        run_ref[...] = jnp.zeros_like(run_ref)
        row = lax.broadcasted_iota(jnp.int32, (tm, tm), 0)
        col = lax.broadcasted_iota(jnp.int32, (tm, tm), 1)
        tri_ref[...] = jnp.where(row > col, 1.0, 0.0).astype(BF16)

    y = _dot(m_ref[...], w_ref[...])
    x1 = _ln(alpha * x_ref[...] + mod_ref[0, 2:3, :] * y) * g1_ref[...] + b1_ref[...]
    x1_ref[...] = x1
    h2 = _ln(x1) * (1.0 + mod_ref[0, 4:5, :]) + mod_ref[0, 3:4, :]
    half = h2_ref.shape[1]
    hi = pltpu.bitcast(h2[:, :half].astype(BF16).astype(F32), jnp.uint32)
    lo = pltpu.bitcast(h2[:, half:].astype(BF16).astype(F32), jnp.uint32)
    h2_ref[...] = hi | (lo >> 16)

    h_hi, h_lo = _split3(h2)
    w_hi, w_lo = _split3(rw_ref[...])
    logits = _dot(h_hi, w_hi) + _dot(h_lo, w_hi) + _dot(h_hi, w_lo) + rb_ref[...]
    lane = lax.broadcasted_iota(jnp.int32, logits.shape, 1).astype(F32)
    neg = jnp.float32(-jnp.inf)
    logits = jnp.where(lane < n_exp, logits, neg)

    counts = jnp.zeros(logits.shape, F32)
    vals, idxs = [], []
    for _ in range(TOP_K):
        mx = jnp.max(logits, axis=-1, keepdims=True)
        idx = jnp.min(jnp.where(logits == mx, lane, float(LANES)), axis=-1, keepdims=True)
        hit = lane == idx
        counts = counts + jnp.where(hit, 1.0, 0.0)
        logits = jnp.where(hit, neg, logits)
        vals.append(mx)
        idxs.append(idx)

    es = [jnp.exp(v - vals[0]) for v in vals]
    denom = es[0]
    for e in es[1:]:
        denom = denom + e
    before = _dot(tri_ref[...], counts.astype(BF16)) + run_ref[...]
    out_i = jnp.zeros(logits.shape, jnp.int32)
    out_g = jnp.zeros(logits.shape, F32)
    for k in range(TOP_K):
        rank = jnp.sum(jnp.where(lane == idxs[k], before, 0.0), axis=-1, keepdims=True)
        out_i = jnp.where(lane == k, idxs[k].astype(jnp.int32), out_i)
        out_i = jnp.where(lane == TOP_K + k, rank.astype(jnp.int32), out_i)
        out_g = jnp.where(lane == k, es[k] / denom, out_g)
    ri_ref[...] = out_i[:, 0:2 * TOP_K]
    rg_ref[...] = out_g[:, 0:TOP_K]
    run_ref[...] = run_ref[...] + jnp.sum(counts, axis=0, keepdims=True)
    cnt_ref[...] = run_ref[...]


def _post_attn(merged, x, w_out, mod, ln_g, ln_b, router_w, router_b, n_prompt, t_sample, alpha, n_exp, tm=512):
    m, d = x.shape
    tm = min(tm, t_sample, n_prompt)
    row_spec = pl.BlockSpec((tm, d), lambda i: (i, 0))
    vec_spec = pl.BlockSpec((1, d), lambda i: (0, 0))
    return pl.pallas_call(
        functools.partial(_post_attn_kernel, alpha=alpha, n_exp=n_exp),
        grid=(m // tm,),
        in_specs=[
            row_spec, row_spec,
            pl.BlockSpec((d, d), lambda i: (0, 0)),
            pl.BlockSpec((1, 6, d), lambda i: (_mod_row(i, tm, n_prompt, t_sample), 0, 0)),
            vec_spec, vec_spec,
            pl.BlockSpec((d, LANES), lambda i: (0, 0)),
            pl.BlockSpec((1, LANES), lambda i: (0, 0)),
        ],
        out_specs=[
            row_spec,
            pl.BlockSpec((tm, d // 2), lambda i: (i, 0)),
            pl.BlockSpec((tm, 2 * TOP_K), lambda i: (i, 0)),
            pl.BlockSpec((tm, TOP_K), lambda i: (i, 0)),
            pl.BlockSpec((1, LANES), lambda i: (0, 0)),
        ],
        out_shape=[
            jax.ShapeDtypeStruct((m, d), F32),
            jax.ShapeDtypeStruct((m, d // 2), jnp.uint32),
            jax.ShapeDtypeStruct((m, 2 * TOP_K), jnp.int32),
            jax.ShapeDtypeStruct((m, TOP_K), F32),
            jax.ShapeDtypeStruct((1, LANES), F32),
        ],
        scratch_shapes=[pltpu.VMEM((1, LANES), F32), pltpu.VMEM((tm, tm), BF16)],
        compiler_params=_params("arbitrary"),
        name="post_attn_router",
    )(merged, x, w_out, mod, ln_g.reshape(1, d), ln_b.reshape(1, d), router_w, router_b)


def _wait_slot(buf_ref, sem, slot):
    pltpu.make_async_copy(buf_ref.at[slot], buf_ref.at[slot], sem.at[slot]).wait()


def _moe_expert_kernel(be_ref, bv_ref, idx_ref, nxt_ref, h_hbm, wg_ref, bg_ref, wu_ref, bu_ref, wd_ref, bd_ref,
                       o_ref, xbuf_ref, xb_ref, sem, *, n_f):
    i = pl.program_id(0)
    f = pl.program_id(1)
    n_rows, half = xbuf_ref.shape[1], xbuf_ref.shape[2]
    share = n_rows // n_f
    valid = bv_ref[i]
    slot = i % 2
    del be_ref

    def row_copy(ids_ref, s, r):
        return pltpu.make_async_copy(h_hbm.at[pl.ds(ids_ref[0, 0, r], 1), :], xbuf_ref.at[s, pl.ds(r, 1), :],
                                     sem.at[s])

    @pl.when(jnp.logical_and(jnp.logical_and(i == 0, f == 0), valid > 0))
    def _():
        def body(r, carry):
            row_copy(idx_ref, 0, r).start()
            return carry

        lax.fori_loop(0, n_rows, body, 0, unroll=8)

    @pl.when(f == 0)
    def _():
        o_ref[...] = jnp.broadcast_to(bd_ref[0, 0], o_ref.shape)

    rows_in_flight = jnp.where(i == 0, valid > 0, bv_ref[jnp.maximum(i - 1, 0)] > 0)

    @pl.when(jnp.logical_and(f == 0, rows_in_flight))
    def _():
        _wait_slot(xbuf_ref, sem, slot)
        u = xbuf_ref[slot]
        xb_ref[:, :half] = pltpu.bitcast(u & jnp.uint32(0xFFFF0000), F32).astype(BF16)
        xb_ref[:, half:] = pltpu.bitcast(u << 16, F32).astype(BF16)

    for sb in range(n_rows // MOE_SUB):
        rows = slice(sb * MOE_SUB, (sb + 1) * MOE_SUB)

        @pl.when(sb * MOE_SUB < valid)
        def _():
            if sb == 0:
                for j in range(share):
                    row_copy(nxt_ref, 1 - slot, f * share + j).start(priority=j % 2)
            x = xb_ref[rows, :]
            gt = jnp.minimum(_dot(x, wg_ref[0, 0].astype(BF16)) + bg_ref[0, 0], SWIGLU_LIMIT)
            up = jnp.clip(_dot(x, wu_ref[0, 0].astype(BF16)) + bu_ref[0, 0], -SWIGLU_LIMIT, SWIGLU_LIMIT)
            act = gt * jax.nn.sigmoid(SWIGLU_ALPHA * gt) * (up + 1.0)
            o_ref[rows, :] = o_ref[rows, :] + _dot(act.astype(BF16), wd_ref[0, 0].astype(BF16))


def _moe_experts(h2p, buf_tok, blk_e, blk_valid, layer, w_gate, b_gate, w_up, b_up, w_down, b_down, tf=256):
    cap = buf_tok.shape[0]
    half = h2p.shape[1]
    n_layers, n_exp, d, ff = w_gate.shape
    assert d == 2 * half
    tf = _tile(ff, tf)
    nf = ff // tf
    nblk = cap // MOE_ROWS
    assert MOE_ROWS % nf == 0
    ids = buf_tok.reshape(nblk, 1, MOE_ROWS)

    def f_idx(i, f, bv):
        return jnp.where(bv[i] > 0, f, nf - 1)

    return pl.pallas_call(
        functools.partial(_moe_expert_kernel, n_f=nf),
        grid_spec=pltpu.PrefetchScalarGridSpec(
            num_scalar_prefetch=2,
            grid=(nblk, nf),
            in_specs=[
                pl.BlockSpec((1, 1, MOE_ROWS), lambda i, f, be, bv: (i, 0, 0), memory_space=pltpu.SMEM),
                pl.BlockSpec((1, 1, MOE_ROWS), lambda i, f, be, bv: (jnp.minimum(i + 1, nblk - 1), 0, 0),
                             memory_space=pltpu.SMEM),
                pl.BlockSpec(memory_space=pl.ANY),
                pl.BlockSpec((1, 1, d, tf), lambda i, f, be, bv: (layer, be[i], 0, f_idx(i, f, bv))),
                pl.BlockSpec((1, 1, 1, tf), lambda i, f, be, bv: (layer, be[i], 0, f_idx(i, f, bv))),
                pl.BlockSpec((1, 1, d, tf), lambda i, f, be, bv: (layer, be[i], 0, f_idx(i, f, bv))),
                pl.BlockSpec((1, 1, 1, tf), lambda i, f, be, bv: (layer, be[i], 0, f_idx(i, f, bv))),
                pl.BlockSpec((1, 1, tf, d), lambda i, f, be, bv: (layer, be[i], f_idx(i, f, bv), 0)),
                pl.BlockSpec((1, 1, 1, d), lambda i, f, be, bv: (layer, be[i], 0, 0)),
            ],
            out_specs=pl.BlockSpec((MOE_ROWS, d), lambda i, f, be, bv: (i, 0)),
            scratch_shapes=[pltpu.VMEM((2, MOE_ROWS, half), jnp.uint32), pltpu.VMEM((MOE_ROWS, d), BF16),
                            pltpu.SemaphoreType.DMA((2,))],
        ),
        out_shape=jax.ShapeDtypeStruct((cap, d), F32),
        compiler_params=_params("arbitrary", "arbitrary"),
        name="moe_experts",
    )(blk_e, blk_valid, ids, ids, h2p, w_gate, b_gate.reshape(n_layers, n_exp, 1, ff), w_up, b_up.reshape(n_layers, n_exp, 1, ff),
      w_down, b_down.reshape(n_layers, n_exp, 1, d))


def _moe_combine_kernel(idx_ref, nxt_ref, yb_hbm, g_ref, x_ref, mod_ref, lg_ref, lb_ref, o_ref, buf_ref, sem, *,
                        alpha):
    tc = x_ref.shape[0]
    blk = pl.program_id(0)
    slot = blk % 2

    def issue(ids_ref, s):
        def body(r, carry):
            for k in range(TOP_K):
                pltpu.make_async_copy(yb_hbm.at[pl.ds(ids_ref[0, 0, r * TOP_K + k], 1), :],
                                      buf_ref.at[s, k, pl.ds(r, 1), :], sem.at[s]).start()
            return carry

        lax.fori_loop(0, tc, body, 0, unroll=2)

    @pl.when(blk == 0)
    def _():
        issue(idx_ref, 0)

    @pl.when(blk + 1 < pl.num_programs(0))
    def _():
        issue(nxt_ref, 1 - slot)

    _wait_slot(buf_ref, sem, slot)
    g = g_ref[...]
    y = g[:, 0:1] * buf_ref[slot, 0]
    for k in range(1, TOP_K):
        y = y + g[:, k:k + 1] * buf_ref[slot, k]
    x2 = _ln(alpha * x_ref[...] + mod_ref[0, 5:6, :] * y) * lg_ref[...] + lb_ref[...]
    o_ref[...] = x2


def _moe_combine(yb, dest, gates, x1, mod, ln_g, ln_b, n_prompt, t_sample, alpha, tc=128):
    m, d = x1.shape
    tc = min(tc, t_sample, n_prompt)
    nb = m // tc
    ids = dest.reshape(nb, 1, tc * TOP_K)
    return pl.pallas_call(
        functools.partial(_moe_combine_kernel, alpha=alpha),
        grid=(nb,),
        in_specs=[
            pl.BlockSpec((1, 1, tc * TOP_K), lambda i: (i, 0, 0), memory_space=pltpu.SMEM),
            pl.BlockSpec((1, 1, tc * TOP_K), lambda i: (jnp.minimum(i + 1, nb - 1), 0, 0), memory_space=pltpu.SMEM),
            pl.BlockSpec(memory_space=pl.ANY),
            pl.BlockSpec((tc, TOP_K), lambda i: (i, 0)),
            pl.BlockSpec((tc, d), lambda i: (i, 0)),
            pl.BlockSpec((1, 6, d), lambda i: (_mod_row(i, tc, n_prompt, t_sample), 0, 0)),
            pl.BlockSpec((1, d), lambda i: (0, 0)),
            pl.BlockSpec((1, d), lambda i: (0, 0)),
        ],
        out_specs=pl.BlockSpec((tc, d), lambda i: (i, 0)),
        out_shape=jax.ShapeDtypeStruct((m, d), F32),
        scratch_shapes=[pltpu.VMEM((2, TOP_K, tc, d), F32), pltpu.SemaphoreType.DMA((2,))],
        compiler_params=_params("arbitrary"),
        name="moe_combine",
    )(ids, ids, yb, gates, x1, mod, ln_g.reshape(1, d), ln_b.reshape(1, d))


def _moe_plan(route_i, counts, n_exp):
    m = route_i.shape[0]
    top_e = route_i[:, :TOP_K]
    rank = route_i[:, TOP_K:]
    counts = counts[0, :n_exp].astype(jnp.int32)
    padded = (counts + MOE_ROWS - 1) // MOE_ROWS * MOE_ROWS
    pend = jnp.cumsum(padded)
    pstart = pend - padded
    dest = pstart[top_e] + rank
    nblk = (m * TOP_K + n_exp * (MOE_ROWS - 1)) // MOE_ROWS + 1
    blk_row = jnp.arange(nblk, dtype=jnp.int32) * MOE_ROWS
    n_used = pend[-1] // MOE_ROWS
    blk_e = jnp.minimum(jnp.searchsorted(pend, blk_row, side="right"), n_exp - 1).astype(jnp.int32)
    blk_valid = jnp.clip(counts[blk_e] - (blk_row - pstart[blk_e]), 0, MOE_ROWS).astype(jnp.int32)
    used = jnp.arange(nblk, dtype=jnp.int32) < n_used
    blk_valid = jnp.where(used, blk_valid, 0)
    blk_e = jnp.where(used, blk_e, blk_e[jnp.maximum(n_used - 1, 0)])
    tok = jnp.broadcast_to(jnp.arange(m, dtype=jnp.int32)[:, None], (m, TOP_K))
    buf_tok = jnp.zeros((nblk * MOE_ROWS,), jnp.int32).at[dest.reshape(-1)].set(tok.reshape(-1))
    return dest.astype(jnp.int32), buf_tok, blk_e, blk_valid


def _rot_half_cols(w):
    w1, w2, w3, w4 = jnp.split(w, 4, axis=-1)
    return jnp.concatenate([-w2, w1, -w4, w3], axis=-1)


def _rope_tables(n_prompt, n_batch, t_sample, rope, dh):
    rows = t_sample // GRID_W
    row = jnp.repeat(jnp.arange(rows), GRID_W).astype(F32)
    col = jnp.tile(jnp.arange(GRID_W), rows).astype(F32)
    n_freq = rope // 4
    inv = ROPE_BASE ** (-jnp.arange(n_freq, dtype=F32) / n_freq)
    ar = row[:, None] * inv
    ac = col[:, None] * inv
    ang = jnp.concatenate([ar, ar, ac, ac], axis=-1)
    cos = jnp.concatenate([jnp.ones((n_prompt, rope), F32), jnp.tile(jnp.cos(ang), (n_batch, 1))], axis=0)
    sin = jnp.concatenate([jnp.zeros((n_prompt, rope), F32), jnp.tile(jnp.sin(ang), (n_batch, 1))], axis=0)
    n = cos.shape[0]
    hw = 2 * dh
    pad = jnp.zeros((n, hw - dh - rope), F32)
    tq = jnp.concatenate([jnp.ones((n, dh), F32), cos, pad, jnp.zeros((n, dh), F32), sin, pad], axis=-1)
    tk = jnp.concatenate([cos, sin], axis=-1)
    return tq, tk


def kernel(x_prompt, x_sample, c, cache_ckv, cache_kpe, state_gla_fwd, state_gla_bwd, c_ctx, w_mod, b_mod, w_in,
           conv_w_dw, conv_b_dw, conv_ln_g, conv_ln_b, conv_w_out, gla_w_gate_fwd, gla_b_gate_fwd,
           gla_w_gate_bwd, gla_b_gate_bwd, gla_norm_g, gla_w_out, mla_q_norm_g, mla_w_uq, mla_kv_norm_g,
           mla_w_uk, mla_w_uv, mla_w_out, w_out, ln1_g, ln1_b, router_w, router_b, exp_w_gate, exp_b_gate,
           exp_w_up, exp_b_up, exp_w_down, exp_b_down, ln2_g, ln2_b):
    bp, tp, d = x_prompt.shape
    bs, ts, _ = x_sample.shape
    depth = w_in.shape[0]
    n_prompt, n_sample = bp * tp, bs * ts
    n_tok = n_prompt + n_sample
    conv_dim = conv_w_dw.shape[2]
    gla_heads, gla_dk, gla_dv = state_gla_fwd.shape[2:]
    gla_qk, gla_v = gla_heads * gla_dk, gla_heads * gla_dv
    gate_rank = gla_w_gate_fwd.shape[1]
    q_rank = mla_w_uq.shape[1]
    kv_rank = mla_w_uk.shape[1]
    rope = cache_kpe.shape[3]
    dh = mla_w_uk.shape[2] // MLA_HEADS
    n_exp = router_w.shape[2]
    past = cache_ckv.shape[2]
    alpha = (2 * depth) ** 0.25
    assert 2 * dh == 2 * LANES and rope == LANES // 2 and 2 * gate_rank <= LANES

    sizes = (conv_dim, conv_dim, gla_qk, gla_qk, gla_v, gla_v, gate_rank, gate_rank, q_rank, kv_rank, rope, d, d, d)
    offs = [0]
    for s in sizes:
        offs.append(offs[-1] + s)
    o_zf, o_cq, o_kpe, o_ma = offs[6], offs[8], offs[10], offs[11]
    c_ca, c_cg, c_gq, c_gk, c_gv, c_gr = offs[0], offs[1], offs[2], offs[3], offs[4], offs[5]
    c_cq = o_zf
    c_ckv = c_cq + q_rank
    c_ma = c_ckv + kv_rank
    c_pe = c_ma + 3 * d
    c_z = c_pe + LANES
    n_in = c_z + LANES

    cc = jnp.concatenate([c_ctx[None, :], c], axis=0)
    mp = (cc.shape[0] + 7) // 8 * 8
    cc = jnp.pad(cc, ((0, mp - cc.shape[0]), (0, 0)))
    mod_all = _modulation(cc, w_mod, b_mod).reshape(depth, mp, 6, d)

    tq_tab, tk_tab = _rope_tables(n_prompt, bs, ts, rope, dh)
    x = jnp.concatenate([x_prompt.reshape(n_prompt, d), x_sample.reshape(n_sample, d)], axis=0)
    scale = (dh + rope) ** -0.5

    ckv_l, kpe_l, sf_l, sb_l = [], [], [], []
    for l in range(depth):
        mod = mod_all[l]
        wl = w_in[l]
        kpe_cols = wl[:, o_kpe:o_kpe + rope]
        w_in_r = jnp.concatenate(
            [wl[:, :o_zf], wl[:, o_cq:o_kpe], wl[:, o_ma:], kpe_cols, _rot_half_cols(kpe_cols),
             wl[:, o_zf:o_cq], jnp.zeros((d, LANES - 2 * gate_rank), F32)], axis=1).astype(BF16)
        assert w_in_r.shape[1] == n_in
        zpad = jnp.zeros((LANES - 2 * gate_rank, gla_qk), F32)
        wgf = jnp.concatenate([gla_w_gate_fwd[l], jnp.zeros((gate_rank, gla_qk), F32), zpad], axis=0)
        wgb = jnp.concatenate([jnp.zeros((gate_rank, gla_qk), F32), gla_w_gate_bwd[l], zpad], axis=0)
        wq3 = mla_w_uq[l].reshape(q_rank, MLA_HEADS, dh + rope)
        zq = jnp.zeros((q_rank, MLA_HEADS, dh - rope), F32)
        wq = jnp.concatenate([wq3, zq], axis=-1).reshape(q_rank, -1).astype(BF16)
        wqr = jnp.concatenate([jnp.zeros((q_rank, MLA_HEADS, dh), F32), _rot_half_cols(wq3[..., dh:]), zq],
                              axis=-1).reshape(q_rank, -1).astype(BF16)
        wkv = jnp.concatenate([mla_w_uk[l], mla_w_uv[l]], axis=1).astype(BF16)
        rw = jnp.pad(router_w[l], ((0, 0), (0, LANES - n_exp)))
        rb = jnp.pad(router_b[l], (0, LANES - n_exp)).reshape(1, LANES)

        hmat = _gemm_in(x, mod, w_in_r, n_prompt, ts)

        conv_args = (hmat, c_ca, c_cg, conv_w_dw[l], conv_b_dw[l], conv_ln_g[l], conv_ln_b[l])
        ya = (_conv_branch(*conv_args, row_off=0, n_seq=bp, seq=tp),
              _conv_branch(*conv_args, row_off=n_prompt, n_seq=bs, seq=ts))

        gla_args = (hmat, (c_gq, c_gk, c_gv, c_gr, c_z), wgf, gla_b_gate_fwd[l], wgb, gla_b_gate_bwd[l],
                    gla_norm_g[l])
        gla_kw = dict(heads=gla_heads, dk=gla_dk, dv=gla_dv)
        og_p, s_f, s_b = _gla_branch(*gla_args, row_off=0, n_seq=bp, seq=tp, **gla_kw)
        og_s = _gla_branch(*gla_args, row_off=n_prompt, n_seq=bs, seq=ts, **gla_kw,
                           state=(state_gla_fwd[:, l], state_gla_bwd[:, l]))

        q_all, ckv, kpe = _mla_proj(hmat, c_cq, c_ckv, c_pe, mla_q_norm_g[l], mla_kv_norm_g[l], wq, wqr,
                                    tq_tab, tk_tab)
        kv_p = _gemm(ckv, wkv, rows=n_prompt, name="mla_kv").reshape(bp, tp, -1)
        keys_s = jnp.concatenate([cache_ckv[:, l], ckv[n_prompt:].reshape(bs, ts, kv_rank)], axis=1)
        kv_s = _gemm(keys_s.reshape(bs * (past + ts), kv_rank), wkv, name="mla_kv")
        kv_s = kv_s.reshape(bs, past + ts, -1)
        kpe_p = kpe[:n_prompt].reshape(bp, tp, LANES)
        kpe_s = jnp.concatenate([jnp.pad(cache_kpe[:, l], ((0, 0), (0, 0), (0, LANES - rope))),
                                 kpe[n_prompt:].reshape(bs, ts, LANES)], axis=1)
        att_kw = dict(heads=MLA_HEADS, dh=dh, scale=scale)
        att = (_attention(q_all, kv_p, kpe_p, row_off=0, n_seq=bp, seq=tp, **att_kw),
               _attention(q_all, kv_s, kpe_s, row_off=n_prompt, n_seq=bs, seq=ts, **att_kw))
        merged = _merge([ya, (og_p, og_s), att],
                        [conv_w_out[l].astype(BF16), gla_w_out[l].astype(BF16), mla_w_out[l].astype(BF16)],
                        hmat, c_ma)

        x1, h2, route_i, route_g, counts = _post_attn(merged, x, w_out[l].astype(BF16), mod, ln1_g[l], ln1_b[l],
                                                      rw, rb, n_prompt, ts, alpha, n_exp)

        dest, buf_tok, blk_e, blk_valid = _moe_plan(route_i, counts, n_exp)
        yb = _moe_experts(h2, buf_tok, blk_e, blk_valid, l, exp_w_gate, exp_b_gate, exp_w_up, exp_b_up, exp_w_down,
                          exp_b_down)
        x = _moe_combine(yb, dest, route_g, x1, mod, ln2_g[l], ln2_b[l], n_prompt, ts, alpha)

        ckv_l.append(ckv[:n_prompt].reshape(bp, tp, kv_rank))
        kpe_l.append(kpe_p[..., :rope])
        sf_l.append(s_f)
        sb_l.append(s_b)

    y_prompt = x[:n_prompt].reshape(bp, tp, d)
    y_sample = x[n_prompt:].reshape(bs, ts, d)
    return (y_prompt, y_sample, jnp.stack(ckv_l, axis=1), jnp.stack(kpe_l, axis=1),
            jnp.stack(sf_l, axis=1), jnp.stack(sb_l, axis=1))
```
